```python
import math
import jax
import jax.numpy as jnp
from jax import lax
import numpy as np

D_MODEL = 1024
BATCH = 2
SEQ = 8192
DEPTH = 1
DEC_BATCH = 128
DEC_SEQ = 1
PAST_LEN = 2048
PAGE_SIZE = 128

SSD_D_INNER = 2 * D_MODEL
SSD_HEAD_DIM = 64
SSD_HEADS = SSD_D_INNER // SSD_HEAD_DIM
SSD_GROUPS = 4
SSD_HPG = SSD_HEADS // SSD_GROUPS
SSD_STATE = 128
SSD_CONV = 4
SSD_CHUNK = 128
SSD_CONV_CH = SSD_D_INNER + 2 * SSD_GROUPS * SSD_STATE
NSA_HEAD_DIM = 64
NSA_HEADS = D_MODEL // NSA_HEAD_DIM
NSA_KV_HEADS = 4
NSA_HPG = NSA_HEADS // NSA_KV_HEADS
CMP_BLOCK = 32
CMP_STRIDE = 16
CMP_HIDDEN = 2 * NSA_HEAD_DIM
SLC_BLOCK = 64
SLC_TOPK = 16
WINDOW = 512
Q_BLOCK = 128
N_EXPERTS = 64
TOP_K = 8
N_ROUTE_GROUPS = 8
TOPK_ROUTE_GROUPS = 4
EXPERT_FF = 256
SHARED_FF = 256
ROUTE_SCALE = 2.5
MOE_BLOCK = 128
ALPHA = (2.0 * DEPTH) ** 0.25
BETA = (8.0 * DEPTH) ** -0.25
NORM_EPS = 1e-5
IN_SPLITS = (SSD_D_INNER, SSD_CONV_CH, SSD_HEADS, NSA_HEADS * NSA_HEAD_DIM,
             6 * NSA_KV_HEADS * NSA_HEAD_DIM, 3 * NSA_HEADS, 2 * D_MODEL)
IN_OFFSETS = tuple(int(v) for v in np.cumsum(IN_SPLITS)[:-1])
N_IN = int(sum(IN_SPLITS))

kernel_name = 'hybrid_ssd_nsa_moe_decode_step'


def _alibi_slopes():
    h = np.arange(1, NSA_HEADS + 1, dtype=np.float32)
    return (2.0 ** (-8.0 * h / NSA_HEADS)).astype(np.float32)


def _layer_norm(x, g, b):
    xf = x.astype(jnp.float32)
    mu = jnp.mean(xf, -1, keepdims=True)
    var = jnp.mean(jnp.square(xf - mu), -1, keepdims=True)
    return ((xf - mu) * lax.rsqrt(var + NORM_EPS) * g + b).astype(x.dtype)


def _last_rows(x, n):
    L = x.shape[1]
    if L >= n:
        return x[:, L - n:]
    pad = [(0, 0)] * x.ndim
    pad[1] = (n - L, 0)
    return jnp.pad(x, pad)


def _masked_softmax(s, mask):
    s = jnp.where(mask, s, -jnp.inf)
    mx = jnp.max(s, axis=-1, keepdims=True)
    mx = jnp.where(jnp.isfinite(mx), mx, 0.0)
    p = jnp.exp(s - mx)
    return p / jnp.maximum(jnp.sum(p, -1, keepdims=True), 1e-30)


def _project_in(x, w_in, b_in):
    b, L, _ = x.shape
    h = x @ w_in + b_in
    z, xbc, dt_raw, q, kv, g_nsa, g_merge = jnp.split(h, list(IN_OFFSETS), axis=-1)
    q = q.reshape(b, L, NSA_KV_HEADS, NSA_HPG, NSA_HEAD_DIM)
    kv = kv.reshape(b, L, 6, NSA_KV_HEADS, NSA_HEAD_DIM)
    g_nsa = jax.nn.sigmoid(g_nsa).reshape(b, L, NSA_KV_HEADS, NSA_HPG, 3)
    return z, xbc, dt_raw, q, kv, g_nsa, g_merge


def _causal_conv(xh, w, bias):
    out = lax.conv_general_dilated(xh, w[:, None, :].astype(xh.dtype), window_strides=(1,), padding='VALID',
                                   dimension_numbers=('NWC', 'WIO', 'NWC'), feature_group_count=xh.shape[-1])
    return jax.nn.silu(out + bias.astype(xh.dtype))


def _ssd_scan(x, dt, a, bm, cm, init, chunk):
    f32 = jnp.float32
    b, L = x.shape[:2]
    nc = L // chunk
    xr = x.reshape(b, nc, chunk, SSD_GROUPS, SSD_HPG, SSD_HEAD_DIM).astype(f32)
    dtr = dt.reshape(b, nc, chunk, SSD_GROUPS, SSD_HPG)
    br = bm.reshape(b, nc, chunk, SSD_GROUPS, SSD_STATE).astype(f32)
    cr = cm.reshape(b, nc, chunk, SSD_GROUPS, SSD_STATE).astype(f32)
    a_cs = jnp.cumsum(dtr * a.reshape(SSD_GROUPS, SSD_HPG), axis=2)
    xdt = xr * dtr[..., None]
    tri = np.tril(np.ones((chunk, chunk), dtype=bool))[None, None, :, :, None, None]
    lmat = jnp.exp(jnp.where(tri, a_cs[:, :, :, None] - a_cs[:, :, None, :], -jnp.inf))
    cb = jnp.einsum('bclgn,bcsgn->bclsg', cr, br)
    y_diag = jnp.einsum('bclsgj,bcsgjp->bclgjp', cb[..., None] * lmat, xdt)
    chunk_states = jnp.einsum('bclgn,bclgjp->bcgjpn', br, xdt * jnp.exp(a_cs[:, :, -1:] - a_cs)[..., None])
    chunk_decay = jnp.exp(a_cs[:, :, -1])

    def step(h, inp):
        st, dec = inp
        return h * dec[..., None, None] + st, h

    h0 = init.astype(f32).reshape(b, SSD_GROUPS, SSD_HPG, SSD_HEAD_DIM, SSD_STATE)
    h_fin, h_in = lax.scan(step, h0, (jnp.moveaxis(chunk_states, 1, 0), jnp.moveaxis(chunk_decay, 1, 0)))
    h_in = jnp.moveaxis(h_in, 0, 1)
    y_off = jnp.einsum('bclgn,bcgjpn->bclgjp', cr, h_in) * jnp.exp(a_cs)[..., None]
    y = (y_diag + y_off).reshape(b, L, SSD_HEADS, SSD_HEAD_DIM).astype(x.dtype)
    return y, h_fin.reshape(b, SSD_HEADS, SSD_HEAD_DIM, SSD_STATE).astype(init.dtype)


def _ssd_branch(xbc_hist, z, dt_raw, init, chunk, ssd_w):
    conv_w, conv_b, dt_bias, a_log, d_skip, norm_w = ssd_w
    b, L, _ = z.shape
    xbc = _causal_conv(xbc_hist, conv_w, conv_b)
    xs, bs, cs = jnp.split(xbc, [SSD_D_INNER, SSD_D_INNER + SSD_GROUPS * SSD_STATE], axis=-1)
    xs = xs.reshape(b, L, SSD_HEADS, SSD_HEAD_DIM)
    bs = bs.reshape(b, L, SSD_GROUPS, SSD_STATE)
    cs = cs.reshape(b, L, SSD_GROUPS, SSD_STATE)
    dt = jax.nn.softplus(dt_raw.astype(jnp.float32) + dt_bias.astype(jnp.float32))
    a = -jnp.exp(a_log.astype(jnp.float32))
    y, st = _ssd_scan(xs, dt, a, bs, cs, init, chunk)
    y = y + d_skip[:, None].astype(y.dtype) * xs
    y = y.reshape(b, L, SSD_D_INNER) * jax.nn.silu(z)
    yg = y.reshape(b, L, SSD_GROUPS, SSD_D_INNER // SSD_GROUPS).astype(jnp.float32)
    yg = yg * lax.rsqrt(jnp.mean(jnp.square(yg), -1, keepdims=True) + NORM_EPS)
    return (yg.reshape(b, L, SSD_D_INNER) * norm_w).astype(z.dtype), st


def _compress(rows, cmp, which):
    w1, b1, w2, b2, pe = (p[which] for p in cmp)
    T = rows.shape[1]
    nc = (T - CMP_BLOCK) // CMP_STRIDE + 1
    idx = np.arange(nc, dtype=np.int32)[:, None] * CMP_STRIDE + np.arange(CMP_BLOCK, dtype=np.int32)[None, :]
    blk = rows[:, idx] + pe[None, None, :, None, :]
    hid = jax.nn.silu(jnp.einsum('bnlgd,lde->bnge', blk, w1) + b1)
    return hid @ w2 + b2


def _to_sel_blocks(rows):
    b, T = rows.shape[:2]
    ns = -(-T // SLC_BLOCK)
    rows = jnp.pad(rows, ((0, 0), (0, ns * SLC_BLOCK - T), (0, 0), (0, 0)))
    return rows.reshape(b, ns, SLC_BLOCK, NSA_KV_HEADS, NSA_HEAD_DIM).transpose(0, 3, 1, 2, 4)


def _overlap(nc, ns):
    i = np.arange(nc)[:, None] * CMP_STRIDE
    m = np.arange(ns)[None, :] * SLC_BLOCK
    return ((i < m + SLC_BLOCK) & (i + CMP_BLOCK > m)).astype(np.float32)


def _nsa_core(q, qpos, gates, kc, vc, ksb, vsb, kw, vw, wpos):
    f32 = jnp.float32
    scale = NSA_HEAD_DIM ** -0.5
    slopes = jnp.asarray(_alibi_slopes().reshape(NSA_KV_HEADS, NSA_HPG))[None, :, :, None, None]
    bsz, lq = q.shape[0], q.shape[1]
    nc, ns = kc.shape[1], ksb.shape[2]
    ends = np.arange(nc, dtype=np.int32) * CMP_STRIDE + (CMP_BLOCK - 1)
    d_c = qpos[:, None] - ends[None, :]
    s_c = jnp.einsum('bqgjd,bngd->bgjqn', q, kc).astype(f32) * scale - slopes * d_c.astype(f32)
    p_c = _masked_softmax(s_c, d_c >= 0)
    o_c = jnp.einsum('bgjqn,bngd->bqgjd', p_c.astype(vc.dtype), vc)
    imp = jnp.einsum('bgjqn,nm->bgqm', p_c, jnp.asarray(_overlap(nc, ns)))
    m = np.arange(ns, dtype=np.int32)
    valid = (m * SLC_BLOCK)[None, :] <= qpos[:, None]
    cur = (qpos // SLC_BLOCK)[:, None]
    forced = (m[None, :] == 0) | (m[None, :] == cur) | (m[None, :] == cur - 1)
    imp = jnp.where(forced & valid, jnp.inf, imp)
    imp = jnp.where(valid, imp, -jnp.inf)
    top_v, top_i = lax.top_k(imp, min(SLC_TOPK, ns))
    gather = jax.vmap(jax.vmap(lambda blk, i: blk[i]))
    ks = gather(ksb, top_i)
    vs = gather(vsb, top_i)
    kpos = top_i[..., None] * SLC_BLOCK + np.arange(SLC_BLOCK, dtype=np.int32)
    d_s = qpos[None, None, :, None, None] - kpos
    mask_s = (d_s >= 0) & (top_v > -jnp.inf)[..., None]
    s_s = jnp.einsum('bqgjd,bgqkrd->bgjqkr', q, ks).astype(f32) * scale - slopes[..., None] * d_s[:, :, None].astype(f32)
    kk = top_i.shape[-1] * SLC_BLOCK
    p_s = _masked_softmax(s_s.reshape(bsz, NSA_KV_HEADS, NSA_HPG, lq, kk), mask_s.reshape(bsz, NSA_KV_HEADS, 1, lq, kk))
    o_s = jnp.einsum('bgjqt,bgqtd->bqgjd', p_s.astype(vs.dtype), vs.reshape(bsz, NSA_KV_HEADS, lq, kk, NSA_HEAD_DIM))
    d_w = qpos[:, None] - wpos[None, :]
    mask_w = (d_w >= 0) & (d_w < WINDOW) & (wpos >= 0)[None, :]
    s_w = jnp.einsum('bqgjd,bwgd->bgjqw', q, kw).astype(f32) * scale - slopes * d_w.astype(f32)
    p_w = _masked_softmax(s_w, mask_w)
    o_w = jnp.einsum('bgjqw,bwgd->bqgjd', p_w.astype(vw.dtype), vw)
    return gates[..., 0:1] * o_c + gates[..., 1:2] * o_s + gates[..., 2:3] * o_w


def _nsa_prompt(q, kv, gates, cmp):
    b, S = q.shape[:2]
    kc = _compress(kv[:, :, 0], cmp, 0)
    vc = _compress(kv[:, :, 1], cmp, 1)
    ksb = _to_sel_blocks(kv[:, :, 2])
    vsb = _to_sel_blocks(kv[:, :, 3])
    kw = jnp.pad(kv[:, :, 4], ((0, 0), (WINDOW, 0), (0, 0), (0, 0)))
    vw = jnp.pad(kv[:, :, 5], ((0, 0), (WINDOW, 0), (0, 0), (0, 0)))
    nqb = S // Q_BLOCK
    qb = q.reshape(b, nqb, Q_BLOCK, NSA_KV_HEADS, NSA_HPG, NSA_HEAD_DIM).swapaxes(0, 1)
    gb = gates.reshape(b, nqb, Q_BLOCK, NSA_KV_HEADS, NSA_HPG, 3).swapaxes(0, 1)

    def one_block(args):
        q_i, g_i, i = args
        start = i * Q_BLOCK
        qpos = start + jnp.arange(Q_BLOCK, dtype=jnp.int32)
        wpos = start - WINDOW + jnp.arange(Q_BLOCK + WINDOW, dtype=jnp.int32)
        kw_i = lax.dynamic_slice_in_dim(kw, start, Q_BLOCK + WINDOW, axis=1)
        vw_i = lax.dynamic_slice_in_dim(vw, start, Q_BLOCK + WINDOW, axis=1)
        return _nsa_core(q_i, qpos, g_i, kc, vc, ksb, vsb, kw_i, vw_i, wpos)

    o = lax.map(one_block, (qb, gb, jnp.arange(nqb, dtype=jnp.int32)))
    return o.swapaxes(0, 1).reshape(b, S, NSA_HEADS * NSA_HEAD_DIM)


def _nsa_sample(q, kv, gates, cache_kv_paged, page_table, cache_kv_win, cmp):
    db, ds = q.shape[:2]
    past = page_table.shape[1] * PAGE_SIZE
    hist = cache_kv_paged[page_table].reshape(db, past, 4, NSA_KV_HEADS, NSA_HEAD_DIM)
    rows = jnp.concatenate([hist, kv[:, :, :4].astype(hist.dtype)], axis=1)
    kc = _compress(rows[:, :, 0], cmp, 0)
    vc = _compress(rows[:, :, 1], cmp, 1)
    ksb = _to_sel_blocks(rows[:, :, 2])
    vsb = _to_sel_blocks(rows[:, :, 3])
    wbuf = cache_kv_win.shape[1]
    win = jnp.concatenate([cache_kv_win, kv[:, :, 4:].astype(cache_kv_win.dtype)], axis=1)
    qpos = past + jnp.arange(ds, dtype=jnp.int32)
    wpos = past - wbuf + jnp.arange(wbuf + ds, dtype=jnp.int32)
    o = _nsa_core(q, qpos, gates, kc, vc, ksb, vsb, win[:, :, 0], win[:, :, 1], wpos)
    return o.reshape(db, ds, NSA_HEADS * NSA_HEAD_DIM), win[:, -wbuf:]


def _moe(x, w_router, b_router, w_e1, w_e3, w_e2, w_s1, w_s3, w_s2):
    t, d = x.shape
    scores = jax.nn.sigmoid((x @ w_router).astype(jnp.float32))
    biased = scores + b_router.astype(jnp.float32)
    grp = biased.reshape(t, N_ROUTE_GROUPS, N_EXPERTS // N_ROUTE_GROUPS)
    grp_score = jnp.sum(lax.top_k(grp, 2)[0], -1)
    _, grp_idx = lax.top_k(grp_score, TOPK_ROUTE_GROUPS)
    grp_keep = jnp.sum(jax.nn.one_hot(grp_idx, N_ROUTE_GROUPS, dtype=jnp.float32), 1) > 0
    masked = jnp.where(grp_keep[:, :, None], grp, -jnp.inf).reshape(t, N_EXPERTS)
    _, e_idx = lax.top_k(masked, TOP_K)
    gate = jnp.take_along_axis(scores, e_idx, axis=1)
    gate = gate / jnp.sum(gate, -1, keepdims=True) * ROUTE_SCALE
    n_asg = t * TOP_K
    e_flat = e_idx.reshape(-1)
    tok_flat = jnp.repeat(jnp.arange(t, dtype=jnp.int32), TOP_K)
    order = jnp.argsort(e_flat)
    e_s, tok_s, g_s = e_flat[order], tok_flat[order], gate.reshape(-1)[order]
    counts = jnp.bincount(e_flat, length=N_EXPERTS)
    starts = jnp.cumsum(counts) - counts
    padded = (counts + MOE_BLOCK - 1) // MOE_BLOCK * MOE_BLOCK
    pad_ends = jnp.cumsum(padded)
    dest = (pad_ends - padded)[e_s] + jnp.arange(n_asg, dtype=jnp.int32) - starts[e_s]
    n_blk = -(-(n_asg + N_EXPERTS * (MOE_BLOCK - 1)) // MOE_BLOCK)
    n_rows = n_blk * MOE_BLOCK
    row_tok = jnp.full((n_rows,), t, jnp.int32).at[dest].set(tok_s)
    row_gate = jnp.zeros((n_rows,), jnp.float32).at[dest].set(g_s)
    blk_exp = jnp.minimum(jnp.searchsorted(pad_ends, jnp.arange(n_blk, dtype=jnp.int32) * MOE_BLOCK, side='right'), N_EXPERTS - 1)
    x_rows = jnp.concatenate([x, jnp.zeros((1, d), x.dtype)], 0)[row_tok].reshape(n_blk, MOE_BLOCK, d)

    def expert_block(args):
        xb, e = args
        return (jax.nn.silu(xb @ w_e1[e]) * (xb @ w_e3[e])) @ w_e2[e]

    out = lax.map(expert_block, (x_rows, blk_exp)).reshape(n_rows, d)
    routed = jnp.zeros((t + 1, d), x.dtype).at[row_tok].add((out * row_gate[:, None]).astype(x.dtype))[:t]
    shared = (jax.nn.silu(x @ w_s1) * (x @ w_s3)) @ w_s2
    return routed + shared


def _finish(x, ssd_y, nsa_y, g_merge, out_w):
    (w_ssd_down, w_nsa_down, w_out, b_out, ln1_g, ln1_b, w_router, b_router,
     w_e1, w_e3, w_e2, w_s1, w_s3, w_s2, ln2_g, ln2_b) = out_w
    g_ssd, g_nsa = jnp.split(jax.nn.sigmoid(g_merge), 2, axis=-1)
    mixed = g_ssd * (ssd_y @ w_ssd_down) + g_nsa * (nsa_y @ w_nsa_down)
    h = _layer_norm(ALPHA * x + (mixed @ w_out + b_out), ln1_g, ln1_b)
    b, L, d = h.shape
    f = _moe(h.reshape(b * L, d), w_router, b_router, w_e1, w_e3, w_e2, w_s1, w_s3, w_s2).reshape(b, L, d)
    return _layer_norm(ALPHA * h + f, ln2_g, ln2_b)


def setup_inputs(seed: int = 0) -> dict:
    key = jax.random.key(seed)
    ks = jax.random.split(key, 48)
    f32 = jnp.float32

    def nrm(k, shape, scale):
        return jax.random.normal(k, shape, f32) * scale

    n_pages = PAST_LEN // PAGE_SIZE
    n_phys = (DEC_BATCH * n_pages * 5 + 3) // 4
    wbuf = min(WINDOW, PAST_LEN)
    page_table = jax.random.permutation(ks[0], n_phys)[: DEC_BATCH * n_pages].reshape(DEC_BATCH, n_pages).astype(jnp.int32)
    dt0 = jnp.exp(jax.random.uniform(ks[1], (SSD_HEADS,), f32, math.log(1e-3), math.log(1e-1)))
    dt_bias = dt0 + jnp.log(-jnp.expm1(-dt0))
    a_log = jnp.log(jax.random.uniform(ks[2], (SSD_HEADS,), f32, 1.0, 16.0))
    kvh, dh = NSA_KV_HEADS, NSA_HEAD_DIM
    return {
        'x_prompt': nrm(ks[3], (BATCH, SEQ, D_MODEL), 1.0),
        'x_sample': nrm(ks[4], (DEC_BATCH, DEC_SEQ, D_MODEL), 1.0),
        'cache_kv_paged': nrm(ks[5], (n_phys, PAGE_SIZE, 4, kvh, dh), 1.0),
        'cache_kv_win': nrm(ks[6], (DEC_BATCH, wbuf, 2, kvh, dh), 1.0),
        'state_ssm': nrm(ks[7], (DEC_BATCH, SSD_HEADS, SSD_HEAD_DIM, SSD_STATE), 0.1),
        'state_conv': nrm(ks[8], (DEC_BATCH, SSD_CONV - 1, SSD_CONV_CH), 1.0),
        'page_table': page_table,
        'w_in': nrm(ks[9], (D_MODEL, N_IN), D_MODEL ** -0.5),
        'b_in': nrm(ks[10], (N_IN,), 0.02),
        'conv_w': nrm(ks[11], (SSD_CONV, SSD_CONV_CH), SSD_CONV ** -0.5),
        'conv_b': nrm(ks[12], (SSD_CONV_CH,), 0.02),
        'dt_bias': dt_bias,
        'a_log': a_log,
        'd_skip': 1.0 + nrm(ks[13], (SSD_HEADS,), 0.1),
        'ssd_norm_w': 1.0 + nrm(ks[14], (SSD_D_INNER,), 0.02),
        'cmp_w1': nrm(ks[15], (2, CMP_BLOCK, dh, CMP_HIDDEN), (CMP_BLOCK * dh) ** -0.5),
        'cmp_b1': nrm(ks[16], (2, CMP_HIDDEN), 0.02),
        'cmp_w2': nrm(ks[17], (2, CMP_HIDDEN, dh), CMP_HIDDEN ** -0.5),
        'cmp_b2': nrm(ks[18], (2, dh), 0.02),
        'cmp_pe': nrm(ks[19], (2, CMP_BLOCK, dh), 0.1),
        'w_ssd_down': nrm(ks[20], (SSD_D_INNER, D_MODEL), SSD_D_INNER ** -0.5),
        'w_nsa_down': nrm(ks[21], (NSA_HEADS * dh, D_MODEL), (NSA_HEADS * dh) ** -0.5),
        'w_out': nrm(ks[22], (D_MODEL, D_MODEL), BETA * D_MODEL ** -0.5),
        'b_out': nrm(ks[23], (D_MODEL,), 0.02),
        'ln1_g': 1.0 + nrm(ks[24], (D_MODEL,), 0.02),
        'ln1_b': nrm(ks[25], (D_MODEL,), 0.02),
        'w_router': nrm(ks[26], (D_MODEL, N_EXPERTS), D_MODEL ** -0.5),
        'b_router': nrm(ks[27], (N_EXPERTS,), 0.01),
        'w_e1': nrm(ks[28], (N_EXPERTS, D_MODEL, EXPERT_FF), D_MODEL ** -0.5),
        'w_e3': nrm(ks[29], (N_EXPERTS, D_MODEL, EXPERT_FF), D_MODEL ** -0.5),
        'w_e2': nrm(ks[30], (N_EXPERTS, EXPERT_FF, D_MODEL), BETA * EXPERT_FF ** -0.5),
        'w_s1': nrm(ks[31], (D_MODEL, SHARED_FF), D_MODEL ** -0.5),
        'w_s3': nrm(ks[32], (D_MODEL, SHARED_FF), D_MODEL ** -0.5),
        'w_s2': nrm(ks[33], (SHARED_FF, D_MODEL), BETA * SHARED_FF ** -0.5),
        'ln2_g': 1.0 + nrm(ks[34], (D_MODEL,), 0.02),
        'ln2_b': nrm(ks[35], (D_MODEL,), 0.02),
    }


def reference(x_prompt, x_sample, cache_kv_paged, cache_kv_win, state_ssm, state_conv, page_table,
              w_in, b_in, conv_w, conv_b, dt_bias, a_log, d_skip, ssd_norm_w,
              cmp_w1, cmp_b1, cmp_w2, cmp_b2, cmp_pe,
              w_ssd_down, w_nsa_down, w_out, b_out, ln1_g, ln1_b,
              w_router, b_router, w_e1, w_e3, w_e2, w_s1, w_s3, w_s2, ln2_g, ln2_b):
    cmp = (cmp_w1, cmp_b1, cmp_w2, cmp_b2, cmp_pe)
    ssd_w = (conv_w, conv_b, dt_bias, a_log, d_skip, ssd_norm_w)
    out_w = (w_ssd_down, w_nsa_down, w_out, b_out, ln1_g, ln1_b, w_router, b_router,
             w_e1, w_e3, w_e2, w_s1, w_s3, w_s2, ln2_g, ln2_b)
    wbuf = cache_kv_win.shape[1]
    yp, ys = x_prompt, x_sample
    for _ in range(DEPTH):
        z, xbc, dt_raw, q, kv, g_nsa, g_merge = _project_in(yp, w_in, b_in)
        xbc_hist = jnp.pad(xbc, ((0, 0), (SSD_CONV - 1, 0), (0, 0)))
        init = jnp.zeros((yp.shape[0], SSD_HEADS, SSD_HEAD_DIM, SSD_STATE), state_ssm.dtype)
        ssd_y, ssm_p = _ssd_branch(xbc_hist, z, dt_raw, init, SSD_CHUNK, ssd_w)
        nsa_y = _nsa_prompt(q, kv, g_nsa, cmp)
        conv_p = _last_rows(xbc, SSD_CONV - 1)
        kv_rows_p = kv[:, :, :4]
        win_p = _last_rows(kv[:, :, 4:], wbuf)
        yp = _finish(yp, ssd_y, nsa_y, g_merge, out_w)
        z, xbc, dt_raw, q, kv, g_nsa, g_merge = _project_in(ys, w_in, b_in)
        xbc_hist = jnp.concatenate([state_conv.astype(xbc.dtype), xbc], axis=1)
        ssd_y, ssm_s = _ssd_branch(xbc_hist, z, dt_raw, state_ssm, ys.shape[1], ssd_w)
        nsa_y, win_s = _nsa_sample(q, kv, g_nsa, cache_kv_paged, page_table, cache_kv_win, cmp)
        conv_s = xbc_hist[:, -(SSD_CONV - 1):]
        kv_rows_s = kv[:, :, :4]
        ys = _finish(ys, ssd_y, nsa_y, g_merge, out_w)
    return (yp, ys, kv_rows_p, kv_rows_s, win_p, win_s, ssm_p, ssm_s, conv_p, conv_s)
```

```python
import functools

import numpy as np
import jax
import jax.numpy as jnp
from jax import lax
from jax.experimental import pallas as pl
from jax.experimental.pallas import tpu as pltpu

F32 = jnp.float32
BF16 = jnp.bfloat16

D_MODEL = 1024
PAGE_SIZE = 128
SSD_D_INNER = 2048
SSD_HEAD_DIM = 64
SSD_HEADS = 32
SSD_GROUPS = 4
SSD_HPG = 8
SSD_STATE = 128
SSD_CONV = 4
SSD_CHUNK = 128
SSD_CONV_CH = SSD_D_INNER + 2 * SSD_GROUPS * SSD_STATE
NSA_HEAD_DIM = 64
NSA_HEADS = 16
NSA_KV_HEADS = 4
NSA_HPG = 4
CMP_BLOCK = 32
CMP_STRIDE = 16
CMP_HIDDEN = 128
SLC_BLOCK = 64
SLC_TOPK = 16
WINDOW = 512
Q_BLOCK = 128
N_EXPERTS = 64
TOP_K = 8
N_ROUTE_GROUPS = 8
TOPK_ROUTE_GROUPS = 4
EXPERT_FF = 256
ROUTE_SCALE = 2.5
ALPHA = 2.0 ** 0.25
NORM_EPS = 1e-5
IN_SPLITS = (SSD_D_INNER, SSD_CONV_CH, SSD_HEADS, NSA_HEADS * NSA_HEAD_DIM,
             6 * NSA_KV_HEADS * NSA_HEAD_DIM, 3 * NSA_HEADS, 2 * D_MODEL)
IN_OFFSETS = tuple(int(v) for v in np.cumsum(IN_SPLITS)[:-1])

LANES = 128
VMEM_LIMIT = 56 * 1024 * 1024

SEG_Z = (0, 2048)
SEG_XBC = (2048, 3072)
SEG_Q = (5120, 1024)
SEG_KV = (6144, 1536)
SEG_GM = (7680, 2048)
SEG_SMALL = (9728, 128)
N_PACKED = 9856


def _params(*sem):
    return pltpu.CompilerParams(dimension_semantics=sem, vmem_limit_bytes=VMEM_LIMIT)


def _silu(x):
    return x * (1.0 / (1.0 + jnp.exp(-x)))


def _sigmoid(x):
    return 1.0 / (1.0 + jnp.exp(-x))


def _softplus(x):
    return jnp.maximum(x, 0.0) + jnp.log(1.0 + jnp.exp(-jnp.abs(x)))


def _split2(x):
    hi = x.astype(BF16)
    lo = (x - hi.astype(F32)).astype(BF16)
    return hi, lo


def _split3(x):
    hi = x.astype(BF16)
    r = x - hi.astype(F32)
    mid = r.astype(BF16)
    lo = (r - mid.astype(F32)).astype(BF16)
    return hi, mid, lo


def _dot(a, b):
    return jnp.dot(a, b, preferred_element_type=F32)


def _dot_nt(a, b):
    return lax.dot_general(a, b, (((1,), (1,)), ((), ())), preferred_element_type=F32)


def _dot_tn(a, b):
    return lax.dot_general(a, b, (((0,), (0,)), ((), ())), preferred_element_type=F32)


def _dot_exact_rhs(x, sel):
    a, b, c = _split3(x)
    return _dot(a, sel) + _dot(b, sel) + _dot(c, sel)


def _pack_w_in(w_in, b_in):
    def pack(m):
        z, xbc, dt, q, kv, gn, gm = jnp.split(m, list(IN_OFFSETS), axis=-1)
        pad = jnp.zeros(m.shape[:-1] + (LANES - SSD_HEADS - 3 * NSA_HEADS,), m.dtype)
        return jnp.concatenate([z, xbc, q, kv, gm, dt, gn, pad], axis=-1)
    return pack(w_in).astype(BF16), pack(b_in[None, :])


def _in_proj_kernel(x_ref, w_ref, b_ref, z_ref, xbc_ref, q_ref, kv_ref, gm_ref, sm_ref):
    x = x_ref[...].astype(BF16)
    for ref, (off, width) in ((z_ref, SEG_Z), (xbc_ref, SEG_XBC), (q_ref, SEG_Q), (kv_ref, SEG_KV),
                              (gm_ref, SEG_GM), (sm_ref, SEG_SMALL)):
        ref[...] = _dot(x, w_ref[:, off:off + width]) + b_ref[:, off:off + width]


def _in_proj(x2d, w_packed, b_packed, tm):
    t = x2d.shape[0]
    segs = (SEG_Z, SEG_XBC, SEG_Q, SEG_KV, SEG_GM, SEG_SMALL)
    return pl.pallas_call(
        _in_proj_kernel,
        grid=(t // tm,),
        in_specs=[pl.BlockSpec((tm, D_MODEL), lambda i: (i, 0)),
                  pl.BlockSpec((D_MODEL, N_PACKED), lambda i: (0, 0), pipeline_mode=pl.Buffered(1)),
                  pl.BlockSpec((1, N_PACKED), lambda i: (0, 0))],
        out_specs=[pl.BlockSpec((tm, w), lambda i: (i, 0)) for _, w in segs],
        out_shape=[jax.ShapeDtypeStruct((t, w), F32) for _, w in segs],
        compiler_params=_params("arbitrary"),
        name="in_proj",
    )(x2d, w_packed, b_packed)


def _head_expand_matrix():
    h = np.arange(LANES)[:, None]
    c = np.arange(SSD_D_INNER)[None, :] // SSD_HEAD_DIM
    return jnp.asarray((h == c).astype(np.float32), BF16)


def _ssd_prompt_kernel(xbc_ref, z_ref, sm_ref, cw_ref, cb_ref, dtb_ref, alog_ref, dskip_ref, nw_ref, e_ref,
                       y_ref, st_ref, xh_ref, state_ref, ybuf_ref):
    c = pl.program_id(1)
    L = SSD_CHUNK
    G, N, P = SSD_GROUPS, SSD_STATE, SSD_HEAD_DIM
    GW = SSD_HPG * P

    @pl.when(c == 0)
    def _():
        xh_ref[0:8, :] = jnp.zeros((8, SSD_CONV_CH), F32)
        state_ref[...] = jnp.zeros_like(state_ref)

    xh_ref[8:8 + L, :] = xbc_ref[0]
    conv = cb_ref[...] + xh_ref[5:5 + L, :] * cw_ref[0:1, :]
    for k in range(1, SSD_CONV):
        conv = conv + xh_ref[5 + k:5 + k + L, :] * cw_ref[k:k + 1, :]
    xh_ref[0:8, :] = xh_ref[L:L + 8, :]
    act = _silu(conv)
    xs = act[:, :SSD_D_INNER]
    bm = act[:, SSD_D_INNER:SSD_D_INNER + G * N].astype(BF16)
    cm = act[:, SSD_D_INNER + G * N:]

    dt = _softplus(sm_ref[0] + dtb_ref[...])
    da = dt * (-jnp.exp(alog_ref[...]))
    row_i = lax.broadcasted_iota(jnp.int32, (L, L), 0)
    col_i = lax.broadcasted_iota(jnp.int32, (L, L), 1)
    tri = row_i >= col_i
    a_cs = _dot_exact_rhs_left(tri, da)
    a_cs_t = a_cs.T
    dt_t = dt.T
    a_last = a_cs[L - 1:L, :]
    w_end = dt * jnp.exp(a_last - a_cs)
    chunk_decay = jnp.broadcast_to(jnp.exp(a_last), (8, LANES))
    e_mat = e_ref[...]
    w_exp = _dot_exact_rhs(w_end, e_mat)
    dec_exp = _dot_exact_rhs(chunk_decay, e_mat)[0:1, :]
    xd = (xs * w_exp).astype(BF16)

    for g in range(G):
        cg = cm[:, g * N:(g + 1) * N]
        bg = bm[:, g * N:(g + 1) * N]
        cb = _dot_nt(cg.astype(BF16), bg)
        for j in range(SSD_HPG):
            h = g * SSD_HPG + j
            col = jnp.broadcast_to(a_cs[:, h:h + 1], (L, L))
            row = a_cs_t[h:h + 1, :]
            lm = jnp.where(tri, jnp.exp(col - row), 0.0)
            m = (cb * lm * dt_t[h:h + 1, :]).astype(BF16)
            eac = (jnp.exp(col) * cg).astype(BF16)
            lhs = jnp.concatenate([m, eac], axis=1)
            rhs = jnp.concatenate([xs[:, h * P:(h + 1) * P].astype(BF16),
                                   state_ref[:, h * P:(h + 1) * P].astype(BF16)], axis=0)
            ybuf_ref[:, h * P:(h + 1) * P] = _dot(lhs, rhs)
        sl = slice(g * GW, (g + 1) * GW)
        state_ref[:, sl] = state_ref[:, sl] * dec_exp[:, sl] + _dot_tn(bg, xd[:, sl])

    y = ybuf_ref[...] + dskip_ref[...] * xs
    y = y * _silu(z_ref[0])
    for g in range(G):
        sl = slice(g * GW, (g + 1) * GW)
        yg = y[:, sl]
        rs = lax.rsqrt(jnp.mean(yg * yg, axis=-1, keepdims=True) + NORM_EPS)
        y_ref[0, :, sl] = (yg * rs * nw_ref[:, sl]).astype(y_ref.dtype)

    @pl.when(c == pl.num_programs(1) - 1)
    def _():
        st_ref[0] = state_ref[...]


def _dot_exact_rhs_left(mask, x):
    sel = jnp.where(mask, 1.0, 0.0).astype(BF16)
    a, b, c = _split3(x)
    return _dot(sel, a) + _dot(sel, b) + _dot(sel, c)


def _ssd_prompt(xbc, z, small, conv_w, conv_b, dt_bias, a_log, d_skip, norm_w):
    b, L, _ = xbc.shape
    nc = L // SSD_CHUNK
    pad = LANES - SSD_HEADS
    dtb = jnp.pad(dt_bias.astype(F32), (0, pad))[None, :]
    alog = jnp.pad(a_log.astype(F32), (0, pad))[None, :]
    dskip = jnp.repeat(d_skip.astype(F32), SSD_HEAD_DIM)[None, :]
    full = lambda shape: pl.BlockSpec(shape, lambda i, j: (0,) * len(shape))
    y, st = pl.pallas_call(
        _ssd_prompt_kernel,
        grid=(b, nc),
        in_specs=[pl.BlockSpec((1, SSD_CHUNK, SSD_CONV_CH), lambda i, j: (i, j, 0)),
                  pl.BlockSpec((1, SSD_CHUNK, SSD_D_INNER), lambda i, j: (i, j, 0)),
                  pl.BlockSpec((1, SSD_CHUNK, LANES), lambda i, j: (i, j, 0)),
                  full((SSD_CONV, SSD_CONV_CH)), full((1, SSD_CONV_CH)), full((1, LANES)), full((1, LANES)),
                  full((1, SSD_D_INNER)), full((1, SSD_D_INNER)), full((LANES, SSD_D_INNER))],
        out_specs=[pl.BlockSpec((1, SSD_CHUNK, SSD_D_INNER), lambda i, j: (i, j, 0)),
                   pl.BlockSpec((1, SSD_STATE, SSD_D_INNER), lambda i, j: (i, 0, 0))],
        out_shape=[jax.ShapeDtypeStruct((b, L, SSD_D_INNER), BF16),
                   jax.ShapeDtypeStruct((b, SSD_STATE, SSD_D_INNER), F32)],
        scratch_shapes=[pltpu.VMEM((SSD_CHUNK + 8, SSD_CONV_CH), F32),
                        pltpu.VMEM((SSD_STATE, SSD_D_INNER), F32),
                        pltpu.VMEM((SSD_CHUNK, SSD_D_INNER), F32)],
        compiler_params=_params("arbitrary", "arbitrary"),
        name="ssd_prompt",
    )(xbc, z, small, conv_w.astype(F32), conv_b.astype(F32)[None, :], dtb, alog, dskip,
      norm_w.astype(F32)[None, :], _head_expand_matrix())
    return y, st


HALF = CMP_BLOCK // 2
KVW = NSA_KV_HEADS * NSA_HEAD_DIM


def _compress_weights(cmp_w1, cmp_b1, cmp_w2, cmp_b2, cmp_pe):
    w1 = cmp_w1.astype(BF16)
    w1ab = jnp.concatenate([w1[:, :HALF], w1[:, HALF:]], axis=-1)
    w1flat = w1.reshape(2, CMP_BLOCK * NSA_HEAD_DIM, CMP_HIDDEN)
    pe8 = jnp.broadcast_to(cmp_pe.reshape(2, 1, CMP_BLOCK * NSA_HEAD_DIM), (2, 8, CMP_BLOCK * NSA_HEAD_DIM))
    return (w1ab, w1flat, pe8.astype(F32), cmp_b1.astype(F32)[:, None, :], cmp_w2.astype(BF16),
            cmp_b2.astype(F32)[:, None, :])


GPT = LANES // NSA_HEAD_DIM


def _compress_body(load_strip, ns, w1ab_ref, w1flat_ref, pe_ref, b1_ref, w2_ref, b2_ref, bsh_ref, store):
    accs = [jnp.zeros((ns, 2 * CMP_HIDDEN), F32) for _ in range(GPT)]
    for l in range(HALF):
        xl = load_strip(l).astype(BF16)
        for g in range(GPT):
            accs[g] = accs[g] + _dot(xl[:, g * NSA_HEAD_DIM:(g + 1) * NSA_HEAD_DIM], w1ab_ref[0, l])
    cpe = _dot(pe_ref[0].astype(BF16), w1flat_ref[0])[0:1, :] + b1_ref[0]
    bsh_ref[ns:ns + 8, :] = jnp.zeros((8, CMP_HIDDEN), F32)
    for g in range(GPT):
        bsh_ref[0:ns, :] = accs[g][:, CMP_HIDDEN:]
        hid = _silu(accs[g][:, :CMP_HIDDEN] + bsh_ref[1:ns + 1, :] + cpe)
        store(g, _dot(hid.astype(BF16), w2_ref[0]) + b2_ref[0])


def _compress_prompt_kernel(kv_ref, w1ab_ref, w1flat_ref, pe_ref, b1_ref, w2_ref, b2_ref, out_ref, bsh_ref):
    ns = out_ref.shape[3]

    def load_strip(l):
        return kv_ref[0, pl.ds(l, ns, stride=CMP_STRIDE), :]

    def store(g, val):
        out_ref[0, 0, g] = val

    _compress_body(load_strip, ns, w1ab_ref, w1flat_ref, pe_ref, b1_ref, w2_ref, b2_ref, bsh_ref, store)


def _compress_prompt(kv, cw):
    b, L, _ = kv.shape
    ns = L // CMP_STRIDE
    w1ab, w1flat, pe8, b1, w2, b2 = cw
    tiles = KVW // LANES
    per_which = lambda shape: pl.BlockSpec((1,) + shape, lambda i, w, t: (w,) + (0,) * len(shape))
    return pl.pallas_call(
        _compress_prompt_kernel,
        grid=(b, 2, tiles),
        in_specs=[pl.BlockSpec((1, L, LANES), lambda i, w, t: (i, 0, w * tiles + t)),
                  per_which((HALF, NSA_HEAD_DIM, 2 * CMP_HIDDEN)),
                  per_which((CMP_BLOCK * NSA_HEAD_DIM, CMP_HIDDEN)),
                  per_which((8, CMP_BLOCK * NSA_HEAD_DIM)),
                  per_which((1, CMP_HIDDEN)),
                  per_which((CMP_HIDDEN, NSA_HEAD_DIM)),
                  per_which((1, NSA_HEAD_DIM))],
        out_specs=pl.BlockSpec((1, 1, GPT, ns, NSA_HEAD_DIM), lambda i, w, t: (i, w, t, 0, 0)),
        out_shape=jax.ShapeDtypeStruct((b, 2, NSA_KV_HEADS, ns, NSA_HEAD_DIM), F32),
        scratch_shapes=[pltpu.VMEM((ns + 8, CMP_HIDDEN), F32)],
        compiler_params=_params("arbitrary", "arbitrary", "arbitrary"),
        name="compress_prompt",
    )(kv, w1ab, w1flat, pe8, b1, w2, b2)


NEG = -1e30
SEL_TILE = 512
WIN_KEYS = WINDOW + Q_BLOCK


def _alibi_slopes():
    h = np.arange(1, NSA_HEADS + 1, dtype=np.float32)
    return (2.0 ** (-8.0 * h / NSA_HEADS)).astype(np.float32)


def _overlap_t(ncb, ns):
    i = np.arange(ncb)[None, :] * CMP_STRIDE
    m = np.arange(ns)[:, None] * SLC_BLOCK
    return ((i < m + SLC_BLOCK) & (i + CMP_BLOCK > m)).astype(np.float32)


def _masked_softmax_rows(s, mask):
    sm = jnp.where(mask, s, NEG)
    mx = jnp.max(sm, axis=-1, keepdims=True)
    p = jnp.where(mask, jnp.exp(sm - mx), 0.0)
    return p / jnp.maximum(jnp.sum(p, axis=-1, keepdims=True), 1e-30)


def _select_blocks_t(imp_t, qpos_row, v_ref, n_live):
    ns, nq = imp_t.shape
    m_i = lax.broadcasted_iota(jnp.int32, (ns, nq), 0)
    valid = m_i * SLC_BLOCK <= qpos_row
    cur = qpos_row // SLC_BLOCK
    forced = (m_i == 0) | (m_i == cur) | (m_i == cur - 1)
    v = jnp.where(forced, jnp.inf, imp_t)
    v = jnp.where(valid, v, -jnp.inf)
    v_ref[...] = v

    def body(mp, rank):
        r = v_ref[pl.ds(mp, 1), :]
        ahead = (r > v) | ((r == v) & (mp < m_i))
        return rank + jnp.where(ahead, 1, 0)

    rank = lax.fori_loop(0, n_live, body, jnp.zeros((ns, nq), jnp.int32))
    return jnp.where((rank < SLC_TOPK) & valid, 1.0, 0.0)


def _nsa_prompt_kernel(slopes_ref, q_ref, gate_ref, kct_ref, vc_ref, kst_ref, vs_ref, kwt_ref, vw_ref, ovl_ref,
                       o_ref, v_ref):
    g = pl.program_id(1)
    qi = pl.program_id(2)
    nq, dh, J = Q_BLOCK, NSA_HEAD_DIM, NSA_HPG
    ns, ncb = ovl_ref.shape
    start = qi * nq
    slopes = [slopes_ref[g * J + j] for j in range(J)]

    qb = q_ref[0] * (dh ** -0.5)
    qh = [qb[:, j * dh:(j + 1) * dh].astype(BF16) for j in range(J)]
    q_all = jnp.concatenate(qh, axis=0)
    gates = _sigmoid(gate_ref[0, 0])

    s_all = _dot(q_all, kct_ref[0, 0])
    qrow = lax.broadcasted_iota(jnp.int32, (nq, ncb), 0)
    ncol = lax.broadcasted_iota(jnp.int32, (nq, ncb), 1)
    d_c = (start + qrow) - (ncol * CMP_STRIDE + (CMP_BLOCK - 1))
    mask_c = d_c >= 0
    d_cf = d_c.astype(F32)
    vc = vc_ref[0, 0]
    o_c, p_sum = [], jnp.zeros((nq, ncb), F32)
    for j in range(J):
        p = _masked_softmax_rows(s_all[j * nq:(j + 1) * nq] - slopes[j] * d_cf, mask_c)
        p_sum = p_sum + p
        o_c.append(_dot(p.astype(BF16), vc))

    ovl = ovl_ref[...]
    imp_t = sum(_dot_nt(ovl, part) for part in _split3(p_sum))
    qpos_row = start + lax.broadcasted_iota(jnp.int32, (ns, nq), 1)
    sel_t = _select_blocks_t(imp_t, qpos_row, v_ref, start // SLC_BLOCK + nq // SLC_BLOCK)
    sel_q = sel_t.T.astype(BF16)

    tk = SEL_TILE
    bpt = tk // SLC_BLOCK
    dqt = (lax.broadcasted_iota(jnp.int32, (nq, tk), 0) - lax.broadcasted_iota(jnp.int32, (nq, tk), 1))
    blk_of_key = lax.broadcasted_iota(jnp.int32, (ns, tk), 1) // SLC_BLOCK
    blk_row = lax.broadcasted_iota(jnp.int32, (ns, tk), 0)

    def sel_step(kt, carry):
        ms, ls, accs = carry
        k0 = pl.multiple_of(kt * tk, tk)
        s = _dot(q_all, kst_ref[0, 0, :, pl.ds(k0, tk)])
        vt = vs_ref[0, 0, pl.ds(k0, tk), :]
        expand = jnp.where(blk_row == blk_of_key + kt * bpt, 1.0, 0.0).astype(BF16)
        chosen = _dot(sel_q, expand)
        d = dqt + (start - k0)
        ok = (chosen > 0.5) & (d >= 0)
        df = d.astype(F32)
        ms2, ls2, accs2 = [], [], []
        for j in range(J):
            sj = jnp.where(ok, s[j * nq:(j + 1) * nq] - slopes[j] * df, NEG)
            m_new = jnp.maximum(ms[j], jnp.max(sj, axis=-1, keepdims=True))
            p = jnp.where(ok, jnp.exp(sj - m_new), 0.0)
            corr = jnp.exp(ms[j] - m_new)
            ms2.append(m_new)
            ls2.append(ls[j] * corr + jnp.sum(p, axis=-1, keepdims=True))
            accs2.append(accs[j] * corr + _dot(p.astype(BF16), vt))
        return tuple(ms2), tuple(ls2), tuple(accs2)

    init = (tuple(jnp.full((nq, 1), NEG, F32) for _ in range(J)),
            tuple(jnp.zeros((nq, 1), F32) for _ in range(J)),
            tuple(jnp.zeros((nq, dh), F32) for _ in range(J)))
    n_tiles = (start + nq + tk - 1) // tk
    ms, ls, accs = lax.fori_loop(0, n_tiles, sel_step, init)
    o_s = [accs[j] / jnp.maximum(ls[j], 1e-30) for j in range(J)]

    w0 = pl.multiple_of(jnp.maximum(start - WINDOW, 0), nq)
    s_w = _dot(q_all, kwt_ref[0, 0, :, pl.ds(w0, WIN_KEYS)])
    vw = vw_ref[0, 0, pl.ds(w0, WIN_KEYS), :]
    d_w = (lax.broadcasted_iota(jnp.int32, (nq, WIN_KEYS), 0)
           - lax.broadcasted_iota(jnp.int32, (nq, WIN_KEYS), 1)) + (start - w0)
    mask_w = (d_w >= 0) & (d_w < WINDOW)
    d_wf = d_w.astype(F32)
    for j in range(J):
        p = _masked_softmax_rows(s_w[j * nq:(j + 1) * nq] - slopes[j] * d_wf, mask_w)
        o_w = _dot(p.astype(BF16), vw)
        out = (gates[:, 3 * j:3 * j + 1] * o_c[j] + gates[:, 3 * j + 1:3 * j + 2] * o_s[j]
               + gates[:, 3 * j + 2:3 * j + 3] * o_w)
        o_ref[0, :, j * dh:(j + 1) * dh] = out.astype(o_ref.dtype)


def _nsa_prompt(q, gates_raw, kct, vc, kst, vs, kwt, vw):
    b, L, _ = q.shape
    G, dh, J = NSA_KV_HEADS, NSA_HEAD_DIM, NSA_HPG
    ncb, ns = L // CMP_STRIDE, L // SLC_BLOCK
    nqb = L // Q_BLOCK
    per_bg = lambda shape: pl.BlockSpec((1, 1) + shape, lambda i, g, t, s: (i, g, 0, 0))
    grid_spec = pltpu.PrefetchScalarGridSpec(
        num_scalar_prefetch=1,
        grid=(b, G, nqb),
        in_specs=[pl.BlockSpec((1, Q_BLOCK, J * dh), lambda i, g, t, s: (i, t, g)),
                  pl.BlockSpec((1, 1, Q_BLOCK, 3 * J), lambda i, g, t, s: (i, g, t, 0)),
                  per_bg((dh, ncb)), per_bg((ncb, dh)), per_bg((dh, L)), per_bg((L, dh)),
                  per_bg((dh, L)), per_bg((L, dh)),
                  pl.BlockSpec((ns, ncb), lambda i, g, t, s: (0, 0))],
        out_specs=pl.BlockSpec((1, Q_BLOCK, J * dh), lambda i, g, t, s: (i, t, g)),
        scratch_shapes=[pltpu.VMEM((ns, Q_BLOCK), F32)],
    )
    return pl.pallas_call(
        _nsa_prompt_kernel,
        grid_spec=grid_spec,
        out_shape=jax.ShapeDtypeStruct((b, L, G * J * dh), BF16),
        compiler_params=_params("arbitrary", "arbitrary", "arbitrary"),
        name="nsa_prompt",
    )(jnp.asarray(_alibi_slopes()), q, gates_raw, kct, vc, kst, vs, kwt, vw,
      jnp.asarray(_overlap_t(ncb, ns), BF16))


def _nsa_prompt_branch(q, kv, small, cw):
    b, L, _ = q.shape
    G, dh = NSA_KV_HEADS, NSA_HEAD_DIM
    cmp = _compress_prompt(kv, cw).astype(BF16)
    kct = cmp[:, 0].transpose(0, 1, 3, 2)
    vc = cmp[:, 1]
    kv5 = kv.astype(BF16).reshape(b, L, 6, G, dh)
    rows = lambda w: kv5[:, :, w].transpose(0, 2, 1, 3)
    cols = lambda w: kv5[:, :, w].transpose(0, 2, 3, 1)
    gates_raw = small[:, :, SSD_HEADS:SSD_HEADS + 3 * NSA_HEADS].reshape(b, L, G, 3 * NSA_HPG).transpose(0, 2, 1, 3)
    return _nsa_prompt(q, gates_raw, kct, vc, cols(2), rows(3), cols(4), rows(5))


def _layer_norm(v, g, b):
    mu = jnp.mean(v, axis=-1, keepdims=True)
    c = v - mu
    var = jnp.mean(c * c, axis=-1, keepdims=True)
    return c * lax.rsqrt(var + NORM_EPS) * g + b


def _route_t(logits_t, bias_col, v_ref):
    e_n, n = logits_t.shape
    gsz = e_n // N_ROUTE_GROUPS
    scores = _sigmoid(logits_t)
    biased = scores + bias_col
    grp = []
    for a in range(N_ROUTE_GROUPS):
        blk = biased[a * gsz:(a + 1) * gsz]
        m1 = jnp.max(blk, axis=0, keepdims=True)
        cnt = jnp.sum(jnp.where(blk == m1, 1.0, 0.0), axis=0, keepdims=True)
        m2 = jnp.max(jnp.where(blk < m1, blk, -jnp.inf), axis=0, keepdims=True)
        grp.append(m1 + jnp.where(cnt >= 2.0, m1, m2))
    pieces = []
    for a in range(N_ROUTE_GROUPS):
        rank = jnp.zeros((1, n), jnp.int32)
        for b in range(N_ROUTE_GROUPS):
            if b != a:
                ahead = (grp[b] > grp[a]) | ((grp[b] == grp[a]) & (b < a))
                rank = rank + jnp.where(ahead, 1, 0)
        pieces.append(jnp.where(rank < TOPK_ROUTE_GROUPS, biased[a * gsz:(a + 1) * gsz], -jnp.inf))
    masked = jnp.concatenate(pieces, axis=0)
    v_ref[...] = masked
    e_i = lax.broadcasted_iota(jnp.int32, (e_n, n), 0)

    def body(ep, rank):
        r = v_ref[pl.ds(ep, 1), :]
        ahead = (r > masked) | ((r == masked) & (ep < e_i))
        return rank + jnp.where(ahead, 1, 0)

    rank = lax.fori_loop(0, e_n, body, jnp.zeros((e_n, n), jnp.int32))
    picked = jnp.where(rank < TOP_K, scores, 0.0)
    return picked / jnp.sum(picked, axis=0, keepdims=True) * ROUTE_SCALE


def _finish_kernel(x_ref, sy_ref, ny_ref, gm_ref, wsd_ref, wnd_ref, wo_ref, bo_ref, g1_ref, b1_ref,
                   wrh_ref, wrl_ref, br_ref, h_ref, hb_ref, gt_ref, v_ref):
    gate = _sigmoid(gm_ref[...])
    mixed = (gate[:, :D_MODEL] * _dot(sy_ref[...], wsd_ref[...])
             + gate[:, D_MODEL:] * _dot(ny_ref[...], wnd_ref[...]))
    o = _dot(mixed.astype(BF16), wo_ref[...]) + bo_ref[...]
    h = _layer_norm(ALPHA * x_ref[...] + o, g1_ref[...], b1_ref[...])
    h_ref[...] = h
    hb_ref[...] = h.astype(BF16)
    h_hi, h_lo = _split2(h)
    logits_t = _dot_nt(wrh_ref[...], h_hi) + _dot_nt(wrh_ref[...], h_lo) + _dot_nt(wrl_ref[...], h_hi)
    gt_ref[...] = _route_t(logits_t, br_ref[...], v_ref)


def _finish(x2d, ssd_y, nsa_y, gm, fw, tm):
    t = x2d.shape[0]
    wsd, wnd, wo, bo, g1, b1, wrh, wrl, br = fw
    row = lambda w: pl.BlockSpec((tm, w), lambda i: (i, 0))
    full = lambda a: pl.BlockSpec(a.shape, lambda i: (0,) * a.ndim)
    return pl.pallas_call(
        _finish_kernel,
        grid=(t // tm,),
        in_specs=[row(D_MODEL), row(SSD_D_INNER), row(D_MODEL), row(2 * D_MODEL)] + [full(a) for a in fw],
        out_specs=[row(D_MODEL), row(D_MODEL), pl.BlockSpec((N_EXPERTS, tm), lambda i: (0, i))],
        out_shape=[jax.ShapeDtypeStruct((t, D_MODEL), F32), jax.ShapeDtypeStruct((t, D_MODEL), BF16),
                   jax.ShapeDtypeStruct((N_EXPERTS, t), F32)],
        scratch_shapes=[pltpu.VMEM((N_EXPERTS, tm), F32)],
        compiler_params=_params("arbitrary"),
        name="finish",
    )(x2d, ssd_y, nsa_y, gm, *fw)


def _finish_weights(w_ssd_down, w_nsa_down, w_out, b_out, ln1_g, ln1_b, w_router, b_router):
    wr_t = w_router.astype(F32).T
    wrh = wr_t.astype(BF16)
    wrl = (wr_t - wrh.astype(F32)).astype(BF16)
    return (w_ssd_down.astype(BF16), w_nsa_down.astype(BF16), w_out.astype(BF16), b_out.astype(F32)[None, :],
            ln1_g.astype(F32)[None, :], ln1_b.astype(F32)[None, :], wrh, wrl, b_router.astype(F32)[:, None])


def _swiglu(x, w1, w3):
    return _silu(_dot(x, w1)) * _dot(x, w3)


def _moe_dense_kernel(hb_ref, h_ref, g_ref, w1_ref, w3_ref, w2_ref, ws1_ref, ws3_ref, ws2_ref, g2_ref, b2_ref,
                      y_ref, acc_ref):
    e = pl.program_id(1)
    x = hb_ref[...]

    @pl.when(e == 0)
    def _():
        acc_ref[...] = _dot(_swiglu(x, ws1_ref[...], ws3_ref[...]).astype(BF16), ws2_ref[...])

    pick = jnp.where(lax.broadcasted_iota(jnp.int32, (N_EXPERTS, EXPERT_FF), 0) == e, 1.0, 0.0).astype(BF16)
    g_hi, g_lo = _split2(g_ref[...])
    gate = _dot(g_hi, pick) + _dot(g_lo, pick)
    a = jnp.where(gate > 0.0, _swiglu(x, w1_ref[0], w3_ref[0]) * gate, 0.0)
    acc_ref[...] += _dot(a.astype(BF16), w2_ref[0])

    @pl.when(e == pl.num_programs(1) - 1)
    def _():
        y_ref[...] = _layer_norm(ALPHA * h_ref[...] + acc_ref[...], g2_ref[...], b2_ref[...])


def _moe_dense(hb, h, gates, mw, tm):
    t = h.shape[0]
    w1, w3, w2, ws1, ws3, ws2, g2, b2 = mw
    row = lambda w: pl.BlockSpec((tm, w), lambda i, e: (i, 0))
    full = lambda a: pl.BlockSpec(a.shape, lambda i, e: (0,) * a.ndim)
    return pl.pallas_call(
        _moe_dense_kernel,
        grid=(t // tm, N_EXPERTS),
        in_specs=[row(D_MODEL), row(D_MODEL), row(N_EXPERTS),
                  pl.BlockSpec((1, D_MODEL, EXPERT_FF), lambda i, e: (e, 0, 0)),
                  pl.BlockSpec((1, D_MODEL, EXPERT_FF), lambda i, e: (e, 0, 0)),
                  pl.BlockSpec((1, EXPERT_FF, D_MODEL), lambda i, e: (e, 0, 0)),
                  full(ws1), full(ws3), full(ws2), full(g2), full(b2)],
        out_specs=row(D_MODEL),
        out_shape=jax.ShapeDtypeStruct((t, D_MODEL), F32),
        scratch_shapes=[pltpu.VMEM((tm, D_MODEL), F32)],
        compiler_params=_params("arbitrary", "arbitrary"),
        name="moe_dense",
    )(hb, h, gates, *mw)


def _moe_weights(w_e1, w_e3, w_e2, w_s1, w_s3, w_s2, ln2_g, ln2_b):
    return (w_e1.astype(BF16), w_e3.astype(BF16), w_e2.astype(BF16), w_s1.astype(BF16), w_s3.astype(BF16),
            w_s2.astype(BF16), ln2_g.astype(F32)[None, :], ln2_b.astype(F32)[None, :])


def _finish_and_moe(x2d, ssd_y, nsa_y, gm, fw, mw, tm_finish, tm_moe):
    h, hb, gates_t = _finish(x2d, ssd_y, nsa_y, gm, fw, tm_finish)
    return _moe_dense(hb, h, gates_t.T, mw, tm_moe)


def _ssd_sample_kernel(xbc_ref, sc_ref, z_ref, sm_ref, st_ref, cw_ref, cb_ref, dtb_ref, alog_ref, dskip_ref,
                       nw_ref, e_ref, y_ref, sto_ref, xdt_t_ref, dec_t_ref, b_ref, c_ref, xs_ref, yt_ref):
    i = pl.program_id(0)
    db = xbc_ref.shape[0]
    G, N = SSD_GROUPS, SSD_STATE
    GW = SSD_HPG * SSD_HEAD_DIM

    @pl.when(i == 0)
    def _():
        conv = cb_ref[...] + xbc_ref[...] * cw_ref[SSD_CONV - 1:SSD_CONV, :]
        for k in range(SSD_CONV - 1):
            conv = conv + sc_ref[k] * cw_ref[k:k + 1, :]
        act = _silu(conv)
        xs = act[:, :SSD_D_INNER]
        xs_ref[...] = xs
        b_ref[...] = act[:, SSD_D_INNER:SSD_D_INNER + G * N].astype(BF16)
        c_ref[...] = act[:, SSD_D_INNER + G * N:].astype(BF16)
        dt = _softplus(sm_ref[...] + dtb_ref[...])
        dec = jnp.exp(dt * (-jnp.exp(alog_ref[...])))
        e_mat = e_ref[...]
        xdt_t_ref[...] = (xs * _dot_exact_rhs(dt, e_mat)).T.astype(BF16)
        dec_t_ref[...] = _dot_exact_rhs(dec, e_mat).T
        yt_ref[...] = jnp.zeros_like(yt_ref)

    is_row = lax.broadcasted_iota(jnp.int32, (db, N), 0) == i
    onehot = jnp.where(is_row, 1.0, 0.0).astype(BF16)
    is_lane = lax.broadcasted_iota(jnp.int32, (GW, db), 1) == i
    for g in range(G):
        rows = slice(g * GW, (g + 1) * GW)
        b_g = jnp.where(is_row, b_ref[:, g * N:(g + 1) * N], jnp.zeros((), BF16))
        contrib = _dot(xdt_t_ref[rows, :], b_g)
        decay = _dot_exact_rhs(dec_t_ref[rows, :], onehot)
        new = st_ref[0, rows, :] * decay + contrib
        sto_ref[0, rows, :] = new
        y_all = _dot_nt(new.astype(BF16), c_ref[:, g * N:(g + 1) * N])
        yt_ref[rows, :] += jnp.where(is_lane, y_all, 0.0)

    @pl.when(i == pl.num_programs(0) - 1)
    def _():
        y = yt_ref[...].T + dskip_ref[...] * xs_ref[...]
        y = y * _silu(z_ref[...])
        for g in range(G):
            sl = slice(g * GW, (g + 1) * GW)
            yg = y[:, sl]
            rs = lax.rsqrt(jnp.mean(yg * yg, axis=-1, keepdims=True) + NORM_EPS)
            y_ref[:, sl] = (yg * rs * nw_ref[:, sl]).astype(y_ref.dtype)


def _ssd_sample(xbc, state_conv, z, small, state_ssm, conv_w, conv_b, dt_bias, a_log, d_skip, norm_w):
    db = xbc.shape[0]
    pad = LANES - SSD_HEADS
    dtb = jnp.pad(dt_bias.astype(F32), (0, pad))[None, :]
    alog = jnp.pad(a_log.astype(F32), (0, pad))[None, :]
    dskip = jnp.repeat(d_skip.astype(F32), SSD_HEAD_DIM)[None, :]
    sc = state_conv.astype(F32).transpose(1, 0, 2)
    st = state_ssm.reshape(db, SSD_D_INNER, SSD_STATE)
    args = (xbc, sc, z, small, st, conv_w.astype(F32), conv_b.astype(F32)[None, :], dtb, alog, dskip,
            norm_w.astype(F32)[None, :], _head_expand_matrix())
    full = lambda a: pl.BlockSpec(a.shape, lambda i: (0,) * a.ndim)
    st_spec = pl.BlockSpec((1, SSD_D_INNER, SSD_STATE), lambda i: (i, 0, 0))
    y, st_new = pl.pallas_call(
        _ssd_sample_kernel,
        grid=(db,),
        in_specs=[full(a) for a in args[:4]] + [st_spec] + [full(a) for a in args[5:]],
        out_specs=[pl.BlockSpec((db, SSD_D_INNER), lambda i: (0, 0)), st_spec],
        out_shape=[jax.ShapeDtypeStruct((db, SSD_D_INNER), BF16),
                   jax.ShapeDtypeStruct((db, SSD_D_INNER, SSD_STATE), F32)],
        scratch_shapes=[pltpu.VMEM((SSD_D_INNER, db), BF16), pltpu.VMEM((SSD_D_INNER, db), F32),
                        pltpu.VMEM((db, SSD_GROUPS * SSD_STATE), BF16),
                        pltpu.VMEM((db, SSD_GROUPS * SSD_STATE), BF16),
                        pltpu.VMEM((db, SSD_D_INNER), F32), pltpu.VMEM((SSD_D_INNER, db), F32)],
        compiler_params=_params("arbitrary"),
        name="ssd_sample",
    )(*args)
    return y, st_new.reshape(state_ssm.shape)


def _nsa_sample_kernel(n_pages, *refs):
    pt_ref = refs[0]
    page_refs = refs[1:1 + n_pages]
    (win_ref, q_ref, kvn_ref, gate_ref, slope_ref, ovl_ref, w1ab_ref, w1flat_ref, pe_ref, b1_ref, w2_ref,
     b2_ref, o_ref, cmp_rows_ref, ks_ref, vs_ref, kw_ref, vw_ref, bsh_ref, kc_ref, vc_ref) = refs[1 + n_pages:]
    del pt_ref
    H, dh, G = NSA_HEADS, NSA_HEAD_DIM, NSA_KV_HEADS
    past = n_pages * PAGE_SIZE
    ncb = past // CMP_STRIDE
    nsb = ovl_ref.shape[1]
    wbuf = win_ref.shape[1]
    tiles = KVW // LANES

    for p in range(n_pages):
        r = slice(p * PAGE_SIZE, (p + 1) * PAGE_SIZE)
        for t in range(2 * tiles):
            cmp_rows_ref[t, r, :] = page_refs[p][0, :, t * LANES:(t + 1) * LANES]
        ks_ref[r, :] = page_refs[p][0, :, 2 * KVW:3 * KVW].astype(BF16)
        vs_ref[r, :] = page_refs[p][0, :, 3 * KVW:4 * KVW].astype(BF16)
    new = kvn_ref[0]
    tail = lax.broadcasted_iota(jnp.int32, (LANES, KVW), 0) == 0
    put = lambda w: jnp.where(tail, jnp.broadcast_to(new[:, w * KVW:(w + 1) * KVW], (LANES, KVW)), 0.0).astype(BF16)
    ks_ref[past:past + LANES, :] = put(2)
    vs_ref[past:past + LANES, :] = put(3)
    kw_ref[0:wbuf, :] = win_ref[0, :, 0:KVW].astype(BF16)
    vw_ref[0:wbuf, :] = win_ref[0, :, KVW:2 * KVW].astype(BF16)
    kw_ref[wbuf:wbuf + LANES, :] = put(4)
    vw_ref[wbuf:wbuf + LANES, :] = put(5)

    for w, dst in ((0, kc_ref), (1, vc_ref)):
        for t in range(tiles):
            def load_strip(l, w=w, t=t):
                return cmp_rows_ref[w * tiles + t, pl.ds(l, ncb, stride=CMP_STRIDE), :]

            def store(g, val, t=t, dst=dst):
                c0 = (t * GPT + g) * dh
                dst[:, c0:c0 + dh] = val.astype(BF16)

            sub = lambda ref: ref.at[pl.ds(w, 1)]
            _compress_body(load_strip, ncb, sub(w1ab_ref), sub(w1flat_ref), sub(pe_ref), sub(b1_ref), sub(w2_ref),
                           sub(b2_ref), bsh_ref, store)

    head_r = lax.broadcasted_iota(jnp.int32, (H, KVW), 0) // NSA_HPG
    lane_g = lax.broadcasted_iota(jnp.int32, (H, KVW), 1) // dh
    diag = head_r == lane_g
    q16 = q_ref[0] * (dh ** -0.5)
    q_bd = jnp.where(diag, jnp.concatenate([q16] * G, axis=1), 0.0).astype(BF16)
    slopes = slope_ref[...]
    gates = _sigmoid(gate_ref[0])

    def fold(o):
        o = jnp.where(diag, o, 0.0)
        return sum(o[:, g * dh:(g + 1) * dh] for g in range(G))

    d_c = past - (lax.broadcasted_iota(jnp.int32, (H, ncb), 1) * CMP_STRIDE + (CMP_BLOCK - 1))
    p_c = _masked_softmax_rows(_dot_nt(q_bd, kc_ref[...]) - slopes * d_c.astype(F32), d_c >= 0)
    o_c = fold(_dot(p_c.astype(BF16), vc_ref[...]))

    p_grp = jnp.concatenate(
        [jnp.sum(p_c[g * NSA_HPG:(g + 1) * NSA_HPG], axis=0, keepdims=True) for g in range(G)]
        + [jnp.zeros((8 - G, ncb), F32)], axis=0)
    imp = sum(_dot(part, ovl_ref[...]) for part in _split3(p_grp))
    m_l = lax.broadcasted_iota(jnp.int32, (8, nsb), 1)
    valid = m_l * SLC_BLOCK <= past
    cur = past // SLC_BLOCK
    forced = (m_l == 0) | (m_l == cur) | (m_l == cur - 1)
    v = jnp.where(valid, jnp.where(forced, jnp.inf, imp), -jnp.inf)
    v_t = v.T
    mp_i = lax.broadcasted_iota(jnp.int32, (nsb, nsb), 0)
    m_i = lax.broadcasted_iota(jnp.int32, (nsb, nsb), 1)
    sel_rows = []
    for g in range(G):
        v_col = jnp.broadcast_to(v_t[:, g:g + 1], (nsb, nsb))
        v_row = jnp.broadcast_to(v[g:g + 1, :], (nsb, nsb))
        ahead = (v_col > v_row) | ((v_col == v_row) & (mp_i < m_i))
        rank = jnp.sum(jnp.where(ahead, 1.0, 0.0), axis=0, keepdims=True)
        sel = jnp.where((rank < SLC_TOPK) & valid[0:1, :], 1.0, 0.0)
        sel_rows.append(jnp.broadcast_to(sel, (NSA_HPG, nsb)))
    sel_h = jnp.concatenate(sel_rows, axis=0).astype(BF16)

    nk = past + LANES
    expand = jnp.where(lax.broadcasted_iota(jnp.int32, (nsb, nk), 0)
                       == lax.broadcasted_iota(jnp.int32, (nsb, nk), 1) // SLC_BLOCK, 1.0, 0.0).astype(BF16)
    d_s = past - lax.broadcasted_iota(jnp.int32, (H, nk), 1)
    ok = (_dot(sel_h, expand) > 0.5) & (d_s >= 0)
    p_s = _masked_softmax_rows(_dot_nt(q_bd, ks_ref[...]) - slopes * d_s.astype(F32), ok)
    o_s = fold(_dot(p_s.astype(BF16), vs_ref[...]))

    nw = wbuf + LANES
    d_w = wbuf - lax.broadcasted_iota(jnp.int32, (H, nw), 1)
    p_w = _masked_softmax_rows(_dot_nt(q_bd, kw_ref[...]) - slopes * d_w.astype(F32), (d_w >= 0) & (d_w < WINDOW))
    o_w = fold(_dot(p_w.astype(BF16), vw_ref[...]))

    o_ref[0] = (gates[:, 0:1] * o_c + gates[:, 1:2] * o_s + gates[:, 2:3] * o_w).astype(o_ref.dtype)


def _overlap_sample(ncb, nsb):
    i = np.arange(ncb)[:, None] * CMP_STRIDE
    m = np.arange(nsb)[None, :] * SLC_BLOCK
    ok = (i < m + SLC_BLOCK) & (i + CMP_BLOCK > m) & (np.arange(ncb)[:, None] < ncb - 1)
    return ok.astype(np.float32)


def _nsa_sample(q, kv_new, small, cache_kv_paged, page_table, cache_kv_win, cw):
    db = q.shape[0]
    H, dh = NSA_HEADS, NSA_HEAD_DIM
    n_pages = page_table.shape[1]
    past = n_pages * PAGE_SIZE
    wbuf = cache_kv_win.shape[1]
    ncb = past // CMP_STRIDE
    nsb = LANES * pl.cdiv(pl.cdiv(past + 1, SLC_BLOCK), LANES)
    pages = cache_kv_paged.reshape(cache_kv_paged.shape[0], PAGE_SIZE, 4 * KVW)
    win = cache_kv_win.reshape(db, wbuf, 2 * KVW)
    gates_raw = small[:, SSD_HEADS:SSD_HEADS + 3 * H].reshape(db, H, 3)
    slopes = jnp.asarray(_alibi_slopes())[:, None]
    ovl = jnp.asarray(_overlap_sample(ncb, nsb), BF16)
    consts = (slopes, ovl) + tuple(cw)
    per_seq = lambda shape: pl.BlockSpec((1,) + shape, lambda i, pt: (i,) + (0,) * len(shape))
    full = lambda a: pl.BlockSpec(a.shape, lambda i, pt: (0,) * a.ndim)
    page_spec = lambda p: pl.BlockSpec((1, PAGE_SIZE, 4 * KVW), lambda i, pt: (pt[i, p], 0, 0))
    grid_spec = pltpu.PrefetchScalarGridSpec(
        num_scalar_prefetch=1,
        grid=(db,),
        in_specs=[page_spec(p) for p in range(n_pages)]
        + [per_seq((wbuf, 2 * KVW)), per_seq((H, dh)), per_seq((1, 6 * KVW)), per_seq((H, 3))]
        + [full(a) for a in consts],
        out_specs=per_seq((H, dh)),
        scratch_shapes=[pltpu.VMEM((2 * KVW // LANES, past, LANES), F32),
                        pltpu.VMEM((past + LANES, KVW), BF16), pltpu.VMEM((past + LANES, KVW), BF16),
                        pltpu.VMEM((wbuf + LANES, KVW), BF16), pltpu.VMEM((wbuf + LANES, KVW), BF16),
                        pltpu.VMEM((ncb + 8, CMP_HIDDEN), F32),
                        pltpu.VMEM((ncb, KVW), BF16), pltpu.VMEM((ncb, KVW), BF16)],
    )
    out = pl.pallas_call(
        functools.partial(_nsa_sample_kernel, n_pages),
        grid_spec=grid_spec,
        out_shape=jax.ShapeDtypeStruct((db, H, dh), BF16),
        compiler_params=_params("arbitrary"),
        name="nsa_sample",
    )(page_table.astype(jnp.int32), *([pages] * n_pages), win, q.reshape(db, H, dh), kv_new.reshape(db, 1, 6 * KVW),
      gates_raw, *consts)
    return out.reshape(db, H * dh)


def kernel(x_prompt, x_sample, cache_kv_paged, cache_kv_win, state_ssm, state_conv, page_table, w_in, b_in, conv_w, conv_b, dt_bias, a_log, d_skip, ssd_norm_w, cmp_w1, cmp_b1, cmp_w2, cmp_b2, cmp_pe, w_ssd_down, w_nsa_down, w_out, b_out, ln1_g, ln1_b, w_router, b_router, w_e1, w_e3, w_e2, w_s1, w_s3, w_s2, ln2_g, ln2_b):
    b, L, _ = x_prompt.shape
    db = x_sample.shape[0]
    G, dh = NSA_KV_HEADS, NSA_HEAD_DIM
    wbuf = cache_kv_win.shape[1]
    wp, bp = _pack_w_in(w_in, b_in)
    cw = _compress_weights(cmp_w1, cmp_b1, cmp_w2, cmp_b2, cmp_pe)
    fw = _finish_weights(w_ssd_down, w_nsa_down, w_out, b_out, ln1_g, ln1_b, w_router, b_router)
    mw = _moe_weights(w_e1, w_e3, w_e2, w_s1, w_s3, w_s2, ln2_g, ln2_b)
    ssd_w = (conv_w, conv_b, dt_bias, a_log, d_skip, ssd_norm_w)
    tm = min(256, b * L)
    per_seq = lambda a: a.reshape(b, L, a.shape[-1])

    xp = x_prompt.reshape(b * L, D_MODEL)
    z, xbc, q, kv, gm, small = _in_proj(xp, wp, bp, tm)
    ssd_y, st_t = _ssd_prompt(per_seq(xbc), per_seq(z), per_seq(small), *ssd_w)
    nsa_y = _nsa_prompt_branch(per_seq(q), per_seq(kv), per_seq(small), cw)
    yp = _finish_and_moe(xp, ssd_y.reshape(b * L, -1), nsa_y.reshape(b * L, -1), gm, fw, mw, tm,
                         min(1024, b * L)).reshape(b, L, D_MODEL)
    kv6 = kv.reshape(b, L, 6, G, dh)
    kv_rows_p = kv6[:, :, :4]
    win_p = kv6[:, L - wbuf:, 4:]
    ssm_p = st_t.reshape(b, SSD_STATE, SSD_HEADS, SSD_HEAD_DIM).transpose(0, 2, 3, 1).astype(state_ssm.dtype)
    conv_p = per_seq(xbc)[:, L - (SSD_CONV - 1):]

    xs = x_sample.reshape(db, D_MODEL)
    z, xbc, q, kv, gm, small = _in_proj(xs, wp, bp, db)
    ssd_y, ssm_s = _ssd_sample(xbc, state_conv, z, small, state_ssm, *ssd_w)
    nsa_y = _nsa_sample(q, kv, small, cache_kv_paged, page_table, cache_kv_win, cw)
    ys = _finish_and_moe(xs, ssd_y, nsa_y, gm, fw, mw, db, db).reshape(db, 1, D_MODEL)
    kv6 = kv.reshape(db, 1, 6, G, dh)
    kv_rows_s = kv6[:, :, :4]
    win_s = jnp.concatenate([cache_kv_win[:, 1:], kv6[:, :, 4:].astype(cache_kv_win.dtype)], axis=1)
    conv_s = jnp.concatenate([state_conv[:, 1:].astype(xbc.dtype), xbc[:, None, :]], axis=1)
    return (yp, ys, kv_rows_p, kv_rows_s, win_p, win_s, ssm_p, ssm_s, conv_p, conv_s)
```

```python
import functools

import numpy as np
import jax
import jax.numpy as jnp
from jax import lax
from jax.experimental import pallas as pl
from jax.experimental.pallas import tpu as pltpu

F32 = jnp.float32
BF16 = jnp.bfloat16

D_MODEL = 1024
PAGE_SIZE = 128
SSD_D_INNER = 2048
SSD_HEAD_DIM = 64
SSD_HEADS = 32
SSD_GROUPS = 4
SSD_HPG = 8
SSD_STATE = 128
SSD_CONV = 4
SSD_CHUNK = 128
SSD_CONV_CH = SSD_D_INNER + 2 * SSD_GROUPS * SSD_STATE
NSA_HEAD_DIM = 64
NSA_HEADS = 16
NSA_KV_HEADS = 4
NSA_HPG = 4
CMP_BLOCK = 32
CMP_STRIDE = 16
CMP_HIDDEN = 128
SLC_BLOCK = 64
SLC_TOPK = 16
WINDOW = 512
Q_BLOCK = 128
N_EXPERTS = 64
TOP_K = 8
N_ROUTE_GROUPS = 8
TOPK_ROUTE_GROUPS = 4
EXPERT_FF = 256
ROUTE_SCALE = 2.5
ALPHA = 2.0 ** 0.25
NORM_EPS = 1e-5
IN_SPLITS = (SSD_D_INNER, SSD_CONV_CH, SSD_HEADS, NSA_HEADS * NSA_HEAD_DIM,
             6 * NSA_KV_HEADS * NSA_HEAD_DIM, 3 * NSA_HEADS, 2 * D_MODEL)
IN_OFFSETS = tuple(int(v) for v in np.cumsum(IN_SPLITS)[:-1])

LANES = 128
VMEM_LIMIT = 56 * 1024 * 1024

SEG_Z = (0, 2048)
SEG_XBC = (2048, 3072)
SEG_Q = (5120, 1024)
SEG_KV = (6144, 1536)
SEG_GM = (7680, 2048)
SEG_SMALL = (9728, 128)
N_PACKED = 9856


def _params(*sem):
    return pltpu.CompilerParams(dimension_semantics=sem, vmem_limit_bytes=VMEM_LIMIT)


def _silu(x):
    return x * (1.0 / (1.0 + jnp.exp(-x)))


def _sigmoid(x):
    return 1.0 / (1.0 + jnp.exp(-x))


def _softplus(x):
    return jnp.maximum(x, 0.0) + jnp.log(1.0 + jnp.exp(-jnp.abs(x)))


def _split2(x):
    hi = x.astype(BF16)
    lo = (x - hi.astype(F32)).astype(BF16)
    return hi, lo


def _split3(x):
    hi = x.astype(BF16)
    r = x - hi.astype(F32)
    mid = r.astype(BF16)
    lo = (r - mid.astype(F32)).astype(BF16)
    return hi, mid, lo


def _dot(a, b):
    return jnp.dot(a, b, preferred_element_type=F32)


def _dot_nt(a, b):
    return lax.dot_general(a, b, (((1,), (1,)), ((), ())), preferred_element_type=F32)


def _dot_tn(a, b):
    return lax.dot_general(a, b, (((0,), (0,)), ((), ())), preferred_element_type=F32)


def _dot_exact_rhs(x, sel):
    a, b, c = _split3(x)
    return _dot(a, sel) + _dot(b, sel) + _dot(c, sel)


def _pack_w_in(w_in, b_in):
    def pack(m):
        z, xbc, dt, q, kv, gn, gm = jnp.split(m, list(IN_OFFSETS), axis=-1)
        pad = jnp.zeros(m.shape[:-1] + (LANES - SSD_HEADS - 3 * NSA_HEADS,), m.dtype)
        return jnp.concatenate([z, xbc, q, kv, gm, dt, gn, pad], axis=-1)
    return pack(w_in).astype(BF16), pack(b_in[None, :])


def _in_proj_kernel(x_ref, w_ref, b_ref, z_ref, xbc_ref, q_ref, kv_ref, gm_ref, sm_ref):
    x = x_ref[...].astype(BF16)
    for ref, (off, width) in ((z_ref, SEG_Z), (xbc_ref, SEG_XBC), (q_ref, SEG_Q), (kv_ref, SEG_KV),
                              (gm_ref, SEG_GM), (sm_ref, SEG_SMALL)):
        ref[...] = _dot(x, w_ref[:, off:off + width]) + b_ref[:, off:off + width]


def _in_proj(x2d, w_packed, b_packed, tm):
    t = x2d.shape[0]
    segs = (SEG_Z, SEG_XBC, SEG_Q, SEG_KV, SEG_GM, SEG_SMALL)
    return pl.pallas_call(
        _in_proj_kernel,
        grid=(t // tm,),
        in_specs=[pl.BlockSpec((tm, D_MODEL), lambda i: (i, 0)),
                  pl.BlockSpec((D_MODEL, N_PACKED), lambda i: (0, 0), pipeline_mode=pl.Buffered(1)),
                  pl.BlockSpec((1, N_PACKED), lambda i: (0, 0))],
        out_specs=[pl.BlockSpec((tm, w), lambda i: (i, 0)) for _, w in segs],
        out_shape=[jax.ShapeDtypeStruct((t, w), F32) for _, w in segs],
        compiler_params=_params("arbitrary"),
        name="in_proj",
    )(x2d, w_packed, b_packed)


def _head_expand_matrix():
    h = np.arange(LANES)[:, None]
    c = np.arange(SSD_D_INNER)[None, :] // SSD_HEAD_DIM
    return jnp.asarray((h == c).astype(np.float32), BF16)


def _ssd_prompt_kernel(xbc_ref, z_ref, sm_ref, cw_ref, cb_ref, dtb_ref, alog_ref, dskip_ref, nw_ref, e_ref,
                       y_ref, st_ref, xh_ref, state_ref, ybuf_ref):
    c = pl.program_id(1)
    L = SSD_CHUNK
    G, N, P = SSD_GROUPS, SSD_STATE, SSD_HEAD_DIM
    GW = SSD_HPG * P

    @pl.when(c == 0)
    def _():
        xh_ref[0:8, :] = jnp.zeros((8, SSD_CONV_CH), F32)
        state_ref[...] = jnp.zeros_like(state_ref)

    xh_ref[8:8 + L, :] = xbc_ref[0]
    conv = cb_ref[...] + xh_ref[5:5 + L, :] * cw_ref[0:1, :]
    for k in range(1, SSD_CONV):
        conv = conv + xh_ref[5 + k:5 + k + L, :] * cw_ref[k:k + 1, :]
    xh_ref[0:8, :] = xh_ref[L:L + 8, :]
    act = _silu(conv)
    xs = act[:, :SSD_D_INNER]
    bm = act[:, SSD_D_INNER:SSD_D_INNER + G * N].astype(BF16)
    cm = act[:, SSD_D_INNER + G * N:]

    dt = _softplus(sm_ref[0] + dtb_ref[...])
    da = dt * (-jnp.exp(alog_ref[...]))
    row_i = lax.broadcasted_iota(jnp.int32, (L, L), 0)
    col_i = lax.broadcasted_iota(jnp.int32, (L, L), 1)
    tri = row_i >= col_i
    a_cs = _dot_exact_rhs_left(tri, da)
    a_cs_t = a_cs.T
    dt_t = dt.T
    a_last = a_cs[L - 1:L, :]
    w_end = dt * jnp.exp(a_last - a_cs)
    chunk_decay = jnp.broadcast_to(jnp.exp(a_last), (8, LANES))
    e_mat = e_ref[...]
    w_exp = _dot_exact_rhs(w_end, e_mat)
    dec_exp = _dot_exact_rhs(chunk_decay, e_mat)[0:1, :]
    xd = (xs * w_exp).astype(BF16)

    for g in range(G):
        cg = cm[:, g * N:(g + 1) * N]
        bg = bm[:, g * N:(g + 1) * N]
        cb = _dot_nt(cg.astype(BF16), bg)
        for j in range(SSD_HPG):
            h = g * SSD_HPG + j
            col = jnp.broadcast_to(a_cs[:, h:h + 1], (L, L))
            row = a_cs_t[h:h + 1, :]
            lm = jnp.where(tri, jnp.exp(col - row), 0.0)
            m = (cb * lm * dt_t[h:h + 1, :]).astype(BF16)
            eac = (jnp.exp(col) * cg).astype(BF16)
            lhs = jnp.concatenate([m, eac], axis=1)
            rhs = jnp.concatenate([xs[:, h * P:(h + 1) * P].astype(BF16),
                                   state_ref[:, h * P:(h + 1) * P].astype(BF16)], axis=0)
            ybuf_ref[:, h * P:(h + 1) * P] = _dot(lhs, rhs)
        sl = slice(g * GW, (g + 1) * GW)
        state_ref[:, sl] = state_ref[:, sl] * dec_exp[:, sl] + _dot_tn(bg, xd[:, sl])

    y = ybuf_ref[...] + dskip_ref[...] * xs
    y = y * _silu(z_ref[0])
    for g in range(G):
        sl = slice(g * GW, (g + 1) * GW)
        yg = y[:, sl]
        rs = lax.rsqrt(jnp.mean(yg * yg, axis=-1, keepdims=True) + NORM_EPS)
        y_ref[0, :, sl] = (yg * rs * nw_ref[:, sl]).astype(y_ref.dtype)

    @pl.when(c == pl.num_programs(1) - 1)
    def _():
        st_ref[0] = state_ref[...]


def _dot_exact_rhs_left(mask, x):
    sel = jnp.where(mask, 1.0, 0.0).astype(BF16)
    a, b, c = _split3(x)
    return _dot(sel, a) + _dot(sel, b) + _dot(sel, c)


def _ssd_prompt(xbc, z, small, conv_w, conv_b, dt_bias, a_log, d_skip, norm_w):
    b, L, _ = xbc.shape
    nc = L // SSD_CHUNK
    pad = LANES - SSD_HEADS
    dtb = jnp.pad(dt_bias.astype(F32), (0, pad))[None, :]
    alog = jnp.pad(a_log.astype(F32), (0, pad))[None, :]
    dskip = jnp.repeat(d_skip.astype(F32), SSD_HEAD_DIM)[None, :]
    full = lambda shape: pl.BlockSpec(shape, lambda i, j: (0,) * len(shape))
    y, st = pl.pallas_call(
        _ssd_prompt_kernel,
        grid=(b, nc),
        in_specs=[pl.BlockSpec((1, SSD_CHUNK, SSD_CONV_CH), lambda i, j: (i, j, 0)),
                  pl.BlockSpec((1, SSD_CHUNK, SSD_D_INNER), lambda i, j: (i, j, 0)),
                  pl.BlockSpec((1, SSD_CHUNK, LANES), lambda i, j: (i, j, 0)),
                  full((SSD_CONV, SSD_CONV_CH)), full((1, SSD_CONV_CH)), full((1, LANES)), full((1, LANES)),
                  full((1, SSD_D_INNER)), full((1, SSD_D_INNER)), full((LANES, SSD_D_INNER))],
        out_specs=[pl.BlockSpec((1, SSD_CHUNK, SSD_D_INNER), lambda i, j: (i, j, 0)),
                   pl.BlockSpec((1, SSD_STATE, SSD_D_INNER), lambda i, j: (i, 0, 0))],
        out_shape=[jax.ShapeDtypeStruct((b, L, SSD_D_INNER), BF16),
                   jax.ShapeDtypeStruct((b, SSD_STATE, SSD_D_INNER), F32)],
        scratch_shapes=[pltpu.VMEM((SSD_CHUNK + 8, SSD_CONV_CH), F32),
                        pltpu.VMEM((SSD_STATE, SSD_D_INNER), F32),
                        pltpu.VMEM((SSD_CHUNK, SSD_D_INNER), F32)],
        compiler_params=_params("arbitrary", "arbitrary"),
        name="ssd_prompt",
    )(xbc, z, small, conv_w.astype(F32), conv_b.astype(F32)[None, :], dtb, alog, dskip,
      norm_w.astype(F32)[None, :], _head_expand_matrix())
    return y, st


HALF = CMP_BLOCK // 2
KVW = NSA_KV_HEADS * NSA_HEAD_DIM


def _compress_weights(cmp_w1, cmp_b1, cmp_w2, cmp_b2, cmp_pe):
    w1 = cmp_w1.astype(BF16)
    w1ab = jnp.concatenate([w1[:, :HALF], w1[:, HALF:]], axis=-1)
    w1flat = w1.reshape(2, CMP_BLOCK * NSA_HEAD_DIM, CMP_HIDDEN)
    pe8 = jnp.broadcast_to(cmp_pe.reshape(2, 1, CMP_BLOCK * NSA_HEAD_DIM), (2, 8, CMP_BLOCK * NSA_HEAD_DIM))
    return (w1ab, w1flat, pe8.astype(F32), cmp_b1.astype(F32)[:, None, :], cmp_w2.astype(BF16),
            cmp_b2.astype(F32)[:, None, :])


GPT = LANES // NSA_HEAD_DIM


def _compress_body(load_strip, ns, w1ab_ref, w1flat_ref, pe_ref, b1_ref, w2_ref, b2_ref, bsh_ref, store):
    accs = [jnp.zeros((ns, 2 * CMP_HIDDEN), F32) for _ in range(GPT)]
    for l in range(HALF):
        xl = load_strip(l).astype(BF16)
        for g in range(GPT):
            accs[g] = accs[g] + _dot(xl[:, g * NSA_HEAD_DIM:(g + 1) * NSA_HEAD_DIM], w1ab_ref[0, l])
    cpe = _dot(pe_ref[0].astype(BF16), w1flat_ref[0])[0:1, :] + b1_ref[0]
    bsh_ref[ns:ns + 8, :] = jnp.zeros((8, CMP_HIDDEN), F32)
    for g in range(GPT):
        bsh_ref[0:ns, :] = accs[g][:, CMP_HIDDEN:]
        hid = _silu(accs[g][:, :CMP_HIDDEN] + bsh_ref[1:ns + 1, :] + cpe)
        store(g, _dot(hid.astype(BF16), w2_ref[0]) + b2_ref[0])


def _compress_prompt_kernel(kv_ref, w1ab_ref, w1flat_ref, pe_ref, b1_ref, w2_ref, b2_ref, out_ref, bsh_ref):
    ns = out_ref.shape[3]

    def load_strip(l):
        return kv_ref[0, pl.ds(l, ns, stride=CMP_STRIDE), :]

    def store(g, val):
        out_ref[0, 0, g] = val

    _compress_body(load_strip, ns, w1ab_ref, w1flat_ref, pe_ref, b1_ref, w2_ref, b2_ref, bsh_ref, store)


def _compress_prompt(kv, cw):
    b, L, _ = kv.shape
    ns = L // CMP_STRIDE
    w1ab, w1flat, pe8, b1, w2, b2 = cw
    tiles = KVW // LANES
    per_which = lambda shape: pl.BlockSpec((1,) + shape, lambda i, w, t: (w,) + (0,) * len(shape))
    return pl.pallas_call(
        _compress_prompt_kernel,
        grid=(b, 2, tiles),
        in_specs=[pl.BlockSpec((1, L, LANES), lambda i, w, t: (i, 0, w * tiles + t)),
                  per_which((HALF, NSA_HEAD_DIM, 2 * CMP_HIDDEN)),
                  per_which((CMP_BLOCK * NSA_HEAD_DIM, CMP_HIDDEN)),
                  per_which((8, CMP_BLOCK * NSA_HEAD_DIM)),
                  per_which((1, CMP_HIDDEN)),
                  per_which((CMP_HIDDEN, NSA_HEAD_DIM)),
                  per_which((1, NSA_HEAD_DIM))],
        out_specs=pl.BlockSpec((1, 1, GPT, ns, NSA_HEAD_DIM), lambda i, w, t: (i, w, t, 0, 0)),
        out_shape=jax.ShapeDtypeStruct((b, 2, NSA_KV_HEADS, ns, NSA_HEAD_DIM), F32),
        scratch_shapes=[pltpu.VMEM((ns + 8, CMP_HIDDEN), F32)],
        compiler_params=_params("arbitrary", "arbitrary", "arbitrary"),
        name="compress_prompt",
    )(kv, w1ab, w1flat, pe8, b1, w2, b2)


NEG = -1e30
SEL_TILE = 512
WIN_KEYS = WINDOW + Q_BLOCK


def _alibi_slopes():
    h = np.arange(1, NSA_HEADS + 1, dtype=np.float32)
    return (2.0 ** (-8.0 * h / NSA_HEADS)).astype(np.float32)


def _overlap_t(ncb, ns):
    i = np.arange(ncb)[None, :] * CMP_STRIDE
    m = np.arange(ns)[:, None] * SLC_BLOCK
    return ((i < m + SLC_BLOCK) & (i + CMP_BLOCK > m)).astype(np.float32)


def _masked_softmax_rows(s, mask):
    sm = jnp.where(mask, s, NEG)
    mx = jnp.max(sm, axis=-1, keepdims=True)
    p = jnp.where(mask, jnp.exp(sm - mx), 0.0)
    return p / jnp.maximum(jnp.sum(p, axis=-1, keepdims=True), 1e-30)


def _select_blocks_t(imp_t, qpos_row, v_ref, n_live):
    ns, nq = imp_t.shape
    m_i = lax.broadcasted_iota(jnp.int32, (ns, nq), 0)
    valid = m_i * SLC_BLOCK <= qpos_row
    cur = qpos_row // SLC_BLOCK
    forced = (m_i == 0) | (m_i == cur) | (m_i == cur - 1)
    v = jnp.where(forced, jnp.inf, imp_t)
    v = jnp.where(valid, v, -jnp.inf)
    v_ref[...] = v

    def body(mp, rank):
        r = v_ref[pl.ds(mp, 1), :]
        ahead = (r > v) | ((r == v) & (mp < m_i))
        return rank + jnp.where(ahead, 1, 0)

    rank = lax.fori_loop(0, n_live, body, jnp.zeros((ns, nq), jnp.int32))
    return jnp.where((rank < SLC_TOPK) & valid, 1.0, 0.0)


BIG = 2.0 ** 20
POS_PERIOD = SEL_TILE


def _nsa_prompt_kernel(slopes_ref, q_ref, gate_ref, qpc_ref, kct_ref, vc_ref, kst_ref, va_ref, kwt_ref, vwp_ref,
                       pos_ref, blk_ref, ovl_ref, o_ref, v_ref, kaug_ref, kwaug_ref, m_ref, acc_ref):
    g = pl.program_id(1)
    qi = pl.program_id(2)
    nq, dh, J = Q_BLOCK, NSA_HEAD_DIM, NSA_HPG
    ns, ncb = ovl_ref.shape
    L = kst_ref.shape[3]
    start = qi * nq
    slopes = [slopes_ref[g * J + j] for j in range(J)]
    rows = [slice(j * nq, (j + 1) * nq) for j in range(J)]

    @pl.when(qi == 0)
    def _():
        kaug_ref[0:dh, :] = kst_ref[0, 0]
        kaug_ref[dh:LANES, :] = pos_ref[...]
        kaug_ref[LANES:, :] = blk_ref[...]
        kwaug_ref[0:dh, 0:WINDOW] = jnp.zeros((dh, WINDOW), BF16)
        kwaug_ref[dh:LANES, 0:WINDOW] = pos_ref[:, 0:WINDOW]
        kwaug_ref[0:dh, WINDOW:] = kwt_ref[0, 0]
        kwaug_ref[dh:LANES, WINDOW:] = pos_ref[...]

    qb = q_ref[0] * (dh ** -0.5)
    q_all = jnp.concatenate([qb[:, j * dh:(j + 1) * dh].astype(BF16) for j in range(J)], axis=0)
    q_pos = jnp.concatenate([q_all, qpc_ref[0].astype(BF16)], axis=1)
    gates = _sigmoid(gate_ref[0, 0])

    s_all = _dot(q_all, kct_ref[0, 0])
    rel_end = (lax.broadcasted_iota(jnp.int32, (1, ncb), 1) * CMP_STRIDE + (CMP_BLOCK - 1)) - start
    mask_c = lax.broadcasted_iota(jnp.int32, (nq, ncb), 0) >= rel_end
    rel_endf = rel_end.astype(F32)
    vc = vc_ref[0, 0]
    o_c, p_sum = [], jnp.zeros((nq, ncb), F32)
    for j in range(J):
        p = _masked_softmax_rows(s_all[rows[j]] + slopes[j] * rel_endf, mask_c)
        p_sum = p_sum + p
        o_c.append(_dot(p.astype(BF16), vc))

    ovl = ovl_ref[...]
    imp_t = sum(_dot_nt(ovl, part) for part in _split3(p_sum))
    qpos_row = start + lax.broadcasted_iota(jnp.int32, (ns, nq), 1)
    sel_t = _select_blocks_t(imp_t, qpos_row, v_ref, start // SLC_BLOCK + nq // SLC_BLOCK)
    not_q = ((sel_t.T - 1.0) * BIG).astype(BF16)
    q_aug = jnp.concatenate([q_pos, jnp.concatenate([not_q] * J, axis=0)], axis=1)

    tk = SEL_TILE
    m_ref[...] = jnp.full(m_ref.shape, NEG, F32)
    acc_ref[...] = jnp.zeros(acc_ref.shape, F32)
    causal = (lax.broadcasted_iota(jnp.int32, (nq, tk), 0) - lax.broadcasted_iota(jnp.int32, (nq, tk), 1))

    def sel_tile(k0, on_diagonal):
        s = _dot(q_aug, kaug_ref[:, pl.ds(k0, tk)])
        vt = va_ref[0, 0, pl.ds(k0, tk), :]
        shift = (k0 - start).astype(F32)
        for j in range(J):
            sj = s[rows[j]]
            if on_diagonal:
                sj = jnp.where(causal + (start - k0) >= 0, sj, -BIG)
            c = slopes[j] * shift
            m_old = m_ref[rows[j], :]
            m_new = jnp.maximum(m_old, jnp.max(sj, axis=-1, keepdims=True) + c)
            p = jnp.exp(sj - (m_new - c))
            acc_ref[rows[j], :] = acc_ref[rows[j], :] * jnp.exp(m_old - m_new) + _dot(p.astype(BF16), vt)
            m_ref[rows[j], :] = m_new

    n_full = start // tk
    bpt = tk // SLC_BLOCK
    v_ref[...] = sel_t

    def full_tile(kt, carry):
        chosen_here = jnp.max(v_ref[pl.ds(pl.multiple_of(kt * bpt, bpt), bpt), :])

        @pl.when(chosen_here > 0.5)
        def _():
            sel_tile(pl.multiple_of(kt * tk, tk), False)

        return carry

    lax.fori_loop(0, n_full, full_tile, 0)
    sel_tile(pl.multiple_of(n_full * tk, tk), True)

    s_w = _dot(q_pos, kwaug_ref[:, pl.ds(pl.multiple_of(start, nq), WIN_KEYS)])
    vw = vwp_ref[0, 0, pl.ds(pl.multiple_of(start, nq), WIN_KEYS), :]
    qrow = lax.broadcasted_iota(jnp.int32, (nq, WIN_KEYS), 0)
    wcol = lax.broadcasted_iota(jnp.int32, (nq, WIN_KEYS), 1)
    mask_w = (wcol > qrow) & (wcol - WINDOW <= qrow)
    chunk = lax.broadcasted_iota(jnp.int32, (1, WIN_KEYS), 1) // nq
    p0 = start - WINDOW + chunk * nq
    origin = (jnp.maximum(p0, 0) // POS_PERIOD) * POS_PERIOD - start
    for j in range(J):
        bias = jnp.where(p0 >= 0, slopes[j] * origin.astype(F32), -BIG)
        sm = jnp.where(mask_w, s_w[rows[j]] + bias, NEG)
        p = jnp.exp(sm - jnp.max(sm, axis=-1, keepdims=True))
        o_w = _dot(p.astype(BF16), vw) / jnp.sum(p, axis=-1, keepdims=True)
        acc = acc_ref[rows[j], :]
        o_s = acc[:, 0:dh] / jnp.maximum(acc[:, dh:dh + 1], 1e-30)
        out = (gates[:, 3 * j:3 * j + 1] * o_c[j] + gates[:, 3 * j + 1:3 * j + 2] * o_s
               + gates[:, 3 * j + 2:3 * j + 3] * o_w)
        o_ref[0, :, j * dh:(j + 1) * dh] = out.astype(o_ref.dtype)


def _slope_columns():
    s = jnp.asarray(_alibi_slopes())
    pieces = jnp.stack([p.astype(F32) for p in _split3(s)], axis=1)
    cols = jnp.concatenate([pieces * SLC_BLOCK, pieces,
                            jnp.zeros((NSA_HEADS, NSA_HEAD_DIM - 6), F32)], axis=1)
    return jnp.repeat(cols, Q_BLOCK, axis=0).reshape(NSA_KV_HEADS, NSA_HPG * Q_BLOCK, NSA_HEAD_DIM)


def _position_rows(L):
    t = np.arange(L) % POS_PERIOD
    rows = np.zeros((NSA_HEAD_DIM, L), np.float32)
    rows[0:3] = t // SLC_BLOCK
    rows[3:6] = t % SLC_BLOCK
    return jnp.asarray(rows, BF16)


def _block_rows(ns, L):
    return jnp.asarray((np.arange(ns)[:, None] == np.arange(L)[None, :] // SLC_BLOCK).astype(np.float32), BF16)


def _nsa_prompt(q, gates_raw, kct, vc, kst, va, kwt, vwp):
    b, L, _ = q.shape
    G, dh, J = NSA_KV_HEADS, NSA_HEAD_DIM, NSA_HPG
    ncb, ns = L // CMP_STRIDE, L // SLC_BLOCK
    nqb = L // Q_BLOCK
    per_bg = lambda shape: pl.BlockSpec((1, 1) + shape, lambda i, g, t, s: (i, g, 0, 0))
    const = lambda shape: pl.BlockSpec(shape, lambda i, g, t, s: (0,) * len(shape))
    grid_spec = pltpu.PrefetchScalarGridSpec(
        num_scalar_prefetch=1,
        grid=(b, G, nqb),
        in_specs=[pl.BlockSpec((1, Q_BLOCK, J * dh), lambda i, g, t, s: (i, t, g)),
                  pl.BlockSpec((1, 1, Q_BLOCK, 3 * J), lambda i, g, t, s: (i, g, t, 0)),
                  pl.BlockSpec((1, J * Q_BLOCK, dh), lambda i, g, t, s: (g, 0, 0)),
                  per_bg((dh, ncb)), per_bg((ncb, dh)), per_bg((dh, L)), per_bg((L, LANES)),
                  per_bg((dh, L)), per_bg((WINDOW + L, dh)),
                  const((dh, L)), const((ns, L)), const((ns, ncb))],
        out_specs=pl.BlockSpec((1, Q_BLOCK, J * dh), lambda i, g, t, s: (i, t, g)),
        scratch_shapes=[pltpu.VMEM((ns, Q_BLOCK), F32),
                        pltpu.VMEM((LANES + ns, L), BF16), pltpu.VMEM((LANES, WINDOW + L), BF16),
                        pltpu.VMEM((J * Q_BLOCK, 1), F32), pltpu.VMEM((J * Q_BLOCK, LANES), F32)],
    )
    return pl.pallas_call(
        _nsa_prompt_kernel,
        grid_spec=grid_spec,
        out_shape=jax.ShapeDtypeStruct((b, L, G * J * dh), BF16),
        compiler_params=_params("arbitrary", "arbitrary", "arbitrary"),
        name="nsa_prompt",
    )(jnp.asarray(_alibi_slopes()), q, gates_raw, _slope_columns(), kct, vc, kst, va, kwt, vwp,
      _position_rows(L), _block_rows(ns, L), jnp.asarray(_overlap_t(ncb, ns), BF16))


def _nsa_prompt_branch(q, kv, small, cw):
    b, L, _ = q.shape
    G, dh = NSA_KV_HEADS, NSA_HEAD_DIM
    cmp = _compress_prompt(kv, cw).astype(BF16)
    kct = cmp[:, 0].transpose(0, 1, 3, 2)
    vc = cmp[:, 1]
    kv5 = kv.astype(BF16).reshape(b, L, 6, G, dh)
    rows = lambda w: kv5[:, :, w].transpose(0, 2, 1, 3)
    cols = lambda w: kv5[:, :, w].transpose(0, 2, 3, 1)
    va = jnp.concatenate([rows(3), jnp.ones((b, G, L, 1), BF16), jnp.zeros((b, G, L, LANES - dh - 1), BF16)], axis=-1)
    vwp = jnp.pad(rows(5), ((0, 0), (0, 0), (WINDOW, 0), (0, 0)))
    gates_raw = small[:, :, SSD_HEADS:SSD_HEADS + 3 * NSA_HEADS].reshape(b, L, G, 3 * NSA_HPG).transpose(0, 2, 1, 3)
    return _nsa_prompt(q, gates_raw, kct, vc, cols(2), va, cols(4), vwp)


def _layer_norm(v, g, b):
    mu = jnp.mean(v, axis=-1, keepdims=True)
    c = v - mu
    var = jnp.mean(c * c, axis=-1, keepdims=True)
    return c * lax.rsqrt(var + NORM_EPS) * g + b


def _route_t(logits_t, bias_col, v_ref):
    e_n, n = logits_t.shape
    gsz = e_n // N_ROUTE_GROUPS
    scores = _sigmoid(logits_t)
    biased = scores + bias_col
    grp = []
    for a in range(N_ROUTE_GROUPS):
        blk = biased[a * gsz:(a + 1) * gsz]
        m1 = jnp.max(blk, axis=0, keepdims=True)
        cnt = jnp.sum(jnp.where(blk == m1, 1.0, 0.0), axis=0, keepdims=True)
        m2 = jnp.max(jnp.where(blk < m1, blk, -jnp.inf), axis=0, keepdims=True)
        grp.append(m1 + jnp.where(cnt >= 2.0, m1, m2))
    pieces = []
    for a in range(N_ROUTE_GROUPS):
        rank = jnp.zeros((1, n), jnp.int32)
        for b in range(N_ROUTE_GROUPS):
            if b != a:
                ahead = (grp[b] > grp[a]) | ((grp[b] == grp[a]) & (b < a))
                rank = rank + jnp.where(ahead, 1, 0)
        pieces.append(jnp.where(rank < TOPK_ROUTE_GROUPS, biased[a * gsz:(a + 1) * gsz], -jnp.inf))
    masked = jnp.concatenate(pieces, axis=0)
    v_ref[...] = masked
    e_i = lax.broadcasted_iota(jnp.int32, (e_n, n), 0)

    def body(ep, rank):
        r = v_ref[pl.ds(ep, 1), :]
        ahead = (r > masked) | ((r == masked) & (ep < e_i))
        return rank + jnp.where(ahead, 1, 0)

    rank = lax.fori_loop(0, e_n, body, jnp.zeros((e_n, n), jnp.int32))
    picked = jnp.where(rank < TOP_K, scores, 0.0)
    return picked / jnp.sum(picked, axis=0, keepdims=True) * ROUTE_SCALE


def _finish_kernel(x_ref, sy_ref, ny_ref, gm_ref, wsd_ref, wnd_ref, wo_ref, bo_ref, g1_ref, b1_ref,
                   wrh_ref, wrl_ref, br_ref, h_ref, hb_ref, gt_ref, v_ref):
    gate = _sigmoid(gm_ref[...])
    mixed = (gate[:, :D_MODEL] * _dot(sy_ref[...], wsd_ref[...])
             + gate[:, D_MODEL:] * _dot(ny_ref[...], wnd_ref[...]))
    o = _dot(mixed.astype(BF16), wo_ref[...]) + bo_ref[...]
    h = _layer_norm(ALPHA * x_ref[...] + o, g1_ref[...], b1_ref[...])
    h_ref[...] = h
    hb_ref[...] = h.astype(BF16)
    h_hi, h_lo = _split2(h)
    logits_t = _dot_nt(wrh_ref[...], h_hi) + _dot_nt(wrh_ref[...], h_lo) + _dot_nt(wrl_ref[...], h_hi)
    gt_ref[...] = _route_t(logits_t, br_ref[...], v_ref)


def _finish(x2d, ssd_y, nsa_y, gm, fw, tm):
    t = x2d.shape[0]
    wsd, wnd, wo, bo, g1, b1, wrh, wrl, br = fw
    row = lambda w: pl.BlockSpec((tm, w), lambda i: (i, 0))
    full = lambda a: pl.BlockSpec(a.shape, lambda i: (0,) * a.ndim)
    return pl.pallas_call(
        _finish_kernel,
        grid=(t // tm,),
        in_specs=[row(D_MODEL), row(SSD_D_INNER), row(D_MODEL), row(2 * D_MODEL)] + [full(a) for a in fw],
        out_specs=[row(D_MODEL), row(D_MODEL), pl.BlockSpec((N_EXPERTS, tm), lambda i: (0, i))],
        out_shape=[jax.ShapeDtypeStruct((t, D_MODEL), F32), jax.ShapeDtypeStruct((t, D_MODEL), BF16),
                   jax.ShapeDtypeStruct((N_EXPERTS, t), F32)],
        scratch_shapes=[pltpu.VMEM((N_EXPERTS, tm), F32)],
        compiler_params=_params("arbitrary"),
        name="finish",
    )(x2d, ssd_y, nsa_y, gm, *fw)


def _finish_weights(w_ssd_down, w_nsa_down, w_out, b_out, ln1_g, ln1_b, w_router, b_router):
    wr_t = w_router.astype(F32).T
    wrh = wr_t.astype(BF16)
    wrl = (wr_t - wrh.astype(F32)).astype(BF16)
    return (w_ssd_down.astype(BF16), w_nsa_down.astype(BF16), w_out.astype(BF16), b_out.astype(F32)[None, :],
            ln1_g.astype(F32)[None, :], ln1_b.astype(F32)[None, :], wrh, wrl, b_router.astype(F32)[:, None])


def _swiglu(x, w1, w3):
    return _silu(_dot(x, w1)) * _dot(x, w3)


def _moe_dense_kernel(hb_ref, h_ref, g_ref, w1_ref, w3_ref, w2_ref, ws1_ref, ws3_ref, ws2_ref, g2_ref, b2_ref,
                      y_ref, acc_ref):
    e = pl.program_id(1)
    x = hb_ref[...]

    @pl.when(e == 0)
    def _():
        acc_ref[...] = _dot(_swiglu(x, ws1_ref[...], ws3_ref[...]).astype(BF16), ws2_ref[...])

    pick = jnp.where(lax.broadcasted_iota(jnp.int32, (N_EXPERTS, EXPERT_FF), 0) == e, 1.0, 0.0).astype(BF16)
    g_hi, g_lo = _split2(g_ref[...])
    gate = _dot(g_hi, pick) + _dot(g_lo, pick)
    a = jnp.where(gate > 0.0, _swiglu(x, w1_ref[0], w3_ref[0]) * gate, 0.0)
    acc_ref[...] += _dot(a.astype(BF16), w2_ref[0])

    @pl.when(e == pl.num_programs(1) - 1)
    def _():
        y_ref[...] = _layer_norm(ALPHA * h_ref[...] + acc_ref[...], g2_ref[...], b2_ref[...])


def _moe_dense(hb, h, gates, mw, tm):
    t = h.shape[0]
    w1, w3, w2, ws1, ws3, ws2, g2, b2 = mw
    row = lambda w: pl.BlockSpec((tm, w), lambda i, e: (i, 0))
    full = lambda a: pl.BlockSpec(a.shape, lambda i, e: (0,) * a.ndim)
    return pl.pallas_call(
        _moe_dense_kernel,
        grid=(t // tm, N_EXPERTS),
        in_specs=[row(D_MODEL), row(D_MODEL), row(N_EXPERTS),
                  pl.BlockSpec((1, D_MODEL, EXPERT_FF), lambda i, e: (e, 0, 0)),
                  pl.BlockSpec((1, D_MODEL, EXPERT_FF), lambda i, e: (e, 0, 0)),
                  pl.BlockSpec((1, EXPERT_FF, D_MODEL), lambda i, e: (e, 0, 0)),
                  full(ws1), full(ws3), full(ws2), full(g2), full(b2)],
        out_specs=row(D_MODEL),
        out_shape=jax.ShapeDtypeStruct((t, D_MODEL), F32),
        scratch_shapes=[pltpu.VMEM((tm, D_MODEL), F32)],
        compiler_params=_params("arbitrary", "arbitrary"),
        name="moe_dense",
    )(hb, h, gates, *mw)


def _moe_weights(w_e1, w_e3, w_e2, w_s1, w_s3, w_s2, ln2_g, ln2_b):
    return (w_e1.astype(BF16), w_e3.astype(BF16), w_e2.astype(BF16), w_s1.astype(BF16), w_s3.astype(BF16),
            w_s2.astype(BF16), ln2_g.astype(F32)[None, :], ln2_b.astype(F32)[None, :])


def _finish_and_moe(x2d, ssd_y, nsa_y, gm, fw, mw, tm_finish, tm_moe):
    h, hb, gates_t = _finish(x2d, ssd_y, nsa_y, gm, fw, tm_finish)
    return _moe_dense(hb, h, gates_t.T, mw, tm_moe)


def _ssd_sample_kernel(xbc_ref, sc_ref, z_ref, sm_ref, st_ref, cw_ref, cb_ref, dtb_ref, alog_ref, dskip_ref,
                       nw_ref, e_ref, y_ref, sto_ref, xdt_t_ref, dec_t_ref, b_ref, c_ref, xs_ref, yt_ref):
    i = pl.program_id(0)
    db = xbc_ref.shape[0]
    G, N = SSD_GROUPS, SSD_STATE
    GW = SSD_HPG * SSD_HEAD_DIM

    @pl.when(i == 0)
    def _():
        conv = cb_ref[...] + xbc_ref[...] * cw_ref[SSD_CONV - 1:SSD_CONV, :]
        for k in range(SSD_CONV - 1):
            conv = conv + sc_ref[k] * cw_ref[k:k + 1, :]
        act = _silu(conv)
        xs = act[:, :SSD_D_INNER]
        xs_ref[...] = xs
        b_ref[...] = act[:, SSD_D_INNER:SSD_D_INNER + G * N].astype(BF16)
        c_ref[...] = act[:, SSD_D_INNER + G * N:].astype(BF16)
        dt = _softplus(sm_ref[...] + dtb_ref[...])
        dec = jnp.exp(dt * (-jnp.exp(alog_ref[...])))
        e_mat = e_ref[...]
        xdt_t_ref[...] = (xs * _dot_exact_rhs(dt, e_mat)).T.astype(BF16)
        dec_t_ref[...] = _dot_exact_rhs(dec, e_mat).T
        yt_ref[...] = jnp.zeros_like(yt_ref)

    is_row = lax.broadcasted_iota(jnp.int32, (db, N), 0) == i
    onehot = jnp.where(is_row, 1.0, 0.0).astype(BF16)
    is_lane = lax.broadcasted_iota(jnp.int32, (GW, db), 1) == i
    for g in range(G):
        rows = slice(g * GW, (g + 1) * GW)
        b_g = jnp.where(is_row, b_ref[:, g * N:(g + 1) * N], jnp.zeros((), BF16))
        contrib = _dot(xdt_t_ref[rows, :], b_g)
        decay = _dot_exact_rhs(dec_t_ref[rows, :], onehot)
        new = st_ref[0, rows, :] * decay + contrib
        sto_ref[0, rows, :] = new
        y_all = _dot_nt(new.astype(BF16), c_ref[:, g * N:(g + 1) * N])
        yt_ref[rows, :] += jnp.where(is_lane, y_all, 0.0)

    @pl.when(i == pl.num_programs(0) - 1)
    def _():
        y = yt_ref[...].T + dskip_ref[...] * xs_ref[...]
        y = y * _silu(z_ref[...])
        for g in range(G):
            sl = slice(g * GW, (g + 1) * GW)
            yg = y[:, sl]
            rs = lax.rsqrt(jnp.mean(yg * yg, axis=-1, keepdims=True) + NORM_EPS)
            y_ref[:, sl] = (yg * rs * nw_ref[:, sl]).astype(y_ref.dtype)


def _ssd_sample(xbc, state_conv, z, small, state_ssm, conv_w, conv_b, dt_bias, a_log, d_skip, norm_w):
    db = xbc.shape[0]
    pad = LANES - SSD_HEADS
    dtb = jnp.pad(dt_bias.astype(F32), (0, pad))[None, :]
    alog = jnp.pad(a_log.astype(F32), (0, pad))[None, :]
    dskip = jnp.repeat(d_skip.astype(F32), SSD_HEAD_DIM)[None, :]
    sc = state_conv.astype(F32).transpose(1, 0, 2)
    st = state_ssm.reshape(db, SSD_D_INNER, SSD_STATE)
    args = (xbc, sc, z, small, st, conv_w.astype(F32), conv_b.astype(F32)[None, :], dtb, alog, dskip,
            norm_w.astype(F32)[None, :], _head_expand_matrix())
    full = lambda a: pl.BlockSpec(a.shape, lambda i: (0,) * a.ndim)
    st_spec = pl.BlockSpec((1, SSD_D_INNER, SSD_STATE), lambda i: (i, 0, 0))
    y, st_new = pl.pallas_call(
        _ssd_sample_kernel,
        grid=(db,),
        in_specs=[full(a) for a in args[:4]] + [st_spec] + [full(a) for a in args[5:]],
        out_specs=[pl.BlockSpec((db, SSD_D_INNER), lambda i: (0, 0)), st_spec],
        out_shape=[jax.ShapeDtypeStruct((db, SSD_D_INNER), BF16),
                   jax.ShapeDtypeStruct((db, SSD_D_INNER, SSD_STATE), F32)],
        scratch_shapes=[pltpu.VMEM((SSD_D_INNER, db), BF16), pltpu.VMEM((SSD_D_INNER, db), F32),
                        pltpu.VMEM((db, SSD_GROUPS * SSD_STATE), BF16),
                        pltpu.VMEM((db, SSD_GROUPS * SSD_STATE), BF16),
                        pltpu.VMEM((db, SSD_D_INNER), F32), pltpu.VMEM((SSD_D_INNER, db), F32)],
        compiler_params=_params("arbitrary"),
        name="ssd_sample",
    )(*args)
    return y, st_new.reshape(state_ssm.shape)


def _nsa_sample_kernel(n_pages, *refs):
    pt_ref = refs[0]
    page_refs = refs[1:1 + n_pages]
    (win_ref, q_ref, kvn_ref, gate_ref, slope_ref, ovl_ref, w1ab_ref, w1flat_ref, pe_ref, b1_ref, w2_ref,
     b2_ref, o_ref, wino_ref, cmp_rows_ref, ks_ref, vs_ref, kw_ref, vw_ref, bsh_ref, kc_ref, vc_ref) = refs[1 + n_pages:]
    del pt_ref
    H, dh, G = NSA_HEADS, NSA_HEAD_DIM, NSA_KV_HEADS
    past = n_pages * PAGE_SIZE
    ncb = past // CMP_STRIDE
    nsb = ovl_ref.shape[1]
    wbuf = win_ref.shape[3]
    tiles = KVW // LANES

    for p in range(n_pages):
        r = slice(p * PAGE_SIZE, (p + 1) * PAGE_SIZE)
        for w in range(2):
            rows_wp = page_refs[p][0, w].T
            for t in range(tiles):
                cmp_rows_ref[w * tiles + t, r, :] = rows_wp[:, t * LANES:(t + 1) * LANES]
        ks_ref[:, r] = page_refs[p][0, 2].astype(BF16)
        vs_ref[:, r] = page_refs[p][0, 3].astype(BF16)
    new8 = jnp.broadcast_to(kvn_ref[0], (8, 6 * KVW))
    first = lax.broadcasted_iota(jnp.int32, (KVW, LANES), 1) == 0

    def new_col(w):
        return new8[:, w * KVW:(w + 1) * KVW].T[:, 0:1]

    put = lambda w: jnp.where(first, new_col(w), 0.0).astype(BF16)
    ks_ref[:, past:past + LANES] = put(2)
    vs_ref[:, past:past + LANES] = put(3)
    kw_ref[:, 0:wbuf] = win_ref[0, 0].astype(BF16)
    vw_ref[:, 0:wbuf] = win_ref[0, 1].astype(BF16)
    kw_ref[:, wbuf:wbuf + LANES] = put(4)
    vw_ref[:, wbuf:wbuf + LANES] = put(5)
    last = lax.broadcasted_iota(jnp.int32, (KVW, wbuf), 1) == wbuf - 1
    for w in range(2):
        wino_ref[0, w] = jnp.where(last, new_col(4 + w), pltpu.roll(win_ref[0, w], wbuf - 1, 1))

    for w, dst in ((0, kc_ref), (1, vc_ref)):
        for t in range(tiles):
            def load_strip(l, w=w, t=t):
                return cmp_rows_ref[w * tiles + t, pl.ds(l, ncb, stride=CMP_STRIDE), :]

            def store(g, val, t=t, dst=dst):
                c0 = (t * GPT + g) * dh
                dst[:, c0:c0 + dh] = val.astype(BF16)

            sub = lambda ref: ref.at[pl.ds(w, 1)]
            _compress_body(load_strip, ncb, sub(w1ab_ref), sub(w1flat_ref), sub(pe_ref), sub(b1_ref), sub(w2_ref),
                           sub(b2_ref), bsh_ref, store)

    head_r = lax.broadcasted_iota(jnp.int32, (H, KVW), 0) // NSA_HPG
    lane_g = lax.broadcasted_iota(jnp.int32, (H, KVW), 1) // dh
    diag = head_r == lane_g
    q16 = q_ref[0] * (dh ** -0.5)
    q_bd = jnp.where(diag, jnp.concatenate([q16] * G, axis=1), 0.0).astype(BF16)
    slopes = slope_ref[...]
    gates = _sigmoid(gate_ref[0])

    def fold(o):
        o = jnp.where(diag, o, 0.0)
        return sum(o[:, g * dh:(g + 1) * dh] for g in range(G))

    d_c = past - (lax.broadcasted_iota(jnp.int32, (H, ncb), 1) * CMP_STRIDE + (CMP_BLOCK - 1))
    p_c = _masked_softmax_rows(_dot_nt(q_bd, kc_ref[...]) - slopes * d_c.astype(F32), d_c >= 0)
    o_c = fold(_dot(p_c.astype(BF16), vc_ref[...]))

    p_grp = jnp.concatenate(
        [jnp.sum(p_c[g * NSA_HPG:(g + 1) * NSA_HPG], axis=0, keepdims=True) for g in range(G)]
        + [jnp.zeros((8 - G, ncb), F32)], axis=0)
    imp = sum(_dot(part, ovl_ref[...]) for part in _split3(p_grp))
    m_l = lax.broadcasted_iota(jnp.int32, (8, nsb), 1)
    valid = m_l * SLC_BLOCK <= past
    cur = past // SLC_BLOCK
    forced = (m_l == 0) | (m_l == cur) | (m_l == cur - 1)
    v = jnp.where(valid, jnp.where(forced, jnp.inf, imp), -jnp.inf)
    v_t = v.T
    mp_i = lax.broadcasted_iota(jnp.int32, (nsb, nsb), 0)
    m_i = lax.broadcasted_iota(jnp.int32, (nsb, nsb), 1)
    sel_rows = []
    for g in range(G):
        v_col = jnp.broadcast_to(v_t[:, g:g + 1], (nsb, nsb))
        v_row = jnp.broadcast_to(v[g:g + 1, :], (nsb, nsb))
        ahead = (v_col > v_row) | ((v_col == v_row) & (mp_i < m_i))
        rank = jnp.sum(jnp.where(ahead, 1.0, 0.0), axis=0, keepdims=True)
        sel = jnp.where((rank < SLC_TOPK) & valid[0:1, :], 1.0, 0.0)
        sel_rows.append(jnp.broadcast_to(sel, (NSA_HPG, nsb)))
    sel_h = jnp.concatenate(sel_rows, axis=0).astype(BF16)

    nk = past + LANES
    expand = jnp.where(lax.broadcasted_iota(jnp.int32, (nsb, nk), 0)
                       == lax.broadcasted_iota(jnp.int32, (nsb, nk), 1) // SLC_BLOCK, 1.0, 0.0).astype(BF16)
    d_s = past - lax.broadcasted_iota(jnp.int32, (H, nk), 1)
    ok = (_dot(sel_h, expand) > 0.5) & (d_s >= 0)
    p_s = _masked_softmax_rows(_dot(q_bd, ks_ref[...]) - slopes * d_s.astype(F32), ok)
    o_s = fold(_dot_nt(p_s.astype(BF16), vs_ref[...]))

    nw = wbuf + LANES
    d_w = wbuf - lax.broadcasted_iota(jnp.int32, (H, nw), 1)
    p_w = _masked_softmax_rows(_dot(q_bd, kw_ref[...]) - slopes * d_w.astype(F32), (d_w >= 0) & (d_w < WINDOW))
    o_w = fold(_dot_nt(p_w.astype(BF16), vw_ref[...]))

    o_ref[0] = (gates[:, 0:1] * o_c + gates[:, 1:2] * o_s + gates[:, 2:3] * o_w).astype(o_ref.dtype)


def _overlap_sample(ncb, nsb):
    i = np.arange(ncb)[:, None] * CMP_STRIDE
    m = np.arange(nsb)[None, :] * SLC_BLOCK
    ok = (i < m + SLC_BLOCK) & (i + CMP_BLOCK > m) & (np.arange(ncb)[:, None] < ncb - 1)
    return ok.astype(np.float32)


def _nsa_sample(q, kv_new, small, cache_kv_paged, page_table, cache_kv_win, cw):
    db = q.shape[0]
    H, dh, G = NSA_HEADS, NSA_HEAD_DIM, NSA_KV_HEADS
    n_pages = page_table.shape[1]
    past = n_pages * PAGE_SIZE
    wbuf = cache_kv_win.shape[1]
    ncb = past // CMP_STRIDE
    nsb = LANES * pl.cdiv(pl.cdiv(past + 1, SLC_BLOCK), LANES)
    pages = cache_kv_paged.transpose(0, 2, 3, 4, 1).reshape(cache_kv_paged.shape[0], 4, KVW, PAGE_SIZE)
    win = cache_kv_win.transpose(0, 2, 3, 4, 1).reshape(db, 2, KVW, wbuf)
    gates_raw = small[:, SSD_HEADS:SSD_HEADS + 3 * H].reshape(db, H, 3)
    slopes = jnp.asarray(_alibi_slopes())[:, None]
    ovl = jnp.asarray(_overlap_sample(ncb, nsb), BF16)
    consts = (slopes, ovl) + tuple(cw)
    per_seq = lambda shape: pl.BlockSpec((1,) + shape, lambda i, pt: (i,) + (0,) * len(shape))
    full = lambda a: pl.BlockSpec(a.shape, lambda i, pt: (0,) * a.ndim)
    page_spec = lambda p: pl.BlockSpec((1, 4, KVW, PAGE_SIZE), lambda i, pt: (pt[i, p], 0, 0, 0))
    grid_spec = pltpu.PrefetchScalarGridSpec(
        num_scalar_prefetch=1,
        grid=(db,),
        in_specs=[page_spec(p) for p in range(n_pages)]
        + [per_seq((2, KVW, wbuf)), per_seq((H, dh)), per_seq((1, 6 * KVW)), per_seq((H, 3))]
        + [full(a) for a in consts],
        out_specs=[per_seq((H, dh)), per_seq((2, KVW, wbuf))],
        scratch_shapes=[pltpu.VMEM((2 * KVW // LANES, past, LANES), F32),
                        pltpu.VMEM((KVW, past + LANES), BF16), pltpu.VMEM((KVW, past + LANES), BF16),
                        pltpu.VMEM((KVW, wbuf + LANES), BF16), pltpu.VMEM((KVW, wbuf + LANES), BF16),
                        pltpu.VMEM((ncb + 8, CMP_HIDDEN), F32),
                        pltpu.VMEM((ncb, KVW), BF16), pltpu.VMEM((ncb, KVW), BF16)],
    )
    out, win_new = pl.pallas_call(
        functools.partial(_nsa_sample_kernel, n_pages),
        grid_spec=grid_spec,
        out_shape=[jax.ShapeDtypeStruct((db, H, dh), BF16), jax.ShapeDtypeStruct((db, 2, KVW, wbuf), cache_kv_win.dtype)],
        compiler_params=_params("arbitrary"),
        name="nsa_sample",
    )(page_table.astype(jnp.int32), *([pages] * n_pages), win, q.reshape(db, H, dh), kv_new.reshape(db, 1, 6 * KVW),
      gates_raw, *consts)
    win_new = win_new.reshape(db, 2, G, dh, wbuf).transpose(0, 4, 1, 2, 3)
    return out.reshape(db, H * dh), win_new


def kernel(x_prompt, x_sample, cache_kv_paged, cache_kv_win, state_ssm, state_conv, page_table, w_in, b_in, conv_w, conv_b, dt_bias, a_log, d_skip, ssd_norm_w, cmp_w1, cmp_b1, cmp_w2, cmp_b2, cmp_pe, w_ssd_down, w_nsa_down, w_out, b_out, ln1_g, ln1_b, w_router, b_router, w_e1, w_e3, w_e2, w_s1, w_s3, w_s2, ln2_g, ln2_b):
    b, L, _ = x_prompt.shape
    db = x_sample.shape[0]
    G, dh = NSA_KV_HEADS, NSA_HEAD_DIM
    wbuf = cache_kv_win.shape[1]
    wp, bp = _pack_w_in(w_in, b_in)
    cw = _compress_weights(cmp_w1, cmp_b1, cmp_w2, cmp_b2, cmp_pe)
    fw = _finish_weights(w_ssd_down, w_nsa_down, w_out, b_out, ln1_g, ln1_b, w_router, b_router)
    mw = _moe_weights(w_e1, w_e3, w_e2, w_s1, w_s3, w_s2, ln2_g, ln2_b)
    ssd_w = (conv_w, conv_b, dt_bias, a_log, d_skip, ssd_norm_w)
    tm = min(256, b * L)
    per_seq = lambda a: a.reshape(b, L, a.shape[-1])

    xp = x_prompt.reshape(b * L, D_MODEL)
    z, xbc, q, kv, gm, small = _in_proj(xp, wp, bp, tm)
    ssd_y, st_t = _ssd_prompt(per_seq(xbc), per_seq(z), per_seq(small), *ssd_w)
    nsa_y = _nsa_prompt_branch(per_seq(q), per_seq(kv), per_seq(small), cw)
    yp = _finish_and_moe(xp, ssd_y.reshape(b * L, -1), nsa_y.reshape(b * L, -1), gm, fw, mw, tm,
                         min(1024, b * L)).reshape(b, L, D_MODEL)
    kv6 = kv.reshape(b, L, 6, G, dh)
    kv_rows_p = kv6[:, :, :4]
    win_p = kv6[:, L - wbuf:, 4:]
    ssm_p = st_t.reshape(b, SSD_STATE, SSD_HEADS, SSD_HEAD_DIM).transpose(0, 2, 3, 1).astype(state_ssm.dtype)
    conv_p = per_seq(xbc)[:, L - (SSD_CONV - 1):]

    xs = x_sample.reshape(db, D_MODEL)
    z, xbc, q, kv, gm, small = _in_proj(xs, wp, bp, db)
    ssd_y, ssm_s = _ssd_sample(xbc, state_conv, z, small, state_ssm, *ssd_w)
    nsa_y, win_s = _nsa_sample(q, kv, small, cache_kv_paged, page_table, cache_kv_win, cw)
    ys = _finish_and_moe(xs, ssd_y, nsa_y, gm, fw, mw, db, db).reshape(db, 1, D_MODEL)
    kv6 = kv.reshape(db, 1, 6, G, dh)
    kv_rows_s = kv6[:, :, :4]
    conv_s = jnp.concatenate([state_conv[:, 1:].astype(xbc.dtype), xbc[:, None, :]], axis=1)
    return (yp, ys, kv_rows_p, kv_rows_s, win_p, win_s, ssm_p, ssm_s, conv_p, conv_s)
```

```python
import functools

import numpy as np
import jax
import jax.numpy as jnp
from jax import lax
from jax.experimental import pallas as pl
from jax.experimental.pallas import tpu as pltpu

F32 = jnp.float32
BF16 = jnp.bfloat16

D_MODEL = 1024
PAGE_SIZE = 128
SSD_D_INNER = 2048
SSD_HEAD_DIM = 64
SSD_HEADS = 32
SSD_GROUPS = 4
SSD_HPG = 8
SSD_STATE = 128
SSD_CONV = 4
SSD_CHUNK = 128
SSD_CONV_CH = SSD_D_INNER + 2 * SSD_GROUPS * SSD_STATE
NSA_HEAD_DIM = 64
NSA_HEADS = 16
NSA_KV_HEADS = 4
NSA_HPG = 4
CMP_BLOCK = 32
CMP_STRIDE = 16
CMP_HIDDEN = 128
SLC_BLOCK = 64
SLC_TOPK = 16
WINDOW = 512
Q_BLOCK = 128
N_EXPERTS = 64
TOP_K = 8
N_ROUTE_GROUPS = 8
TOPK_ROUTE_GROUPS = 4
EXPERT_FF = 256
ROUTE_SCALE = 2.5
ALPHA = 2.0 ** 0.25
NORM_EPS = 1e-5
IN_SPLITS = (SSD_D_INNER, SSD_CONV_CH, SSD_HEADS, NSA_HEADS * NSA_HEAD_DIM,
             6 * NSA_KV_HEADS * NSA_HEAD_DIM, 3 * NSA_HEADS, 2 * D_MODEL)
IN_OFFSETS = tuple(int(v) for v in np.cumsum(IN_SPLITS)[:-1])

LANES = 128
VMEM_LIMIT = 56 * 1024 * 1024

SEG_Z = (0, 2048)
SEG_XBC = (2048, 3072)
SEG_Q = (5120, 1024)
SEG_KV = (6144, 1536)
SEG_GM = (7680, 2048)
SEG_SMALL = (9728, 128)
N_PACKED = 9856


def _params(*sem):
    return pltpu.CompilerParams(dimension_semantics=sem, vmem_limit_bytes=VMEM_LIMIT)


def _silu(x):
    return x * (1.0 / (1.0 + jnp.exp(-x)))


def _sigmoid(x):
    return 1.0 / (1.0 + jnp.exp(-x))


def _softplus(x):
    return jnp.maximum(x, 0.0) + jnp.log(1.0 + jnp.exp(-jnp.abs(x)))


def _split2(x):
    hi = x.astype(BF16)
    lo = (x - hi.astype(F32)).astype(BF16)
    return hi, lo


def _split3(x):
    hi = x.astype(BF16)
    r = x - hi.astype(F32)
    mid = r.astype(BF16)
    lo = (r - mid.astype(F32)).astype(BF16)
    return hi, mid, lo


def _dot(a, b):
    return jnp.dot(a, b, preferred_element_type=F32)


def _dot_nt(a, b):
    return lax.dot_general(a, b, (((1,), (1,)), ((), ())), preferred_element_type=F32)


def _dot_tn(a, b):
    return lax.dot_general(a, b, (((0,), (0,)), ((), ())), preferred_element_type=F32)


def _dot_exact_rhs(x, sel):
    a, b, c = _split3(x)
    return _dot(a, sel) + _dot(b, sel) + _dot(c, sel)


def _pack_w_in(w_in, b_in):
    def pack(m):
        z, xbc, dt, q, kv, gn, gm = jnp.split(m, list(IN_OFFSETS), axis=-1)
        pad = jnp.zeros(m.shape[:-1] + (LANES - SSD_HEADS - 3 * NSA_HEADS,), m.dtype)
        return jnp.concatenate([z, xbc, q, kv, gm, dt, gn, pad], axis=-1)
    return pack(w_in).astype(BF16), pack(b_in[None, :])


def _in_proj_kernel(x_ref, w_ref, b_ref, z_ref, xbc_ref, q_ref, kv_ref, gm_ref, sm_ref):
    x = x_ref[...].astype(BF16)
    for ref, (off, width) in ((z_ref, SEG_Z), (xbc_ref, SEG_XBC), (q_ref, SEG_Q), (kv_ref, SEG_KV),
                              (gm_ref, SEG_GM), (sm_ref, SEG_SMALL)):
        ref[...] = _dot(x, w_ref[:, off:off + width]) + b_ref[:, off:off + width]


def _in_proj(x2d, w_packed, b_packed, tm):
    t = x2d.shape[0]
    segs = (SEG_Z, SEG_XBC, SEG_Q, SEG_KV, SEG_GM, SEG_SMALL)
    return pl.pallas_call(
        _in_proj_kernel,
        grid=(t // tm,),
        in_specs=[pl.BlockSpec((tm, D_MODEL), lambda i: (i, 0)),
                  pl.BlockSpec((D_MODEL, N_PACKED), lambda i: (0, 0), pipeline_mode=pl.Buffered(1)),
                  pl.BlockSpec((1, N_PACKED), lambda i: (0, 0))],
        out_specs=[pl.BlockSpec((tm, w), lambda i: (i, 0)) for _, w in segs],
        out_shape=[jax.ShapeDtypeStruct((t, w), F32) for _, w in segs],
        compiler_params=_params("arbitrary"),
        name="in_proj",
    )(x2d, w_packed, b_packed)


def _head_expand_matrix():
    h = np.arange(LANES)[:, None]
    c = np.arange(SSD_D_INNER)[None, :] // SSD_HEAD_DIM
    return jnp.asarray((h == c).astype(np.float32), BF16)


def _ssd_prompt_kernel(xbc_ref, z_ref, sm_ref, cw_ref, cb_ref, dtb_ref, alog_ref, dskip_ref, nw_ref, e_ref,
                       y_ref, st_ref, xh_ref, state_ref, ybuf_ref):
    c = pl.program_id(1)
    L = SSD_CHUNK
    G, N, P = SSD_GROUPS, SSD_STATE, SSD_HEAD_DIM
    GW = SSD_HPG * P

    @pl.when(c == 0)
    def _():
        xh_ref[0:8, :] = jnp.zeros((8, SSD_CONV_CH), F32)
        state_ref[...] = jnp.zeros_like(state_ref)

    xh_ref[8:8 + L, :] = xbc_ref[0]
    conv = cb_ref[...] + xh_ref[5:5 + L, :] * cw_ref[0:1, :]
    for k in range(1, SSD_CONV):
        conv = conv + xh_ref[5 + k:5 + k + L, :] * cw_ref[k:k + 1, :]
    xh_ref[0:8, :] = xh_ref[L:L + 8, :]
    act = _silu(conv)
    xs = act[:, :SSD_D_INNER]
    bm = act[:, SSD_D_INNER:SSD_D_INNER + G * N].astype(BF16)
    cm = act[:, SSD_D_INNER + G * N:]

    dt = _softplus(sm_ref[0] + dtb_ref[...])
    da = dt * (-jnp.exp(alog_ref[...]))
    row_i = lax.broadcasted_iota(jnp.int32, (L, L), 0)
    col_i = lax.broadcasted_iota(jnp.int32, (L, L), 1)
    tri = row_i >= col_i
    a_cs = _dot_exact_rhs_left(tri, da)
    a_cs_t = a_cs.T
    dt_t = dt.T
    a_last = a_cs[L - 1:L, :]
    w_end = dt * jnp.exp(a_last - a_cs)
    chunk_decay = jnp.broadcast_to(jnp.exp(a_last), (8, LANES))
    e_mat = e_ref[...]
    w_exp = _dot_exact_rhs(w_end, e_mat)
    dec_exp = _dot_exact_rhs(chunk_decay, e_mat)[0:1, :]
    xd = (xs * w_exp).astype(BF16)

    for g in range(G):
        cg = cm[:, g * N:(g + 1) * N]
        bg = bm[:, g * N:(g + 1) * N]
        cb = _dot_nt(cg.astype(BF16), bg)
        for j in range(SSD_HPG):
            h = g * SSD_HPG + j
            col = jnp.broadcast_to(a_cs[:, h:h + 1], (L, L))
            row = a_cs_t[h:h + 1, :]
            lm = jnp.where(tri, jnp.exp(col - row), 0.0)
            m = (cb * lm * dt_t[h:h + 1, :]).astype(BF16)
            eac = (jnp.exp(col) * cg).astype(BF16)
            lhs = jnp.concatenate([m, eac], axis=1)
            rhs = jnp.concatenate([xs[:, h * P:(h + 1) * P].astype(BF16),
                                   state_ref[:, h * P:(h + 1) * P].astype(BF16)], axis=0)
            ybuf_ref[:, h * P:(h + 1) * P] = _dot(lhs, rhs)
        sl = slice(g * GW, (g + 1) * GW)
        state_ref[:, sl] = state_ref[:, sl] * dec_exp[:, sl] + _dot_tn(bg, xd[:, sl])

    y = ybuf_ref[...] + dskip_ref[...] * xs
    y = y * _silu(z_ref[0])
    for g in range(G):
        sl = slice(g * GW, (g + 1) * GW)
        yg = y[:, sl]
        rs = lax.rsqrt(jnp.mean(yg * yg, axis=-1, keepdims=True) + NORM_EPS)
        y_ref[0, :, sl] = (yg * rs * nw_ref[:, sl]).astype(y_ref.dtype)

    @pl.when(c == pl.num_programs(1) - 1)
    def _():
        st_ref[0] = state_ref[...]


def _dot_exact_rhs_left(mask, x):
    sel = jnp.where(mask, 1.0, 0.0).astype(BF16)
    a, b, c = _split3(x)
    return _dot(sel, a) + _dot(sel, b) + _dot(sel, c)


def _ssd_prompt(xbc, z, small, conv_w, conv_b, dt_bias, a_log, d_skip, norm_w):
    b, L, _ = xbc.shape
    nc = L // SSD_CHUNK
    pad = LANES - SSD_HEADS
    dtb = jnp.pad(dt_bias.astype(F32), (0, pad))[None, :]
    alog = jnp.pad(a_log.astype(F32), (0, pad))[None, :]
    dskip = jnp.repeat(d_skip.astype(F32), SSD_HEAD_DIM)[None, :]
    full = lambda shape: pl.BlockSpec(shape, lambda i, j: (0,) * len(shape))
    y, st = pl.pallas_call(
        _ssd_prompt_kernel,
        grid=(b, nc),
        in_specs=[pl.BlockSpec((1, SSD_CHUNK, SSD_CONV_CH), lambda i, j: (i, j, 0)),
                  pl.BlockSpec((1, SSD_CHUNK, SSD_D_INNER), lambda i, j: (i, j, 0)),
                  pl.BlockSpec((1, SSD_CHUNK, LANES), lambda i, j: (i, j, 0)),
                  full((SSD_CONV, SSD_CONV_CH)), full((1, SSD_CONV_CH)), full((1, LANES)), full((1, LANES)),
                  full((1, SSD_D_INNER)), full((1, SSD_D_INNER)), full((LANES, SSD_D_INNER))],
        out_specs=[pl.BlockSpec((1, SSD_CHUNK, SSD_D_INNER), lambda i, j: (i, j, 0)),
                   pl.BlockSpec((1, SSD_STATE, SSD_D_INNER), lambda i, j: (i, 0, 0))],
        out_shape=[jax.ShapeDtypeStruct((b, L, SSD_D_INNER), BF16),
                   jax.ShapeDtypeStruct((b, SSD_STATE, SSD_D_INNER), F32)],
        scratch_shapes=[pltpu.VMEM((SSD_CHUNK + 8, SSD_CONV_CH), F32),
                        pltpu.VMEM((SSD_STATE, SSD_D_INNER), F32),
                        pltpu.VMEM((SSD_CHUNK, SSD_D_INNER), F32)],
        compiler_params=_params("arbitrary", "arbitrary"),
        name="ssd_prompt",
    )(xbc, z, small, conv_w.astype(F32), conv_b.astype(F32)[None, :], dtb, alog, dskip,
      norm_w.astype(F32)[None, :], _head_expand_matrix())
    return y, st


HALF = CMP_BLOCK // 2
KVW = NSA_KV_HEADS * NSA_HEAD_DIM


def _compress_weights(cmp_w1, cmp_b1, cmp_w2, cmp_b2, cmp_pe):
    w1 = cmp_w1.astype(BF16)
    w1ab = jnp.concatenate([w1[:, :HALF], w1[:, HALF:]], axis=-1)
    w1flat = w1.reshape(2, CMP_BLOCK * NSA_HEAD_DIM, CMP_HIDDEN)
    pe8 = jnp.broadcast_to(cmp_pe.reshape(2, 1, CMP_BLOCK * NSA_HEAD_DIM), (2, 8, CMP_BLOCK * NSA_HEAD_DIM))
    return (w1ab, w1flat, pe8.astype(F32), cmp_b1.astype(F32)[:, None, :], cmp_w2.astype(BF16),
            cmp_b2.astype(F32)[:, None, :])


GPT = LANES // NSA_HEAD_DIM


def _compress_body(load_strip, ns, w1ab_ref, w1flat_ref, pe_ref, b1_ref, w2_ref, b2_ref, bsh_ref, store):
    accs = [jnp.zeros((ns, 2 * CMP_HIDDEN), F32) for _ in range(GPT)]
    for l in range(HALF):
        xl = load_strip(l).astype(BF16)
        for g in range(GPT):
            accs[g] = accs[g] + _dot(xl[:, g * NSA_HEAD_DIM:(g + 1) * NSA_HEAD_DIM], w1ab_ref[0, l])
    cpe = _dot(pe_ref[0].astype(BF16), w1flat_ref[0])[0:1, :] + b1_ref[0]
    bsh_ref[ns:ns + 8, :] = jnp.zeros((8, CMP_HIDDEN), F32)
    for g in range(GPT):
        bsh_ref[0:ns, :] = accs[g][:, CMP_HIDDEN:]
        hid = _silu(accs[g][:, :CMP_HIDDEN] + bsh_ref[1:ns + 1, :] + cpe)
        store(g, _dot(hid.astype(BF16), w2_ref[0]) + b2_ref[0])


def _compress_prompt_kernel(kv_ref, w1ab_ref, w1flat_ref, pe_ref, b1_ref, w2_ref, b2_ref, out_ref, bsh_ref):
    ns = out_ref.shape[3]

    def load_strip(l):
        return kv_ref[0, pl.ds(l, ns, stride=CMP_STRIDE), :]

    def store(g, val):
        out_ref[0, 0, g] = val

    _compress_body(load_strip, ns, w1ab_ref, w1flat_ref, pe_ref, b1_ref, w2_ref, b2_ref, bsh_ref, store)


def _compress_prompt(kv, cw):
    b, L, _ = kv.shape
    ns = L // CMP_STRIDE
    w1ab, w1flat, pe8, b1, w2, b2 = cw
    tiles = KVW // LANES
    per_which = lambda shape: pl.BlockSpec((1,) + shape, lambda i, w, t: (w,) + (0,) * len(shape))
    return pl.pallas_call(
        _compress_prompt_kernel,
        grid=(b, 2, tiles),
        in_specs=[pl.BlockSpec((1, L, LANES), lambda i, w, t: (i, 0, w * tiles + t)),
                  per_which((HALF, NSA_HEAD_DIM, 2 * CMP_HIDDEN)),
                  per_which((CMP_BLOCK * NSA_HEAD_DIM, CMP_HIDDEN)),
                  per_which((8, CMP_BLOCK * NSA_HEAD_DIM)),
                  per_which((1, CMP_HIDDEN)),
                  per_which((CMP_HIDDEN, NSA_HEAD_DIM)),
                  per_which((1, NSA_HEAD_DIM))],
        out_specs=pl.BlockSpec((1, 1, GPT, ns, NSA_HEAD_DIM), lambda i, w, t: (i, w, t, 0, 0)),
        out_shape=jax.ShapeDtypeStruct((b, 2, NSA_KV_HEADS, ns, NSA_HEAD_DIM), F32),
        scratch_shapes=[pltpu.VMEM((ns + 8, CMP_HIDDEN), F32)],
        compiler_params=_params("arbitrary", "arbitrary", "arbitrary"),
        name="compress_prompt",
    )(kv, w1ab, w1flat, pe8, b1, w2, b2)


NEG = -1e30
SEL_TILE = 512
WIN_KEYS = WINDOW + Q_BLOCK


def _alibi_slopes():
    h = np.arange(1, NSA_HEADS + 1, dtype=np.float32)
    return (2.0 ** (-8.0 * h / NSA_HEADS)).astype(np.float32)


def _overlap_t(ncb, ns):
    i = np.arange(ncb)[None, :] * CMP_STRIDE
    m = np.arange(ns)[:, None] * SLC_BLOCK
    return ((i < m + SLC_BLOCK) & (i + CMP_BLOCK > m)).astype(np.float32)


def _masked_softmax_rows(s, mask):
    sm = jnp.where(mask, s, NEG)
    mx = jnp.max(sm, axis=-1, keepdims=True)
    p = jnp.where(mask, jnp.exp(sm - mx), 0.0)
    return p / jnp.maximum(jnp.sum(p, axis=-1, keepdims=True), 1e-30)


def _select_blocks_t(imp_t, qpos_row, v_ref, n_live):
    ns, nq = imp_t.shape
    m_i = lax.broadcasted_iota(jnp.int32, (ns, nq), 0)
    valid = m_i * SLC_BLOCK <= qpos_row
    cur = qpos_row // SLC_BLOCK
    forced = (m_i == 0) | (m_i == cur) | (m_i == cur - 1)
    v = jnp.where(forced, jnp.inf, imp_t)
    v = jnp.where(valid, v, -jnp.inf)
    v_ref[...] = v

    def body(mp, rank):
        r = v_ref[pl.ds(mp, 1), :]
        tie = jnp.where(mp < m_i, 1.0, 0.0)
        return rank + jnp.where(r > v, 1.0, jnp.where(r == v, tie, 0.0))

    rank = lax.fori_loop(0, n_live, body, jnp.zeros((ns, nq), F32))
    return jnp.where(valid, jnp.where(rank < SLC_TOPK, 1.0, 0.0), 0.0)


BIG = 2.0 ** 20
POS_PERIOD = SEL_TILE


def _nsa_prompt_kernel(slopes_ref, q_ref, gate_ref, qpc_ref, kct_ref, vc_ref, kst_ref, va_ref, kwt_ref, vwp_ref,
                       pos_ref, blk_ref, ovl_ref, o_ref, v_ref, kaug_ref, kwaug_ref, m_ref, acc_ref):
    g = pl.program_id(1)
    qi = pl.program_id(2)
    nq, dh, J = Q_BLOCK, NSA_HEAD_DIM, NSA_HPG
    ns, ncb = ovl_ref.shape
    L = kst_ref.shape[3]
    start = qi * nq
    slopes = [slopes_ref[g * J + j] for j in range(J)]
    rows = [slice(j * nq, (j + 1) * nq) for j in range(J)]

    @pl.when(qi == 0)
    def _():
        kaug_ref[0:dh, :] = kst_ref[0, 0]
        kaug_ref[dh:LANES, :] = pos_ref[...]
        kaug_ref[LANES:, :] = blk_ref[...]
        kwaug_ref[0:dh, 0:WINDOW] = jnp.zeros((dh, WINDOW), BF16)
        kwaug_ref[dh:LANES, 0:WINDOW] = pos_ref[:, 0:WINDOW]
        kwaug_ref[0:dh, WINDOW:] = kwt_ref[0, 0]
        kwaug_ref[dh:LANES, WINDOW:] = pos_ref[...]

    qb = q_ref[0] * (dh ** -0.5)
    q_all = jnp.concatenate([qb[:, j * dh:(j + 1) * dh].astype(BF16) for j in range(J)], axis=0)
    q_pos = jnp.concatenate([q_all, qpc_ref[0].astype(BF16)], axis=1)
    gates = _sigmoid(gate_ref[0, 0])

    s_all = _dot(q_all, kct_ref[0, 0])
    rel_end = (lax.broadcasted_iota(jnp.int32, (1, ncb), 1) * CMP_STRIDE + (CMP_BLOCK - 1)) - start
    mask_c = lax.broadcasted_iota(jnp.int32, (nq, ncb), 0) >= rel_end
    rel_endf = rel_end.astype(F32)
    vc = vc_ref[0, 0]
    o_c, p_sum = [], jnp.zeros((nq, ncb), F32)
    for j in range(J):
        p = _masked_softmax_rows(s_all[rows[j]] + slopes[j] * rel_endf, mask_c)
        p_sum = p_sum + p
        o_c.append(_dot(p.astype(BF16), vc))

    ovl = ovl_ref[...]
    imp_t = sum(_dot_nt(ovl, part) for part in _split3(p_sum))
    qpos_row = start + lax.broadcasted_iota(jnp.int32, (ns, nq), 1)
    sel_t = _select_blocks_t(imp_t, qpos_row, v_ref, start // SLC_BLOCK + nq // SLC_BLOCK)
    not_q = ((sel_t.T - 1.0) * BIG).astype(BF16)
    q_aug = jnp.concatenate([q_pos, jnp.concatenate([not_q] * J, axis=0)], axis=1)

    tk = SEL_TILE
    m_ref[...] = jnp.full(m_ref.shape, NEG, F32)
    acc_ref[...] = jnp.zeros(acc_ref.shape, F32)
    causal = (lax.broadcasted_iota(jnp.int32, (nq, tk), 0) - lax.broadcasted_iota(jnp.int32, (nq, tk), 1))

    def sel_tile(k0, on_diagonal):
        s = _dot(q_aug, kaug_ref[:, pl.ds(k0, tk)])
        vt = va_ref[0, 0, pl.ds(k0, tk), :]
        shift = (k0 - start).astype(F32)
        for j in range(J):
            sj = s[rows[j]]
            if on_diagonal:
                sj = jnp.where(causal + (start - k0) >= 0, sj, -BIG)
            c = slopes[j] * shift
            m_old = m_ref[rows[j], :]
            m_new = jnp.maximum(m_old, jnp.max(sj, axis=-1, keepdims=True) + c)
            p = jnp.exp(sj - (m_new - c))
            acc_ref[rows[j], :] = acc_ref[rows[j], :] * jnp.exp(m_old - m_new) + _dot(p.astype(BF16), vt)
            m_ref[rows[j], :] = m_new

    n_full = start // tk
    bpt = tk // SLC_BLOCK
    v_ref[...] = sel_t

    def full_tile(kt, carry):
        chosen_here = jnp.max(v_ref[pl.ds(pl.multiple_of(kt * bpt, bpt), bpt), :])

        @pl.when(chosen_here > 0.5)
        def _():
            sel_tile(pl.multiple_of(kt * tk, tk), False)

        return carry

    lax.fori_loop(0, n_full, full_tile, 0)
    sel_tile(pl.multiple_of(n_full * tk, tk), True)

    s_w = _dot(q_pos, kwaug_ref[:, pl.ds(pl.multiple_of(start, nq), WIN_KEYS)])
    vw = vwp_ref[0, 0, pl.ds(pl.multiple_of(start, nq), WIN_KEYS), :]
    qrow = lax.broadcasted_iota(jnp.int32, (nq, WIN_KEYS), 0)
    wcol = lax.broadcasted_iota(jnp.int32, (nq, WIN_KEYS), 1)
    mask_w = (wcol > qrow) & (wcol - WINDOW <= qrow)
    chunk = lax.broadcasted_iota(jnp.int32, (1, WIN_KEYS), 1) // nq
    p0 = start - WINDOW + chunk * nq
    origin = (jnp.maximum(p0, 0) // POS_PERIOD) * POS_PERIOD - start
    for j in range(J):
        bias = jnp.where(p0 >= 0, slopes[j] * origin.astype(F32), -BIG)
        sm = jnp.where(mask_w, s_w[rows[j]] + bias, NEG)
        p = jnp.exp(sm - jnp.max(sm, axis=-1, keepdims=True))
        o_w = _dot(p.astype(BF16), vw) / jnp.sum(p, axis=-1, keepdims=True)
        acc = acc_ref[rows[j], :]
        o_s = acc[:, 0:dh] / jnp.maximum(acc[:, dh:dh + 1], 1e-30)
        out = (gates[:, 3 * j:3 * j + 1] * o_c[j] + gates[:, 3 * j + 1:3 * j + 2] * o_s
               + gates[:, 3 * j + 2:3 * j + 3] * o_w)
        o_ref[0, :, j * dh:(j + 1) * dh] = out.astype(o_ref.dtype)


def _slope_columns():
    s = jnp.asarray(_alibi_slopes())
    pieces = jnp.stack([p.astype(F32) for p in _split3(s)], axis=1)
    cols = jnp.concatenate([pieces * SLC_BLOCK, pieces,
                            jnp.zeros((NSA_HEADS, NSA_HEAD_DIM - 6), F32)], axis=1)
    return jnp.repeat(cols, Q_BLOCK, axis=0).reshape(NSA_KV_HEADS, NSA_HPG * Q_BLOCK, NSA_HEAD_DIM)


def _position_rows(L):
    t = np.arange(L) % POS_PERIOD
    rows = np.zeros((NSA_HEAD_DIM, L), np.float32)
    rows[0:3] = t // SLC_BLOCK
    rows[3:6] = t % SLC_BLOCK
    return jnp.asarray(rows, BF16)


def _block_rows(ns, L):
    return jnp.asarray((np.arange(ns)[:, None] == np.arange(L)[None, :] // SLC_BLOCK).astype(np.float32), BF16)


def _nsa_prompt(q, gates_raw, kct, vc, kst, va, kwt, vwp):
    b, L, _ = q.shape
    G, dh, J = NSA_KV_HEADS, NSA_HEAD_DIM, NSA_HPG
    ncb, ns = L // CMP_STRIDE, L // SLC_BLOCK
    nqb = L // Q_BLOCK
    per_bg = lambda shape: pl.BlockSpec((1, 1) + shape, lambda i, g, t, s: (i, g, 0, 0))
    const = lambda shape: pl.BlockSpec(shape, lambda i, g, t, s: (0,) * len(shape))
    grid_spec = pltpu.PrefetchScalarGridSpec(
        num_scalar_prefetch=1,
        grid=(b, G, nqb),
        in_specs=[pl.BlockSpec((1, Q_BLOCK, J * dh), lambda i, g, t, s: (i, t, g)),
                  pl.BlockSpec((1, 1, Q_BLOCK, 3 * J), lambda i, g, t, s: (i, g, t, 0)),
                  pl.BlockSpec((1, J * Q_BLOCK, dh), lambda i, g, t, s: (g, 0, 0)),
                  per_bg((dh, ncb)), per_bg((ncb, dh)), per_bg((dh, L)), per_bg((L, LANES)),
                  per_bg((dh, L)), per_bg((WINDOW + L, dh)),
                  const((dh, L)), const((ns, L)), const((ns, ncb))],
        out_specs=pl.BlockSpec((1, Q_BLOCK, J * dh), lambda i, g, t, s: (i, t, g)),
        scratch_shapes=[pltpu.VMEM((ns, Q_BLOCK), F32),
                        pltpu.VMEM((LANES + ns, L), BF16), pltpu.VMEM((LANES, WINDOW + L), BF16),
                        pltpu.VMEM((J * Q_BLOCK, 1), F32), pltpu.VMEM((J * Q_BLOCK, LANES), F32)],
    )
    return pl.pallas_call(
        _nsa_prompt_kernel,
        grid_spec=grid_spec,
        out_shape=jax.ShapeDtypeStruct((b, L, G * J * dh), BF16),
        compiler_params=_params("arbitrary", "arbitrary", "arbitrary"),
        name="nsa_prompt",
    )(jnp.asarray(_alibi_slopes()), q, gates_raw, _slope_columns(), kct, vc, kst, va, kwt, vwp,
      _position_rows(L), _block_rows(ns, L), jnp.asarray(_overlap_t(ncb, ns), BF16))


def _nsa_prompt_branch(q, kv, small, cw):
    b, L, _ = q.shape
    G, dh = NSA_KV_HEADS, NSA_HEAD_DIM
    cmp = _compress_prompt(kv, cw).astype(BF16)
    kct = cmp[:, 0].transpose(0, 1, 3, 2)
    vc = cmp[:, 1]
    kv5 = kv.astype(BF16).reshape(b, L, 6, G, dh)
    rows = lambda w: kv5[:, :, w].transpose(0, 2, 1, 3)
    cols = lambda w: kv5[:, :, w].transpose(0, 2, 3, 1)
    va = jnp.concatenate([rows(3), jnp.ones((b, G, L, 1), BF16), jnp.zeros((b, G, L, LANES - dh - 1), BF16)], axis=-1)
    vwp = jnp.pad(rows(5), ((0, 0), (0, 0), (WINDOW, 0), (0, 0)))
    gates_raw = small[:, :, SSD_HEADS:SSD_HEADS + 3 * NSA_HEADS].reshape(b, L, G, 3 * NSA_HPG).transpose(0, 2, 1, 3)
    return _nsa_prompt(q, gates_raw, kct, vc, cols(2), va, cols(4), vwp)


def _layer_norm(v, g, b):
    mu = jnp.mean(v, axis=-1, keepdims=True)
    c = v - mu
    var = jnp.mean(c * c, axis=-1, keepdims=True)
    return c * lax.rsqrt(var + NORM_EPS) * g + b


def _route_t(logits_t, bias_col, v_ref):
    e_n, n = logits_t.shape
    gsz = e_n // N_ROUTE_GROUPS
    scores = _sigmoid(logits_t)
    biased = scores + bias_col
    grp = []
    for a in range(N_ROUTE_GROUPS):
        blk = biased[a * gsz:(a + 1) * gsz]
        m1 = jnp.max(blk, axis=0, keepdims=True)
        cnt = jnp.sum(jnp.where(blk == m1, 1.0, 0.0), axis=0, keepdims=True)
        m2 = jnp.max(jnp.where(blk < m1, blk, -jnp.inf), axis=0, keepdims=True)
        grp.append(m1 + jnp.where(cnt >= 2.0, m1, m2))
    pieces = []
    for a in range(N_ROUTE_GROUPS):
        rank = jnp.zeros((1, n), jnp.int32)
        for b in range(N_ROUTE_GROUPS):
            if b != a:
                ahead = (grp[b] > grp[a]) | ((grp[b] == grp[a]) & (b < a))
                rank = rank + jnp.where(ahead, 1, 0)
        pieces.append(jnp.where(rank < TOPK_ROUTE_GROUPS, biased[a * gsz:(a + 1) * gsz], -jnp.inf))
    masked = jnp.concatenate(pieces, axis=0)
    v_ref[...] = masked
    e_i = lax.broadcasted_iota(jnp.int32, (e_n, n), 0)

    def body(ep, rank):
        r = v_ref[pl.ds(ep, 1), :]
        ahead = (r > masked) | ((r == masked) & (ep < e_i))
        return rank + jnp.where(ahead, 1, 0)

    rank = lax.fori_loop(0, e_n, body, jnp.zeros((e_n, n), jnp.int32))
    picked = jnp.where(rank < TOP_K, scores, 0.0)
    return picked / jnp.sum(picked, axis=0, keepdims=True) * ROUTE_SCALE


def _finish_kernel(x_ref, sy_ref, ny_ref, gm_ref, wsd_ref, wnd_ref, wo_ref, bo_ref, g1_ref, b1_ref,
                   wrh_ref, wrl_ref, br_ref, h_ref, hb_ref, gt_ref, slot_ref, cnt_ref, v_ref):
    gate = _sigmoid(gm_ref[...])
    mixed = (gate[:, :D_MODEL] * _dot(sy_ref[...], wsd_ref[...])
             + gate[:, D_MODEL:] * _dot(ny_ref[...], wnd_ref[...]))
    o = _dot(mixed.astype(BF16), wo_ref[...]) + bo_ref[...]
    h = _layer_norm(ALPHA * x_ref[...] + o, g1_ref[...], b1_ref[...])
    h_ref[...] = h
    hb_ref[...] = h.astype(BF16)
    h_hi, h_lo = _split2(h)
    logits_t = _dot_nt(wrh_ref[...], h_hi) + _dot_nt(wrh_ref[...], h_lo) + _dot_nt(wrl_ref[...], h_hi)
    gates_t = _route_t(logits_t, br_ref[...], v_ref)
    gt_ref[...] = gates_t
    tm = gates_t.shape[1]
    routed = gates_t > 0.0
    earlier = jnp.where(lax.broadcasted_iota(jnp.int32, (tm, tm), 0) < lax.broadcasted_iota(jnp.int32, (tm, tm), 1),
                        1.0, 0.0).astype(BF16)
    ones = jnp.where(routed, 1.0, 0.0)
    slot_ref[...] = jnp.where(routed, _dot(ones.astype(BF16), earlier) + 1.0, 0.0)
    cnt_ref[0] = jnp.broadcast_to(jnp.sum(ones, axis=1, keepdims=True), (N_EXPERTS, LANES))


def _finish(x2d, ssd_y, nsa_y, gm, fw, tm):
    t = x2d.shape[0]
    wsd, wnd, wo, bo, g1, b1, wrh, wrl, br = fw
    row = lambda w: pl.BlockSpec((tm, w), lambda i: (i, 0))
    full = lambda a: pl.BlockSpec(a.shape, lambda i: (0,) * a.ndim)
    col = pl.BlockSpec((N_EXPERTS, tm), lambda i: (0, i))
    return pl.pallas_call(
        _finish_kernel,
        grid=(t // tm,),
        in_specs=[row(D_MODEL), row(SSD_D_INNER), row(D_MODEL), row(2 * D_MODEL)] + [full(a) for a in fw],
        out_specs=[row(D_MODEL), row(D_MODEL), col, col, pl.BlockSpec((1, N_EXPERTS, LANES), lambda i: (i, 0, 0))],
        out_shape=[jax.ShapeDtypeStruct((t, D_MODEL), F32), jax.ShapeDtypeStruct((t, D_MODEL), BF16),
                   jax.ShapeDtypeStruct((N_EXPERTS, t), F32), jax.ShapeDtypeStruct((N_EXPERTS, t), F32),
                   jax.ShapeDtypeStruct((t // tm, N_EXPERTS, LANES), F32)],
        scratch_shapes=[pltpu.VMEM((N_EXPERTS, tm), F32)],
        compiler_params=_params("arbitrary"),
        name="finish",
    )(x2d, ssd_y, nsa_y, gm, *fw)


def _finish_weights(w_ssd_down, w_nsa_down, w_out, b_out, ln1_g, ln1_b, w_router, b_router):
    wr_t = w_router.astype(F32).T
    wrh = wr_t.astype(BF16)
    wrl = (wr_t - wrh.astype(F32)).astype(BF16)
    return (w_ssd_down.astype(BF16), w_nsa_down.astype(BF16), w_out.astype(BF16), b_out.astype(F32)[None, :],
            ln1_g.astype(F32)[None, :], ln1_b.astype(F32)[None, :], wrh, wrl, b_router.astype(F32)[:, None])


MOE_R = 64
MOE_ALIGN = 16
MOE_BM = 256
MOE_TAIL = 2 * MOE_BM
MOE_GROUP = 8


def _swiglu(x, w1, w3):
    return _silu(_dot(x, w1)) * _dot(x, w3)


def _moe_blocks(t, tt):
    rows = (t * TOP_K + (t // tt) * N_EXPERTS * (MOE_ALIGN - 1) + N_EXPERTS * (MOE_R + MOE_BM - 1))
    return pl.cdiv(rows, MOE_BM) + MOE_TAIL // MOE_BM


def _moe_plan(cnt, nb):
    aligned = (cnt + MOE_ALIGN - 1) // MOE_ALIGN * MOE_ALIGN
    before = jnp.cumsum(aligned, axis=0) - aligned
    total = jnp.sum(aligned, axis=0)
    region = (total + MOE_R + MOE_BM - 1) // MOE_BM * MOE_BM
    region_end = jnp.cumsum(region)
    start = (region_end - region)[None, :] + before
    n_pass = jnp.maximum(jnp.max((cnt + MOE_R - 1) // MOE_R, axis=1), 1)
    blk_exp = jnp.minimum(jnp.searchsorted(region_end, jnp.arange(nb, dtype=jnp.int32) * MOE_BM, side='right'),
                          N_EXPERTS - 1)
    i32 = lambda a: a.astype(jnp.int32)
    return (i32(start.reshape(-1)), i32(cnt.reshape(-1)), i32(n_pass), i32(blk_exp),
            i32(region_end[-1:] // MOE_BM))


def _expert_of_row():
    e = np.arange(N_EXPERTS * MOE_R)[:, None] // MOE_R
    return jnp.asarray((e == np.arange(N_EXPERTS)[None, :]).astype(np.float32), BF16)


def _for_runs(cnt_ref, tile, k, fn):
    def body(e, c):
        @pl.when(cnt_ref[tile * N_EXPERTS + e] > k * MOE_R)
        def _():
            fn(e)
        return c
    lax.fori_loop(0, N_EXPERTS, body, 0)


def _moe_dispatch_kernel(start_ref, cnt_ref, npass_ref, hb_ref, slot_ref, pick_ref, xs_in_ref, xs_ref,
                         buf_ref, sem_ref):
    del xs_in_ref
    i = pl.program_id(0)
    E, R, GR = N_EXPERTS, MOE_R, MOE_GROUP * MOE_R
    half = i % 2
    h = hb_ref[...]
    slot_of_row = _dot(pick_ref[...], slot_ref[...].astype(BF16))
    want = (lax.broadcasted_iota(jnp.int32, (GR, 1), 0) % R + 1).astype(F32)

    def fill(k):
        for g in range(E * R // GR):
            hit = slot_of_row[g * GR:(g + 1) * GR] == want + k * R
            buf_ref[half, g * GR:(g + 1) * GR, :] = _dot(jnp.where(hit, 1.0, 0.0).astype(BF16), h).astype(BF16)

    def copy(tile, e, k, hf):
        dst = pl.multiple_of(start_ref[tile * E + e] + k * R, MOE_ALIGN)
        return pltpu.make_async_copy(buf_ref.at[hf, pl.ds(pl.multiple_of(e * R, R), R), :],
                                     xs_ref.at[pl.ds(dst, R), :], sem_ref.at[hf])

    fill(0)

    @pl.when(i > 0)
    def _():
        k_prev = npass_ref[i - 1] - 1
        _for_runs(cnt_ref, i - 1, k_prev, lambda e: copy(i - 1, e, k_prev, 1 - half).wait())

    _for_runs(cnt_ref, i, 0, lambda e: copy(i, e, 0, half).start())

    def more(k, c):
        _for_runs(cnt_ref, i, k - 1, lambda e: copy(i, e, k - 1, half).wait())
        fill(k)
        _for_runs(cnt_ref, i, k, lambda e: copy(i, e, k, half).start())
        return c

    lax.fori_loop(1, npass_ref[i], more, 0)

    @pl.when(i == pl.num_programs(0) - 1)
    def _():
        k_last = npass_ref[i] - 1
        _for_runs(cnt_ref, i, k_last, lambda e: copy(i, e, k_last, half).wait())


def _moe_dispatch(hb, slots, plan, nb, tt):
    t = hb.shape[0]
    start, cnt, n_pass, _, _ = plan
    rows = nb * MOE_BM
    pick = _expert_of_row()
    grid_spec = pltpu.PrefetchScalarGridSpec(
        num_scalar_prefetch=3,
        grid=(t // tt,),
        in_specs=[pl.BlockSpec((tt, D_MODEL), lambda i, *_: (i, 0)),
                  pl.BlockSpec((N_EXPERTS, tt), lambda i, *_: (0, i)),
                  pl.BlockSpec(pick.shape, lambda i, *_: (0, 0)),
                  pl.BlockSpec(memory_space=pl.ANY)],
        out_specs=pl.BlockSpec(memory_space=pl.ANY),
        scratch_shapes=[pltpu.VMEM((2, N_EXPERTS * MOE_R, D_MODEL), BF16), pltpu.SemaphoreType.DMA((2,))],
    )
    return pl.pallas_call(
        _moe_dispatch_kernel,
        grid_spec=grid_spec,
        out_shape=jax.ShapeDtypeStruct((rows, D_MODEL), BF16),
        input_output_aliases={6: 0},
        compiler_params=_params("arbitrary"),
        name="moe_dispatch",
    )(start, cnt, n_pass, hb, slots, pick, jnp.zeros((rows, D_MODEL), BF16))


def _moe_ffn_kernel(blk_exp_ref, nused_ref, x_ref, w1_ref, w3_ref, w2_ref, y_ref):
    del blk_exp_ref
    in_use = pl.program_id(0) < nused_ref[0]

    @pl.when(in_use)
    def _():
        a = _swiglu(x_ref[...], w1_ref[0], w3_ref[0])
        y_ref[...] = _dot(a.astype(BF16), w2_ref[0]).astype(y_ref.dtype)

    @pl.when(jnp.logical_not(in_use))
    def _():
        y_ref[...] = jnp.zeros(y_ref.shape, y_ref.dtype)


def _moe_ffn(xs, plan, nb, w1, w3, w2):
    _, _, _, blk_exp, n_used = plan
    grid_spec = pltpu.PrefetchScalarGridSpec(
        num_scalar_prefetch=2,
        grid=(nb,),
        in_specs=[pl.BlockSpec((MOE_BM, D_MODEL), lambda j, be, nu: (jnp.minimum(j, nu[0] - 1), 0)),
                  pl.BlockSpec((1, D_MODEL, EXPERT_FF), lambda j, be, nu: (be[j], 0, 0)),
                  pl.BlockSpec((1, D_MODEL, EXPERT_FF), lambda j, be, nu: (be[j], 0, 0)),
                  pl.BlockSpec((1, EXPERT_FF, D_MODEL), lambda j, be, nu: (be[j], 0, 0))],
        out_specs=pl.BlockSpec((MOE_BM, D_MODEL), lambda j, be, nu: (j, 0)),
    )
    return pl.pallas_call(
        _moe_ffn_kernel,
        grid_spec=grid_spec,
        out_shape=jax.ShapeDtypeStruct(xs.shape, BF16),
        compiler_params=_params("arbitrary"),
        name="moe_ffn",
    )(blk_exp, n_used, xs, w1, w3, w2)


def _moe_combine_kernel(start_ref, npass_ref, hb_ref, h_ref, gate_ref, slot_ref, pick_ref, ys_ref,
                        ws1_ref, ws3_ref, ws2_ref, g2_ref, b2_ref, y_ref, buf_ref, sem_ref):
    i = pl.program_id(0)
    n = pl.num_programs(0)
    E, R, GR = N_EXPERTS, MOE_R, MOE_GROUP * MOE_R
    half = i % 2

    def copy(tile, e, k, hf):
        src = pl.multiple_of(start_ref[tile * E + e] + k * R, MOE_ALIGN)
        return pltpu.make_async_copy(ys_ref.at[pl.ds(src, R), :],
                                     buf_ref.at[hf, pl.ds(pl.multiple_of(e * R, R), R), :], sem_ref.at[hf])

    def all_runs(fn):
        def body(e, c):
            fn(e)
            return c
        lax.fori_loop(0, E, body, 0)

    @pl.when(i == 0)
    def _():
        all_runs(lambda e: copy(0, e, 0, 0).start())

    @pl.when(i + 1 < n)
    def _():
        all_runs(lambda e: copy(i + 1, e, 0, 1 - half).start())

    pick = pick_ref[...]
    slot_of_row = _dot(pick, slot_ref[...].astype(BF16))
    gate_of_row = sum(_dot(pick, part) for part in _split3(gate_ref[...]))
    want = (lax.broadcasted_iota(jnp.int32, (GR, 1), 0) % R + 1).astype(F32)

    def gathered(k):
        acc = jnp.zeros(y_ref.shape, F32)
        for g in range(E * R // GR):
            rows = slice(g * GR, (g + 1) * GR)
            hit = slot_of_row[rows] == want + k * R
            gate = jnp.sum(jnp.where(hit, gate_of_row[rows], 0.0), axis=1, keepdims=True)
            weighted = (buf_ref[half, rows, :].astype(F32) * gate).astype(BF16)
            acc = acc + _dot_tn(jnp.where(hit, 1.0, 0.0).astype(BF16), weighted)
        return acc

    all_runs(lambda e: copy(i, e, 0, half).wait())
    routed = gathered(0)

    def more(k, acc):
        all_runs(lambda e: copy(i, e, k, half).start())
        all_runs(lambda e: copy(i, e, k, half).wait())
        return acc + gathered(k)

    routed = lax.fori_loop(1, npass_ref[i], more, routed)
    x = hb_ref[...]
    shared = _dot(_swiglu(x, ws1_ref[...], ws3_ref[...]).astype(BF16), ws2_ref[...])
    y_ref[...] = _layer_norm(ALPHA * h_ref[...] + (routed + shared), g2_ref[...], b2_ref[...])


def _moe_combine(hb, h, gates_t, slots, ys, plan, mw, tt):
    t = h.shape[0]
    start, _, n_pass, _, _ = plan
    _, _, _, ws1, ws3, ws2, g2, b2 = mw
    pick = _expert_of_row()
    consts = (pick,)
    weights = (ws1, ws3, ws2, g2, b2)
    row = lambda w: pl.BlockSpec((tt, w), lambda i, *_: (i, 0))
    col = pl.BlockSpec((N_EXPERTS, tt), lambda i, *_: (0, i))
    full = lambda a: pl.BlockSpec(a.shape, lambda i, *_: (0,) * a.ndim)
    grid_spec = pltpu.PrefetchScalarGridSpec(
        num_scalar_prefetch=2,
        grid=(t // tt,),
        in_specs=[row(D_MODEL), row(D_MODEL), col, col] + [full(a) for a in consts]
        + [pl.BlockSpec(memory_space=pl.ANY)] + [full(a) for a in weights],
        out_specs=row(D_MODEL),
        scratch_shapes=[pltpu.VMEM((2, N_EXPERTS * MOE_R, D_MODEL), BF16), pltpu.SemaphoreType.DMA((2,))],
    )
    return pl.pallas_call(
        _moe_combine_kernel,
        grid_spec=grid_spec,
        out_shape=jax.ShapeDtypeStruct((t, D_MODEL), F32),
        compiler_params=_params("arbitrary"),
        name="moe_combine",
    )(start, n_pass, hb, h, gates_t, slots, *consts, ys, *weights)


def _moe_weights(w_e1, w_e3, w_e2, w_s1, w_s3, w_s2, ln2_g, ln2_b):
    return (w_e1.astype(BF16), w_e3.astype(BF16), w_e2.astype(BF16), w_s1.astype(BF16), w_s3.astype(BF16),
            w_s2.astype(BF16), ln2_g.astype(F32)[None, :], ln2_b.astype(F32)[None, :])


def _finish_and_moe(x2d, ssd_y, nsa_y, gm, fw, mw, tt):
    t = x2d.shape[0]
    h, hb, gates_t, slots, run_len = _finish(x2d, ssd_y, nsa_y, gm, fw, tt)
    nb = _moe_blocks(t, tt)
    plan = _moe_plan(run_len[:, :, 0].astype(jnp.int32), nb)
    xs = _moe_dispatch(hb, slots, plan, nb, tt)
    ys = _moe_ffn(xs, plan, nb, mw[0], mw[1], mw[2])
    return _moe_combine(hb, h, gates_t, slots, ys, plan, mw, tt)


def _ssd_sample_kernel(xbc_ref, sc_ref, z_ref, sm_ref, st_ref, cw_ref, cb_ref, dtb_ref, alog_ref, dskip_ref,
                       nw_ref, e_ref, y_ref, sto_ref, xdt_t_ref, dec_t_ref, b_ref, c_ref, xs_ref, yt_ref):
    i = pl.program_id(0)
    db = xbc_ref.shape[0]
    G, N = SSD_GROUPS, SSD_STATE
    GW = SSD_HPG * SSD_HEAD_DIM

    @pl.when(i == 0)
    def _():
        conv = cb_ref[...] + xbc_ref[...] * cw_ref[SSD_CONV - 1:SSD_CONV, :]
        for k in range(SSD_CONV - 1):
            conv = conv + sc_ref[k] * cw_ref[k:k + 1, :]
        act = _silu(conv)
        xs = act[:, :SSD_D_INNER]
        xs_ref[...] = xs
        b_ref[...] = act[:, SSD_D_INNER:SSD_D_INNER + G * N].astype(BF16)
        c_ref[...] = act[:, SSD_D_INNER + G * N:].astype(BF16)
        dt = _softplus(sm_ref[...] + dtb_ref[...])
        dec = jnp.exp(dt * (-jnp.exp(alog_ref[...])))
        e_mat = e_ref[...]
        xdt_t_ref[...] = (xs * _dot_exact_rhs(dt, e_mat)).T.astype(BF16)
        dec_t_ref[...] = _dot_exact_rhs(dec, e_mat).T
        yt_ref[...] = jnp.zeros_like(yt_ref)

    is_row = lax.broadcasted_iota(jnp.int32, (db, N), 0) == i
    onehot = jnp.where(is_row, 1.0, 0.0).astype(BF16)
    is_lane = lax.broadcasted_iota(jnp.int32, (GW, db), 1) == i
    for g in range(G):
        rows = slice(g * GW, (g + 1) * GW)
        b_g = jnp.where(is_row, b_ref[:, g * N:(g + 1) * N], jnp.zeros((), BF16))
        contrib = _dot(xdt_t_ref[rows, :], b_g)
        decay = _dot_exact_rhs(dec_t_ref[rows, :], onehot)
        new = st_ref[0, rows, :] * decay + contrib
        sto_ref[0, rows, :] = new
        y_all = _dot_nt(new.astype(BF16), c_ref[:, g * N:(g + 1) * N])
        yt_ref[rows, :] += jnp.where(is_lane, y_all, 0.0)

    @pl.when(i == pl.num_programs(0) - 1)
    def _():
        y = yt_ref[...].T + dskip_ref[...] * xs_ref[...]
        y = y * _silu(z_ref[...])
        for g in range(G):
            sl = slice(g * GW, (g + 1) * GW)
            yg = y[:, sl]
            rs = lax.rsqrt(jnp.mean(yg * yg, axis=-1, keepdims=True) + NORM_EPS)
            y_ref[:, sl] = (yg * rs * nw_ref[:, sl]).astype(y_ref.dtype)


def _ssd_sample(xbc, state_conv, z, small, state_ssm, conv_w, conv_b, dt_bias, a_log, d_skip, norm_w):
    db = xbc.shape[0]
    pad = LANES - SSD_HEADS
    dtb = jnp.pad(dt_bias.astype(F32), (0, pad))[None, :]
    alog = jnp.pad(a_log.astype(F32), (0, pad))[None, :]
    dskip = jnp.repeat(d_skip.astype(F32), SSD_HEAD_DIM)[None, :]
    sc = state_conv.astype(F32).transpose(1, 0, 2)
    st = state_ssm.reshape(db, SSD_D_INNER, SSD_STATE)
    args = (xbc, sc, z, small, st, conv_w.astype(F32), conv_b.astype(F32)[None, :], dtb, alog, dskip,
            norm_w.astype(F32)[None, :], _head_expand_matrix())
    full = lambda a: pl.BlockSpec(a.shape, lambda i: (0,) * a.ndim)
    st_spec = pl.BlockSpec((1, SSD_D_INNER, SSD_STATE), lambda i: (i, 0, 0))
    y, st_new = pl.pallas_call(
        _ssd_sample_kernel,
        grid=(db,),
        in_specs=[full(a) for a in args[:4]] + [st_spec] + [full(a) for a in args[5:]],
        out_specs=[pl.BlockSpec((db, SSD_D_INNER), lambda i: (0, 0)), st_spec],
        out_shape=[jax.ShapeDtypeStruct((db, SSD_D_INNER), BF16),
                   jax.ShapeDtypeStruct((db, SSD_D_INNER, SSD_STATE), F32)],
        scratch_shapes=[pltpu.VMEM((SSD_D_INNER, db), BF16), pltpu.VMEM((SSD_D_INNER, db), F32),
                        pltpu.VMEM((db, SSD_GROUPS * SSD_STATE), BF16),
                        pltpu.VMEM((db, SSD_GROUPS * SSD_STATE), BF16),
                        pltpu.VMEM((db, SSD_D_INNER), F32), pltpu.VMEM((SSD_D_INNER, db), F32)],
        compiler_params=_params("arbitrary"),
        name="ssd_sample",
    )(*args)
    return y, st_new.reshape(state_ssm.shape)


def _nsa_sample_kernel(n_pages, *refs):
    pt_ref = refs[0]
    page_refs = refs[1:1 + n_pages]
    (win_ref, q_ref, kvn_ref, gate_ref, slope_ref, ovl_ref, w1ab_ref, w1flat_ref, pe_ref, b1_ref, w2_ref,
     b2_ref, o_ref, wino_ref, cmp_rows_ref, ks_ref, vs_ref, kw_ref, vw_ref, bsh_ref, kc_ref, vc_ref) = refs[1 + n_pages:]
    del pt_ref
    H, dh, G = NSA_HEADS, NSA_HEAD_DIM, NSA_KV_HEADS
    past = n_pages * PAGE_SIZE
    ncb = past // CMP_STRIDE
    nsb = ovl_ref.shape[1]
    wbuf = win_ref.shape[3]
    tiles = KVW // LANES

    for p in range(n_pages):
        r = slice(p * PAGE_SIZE, (p + 1) * PAGE_SIZE)
        for w in range(2):
            rows_wp = page_refs[p][0, w].T
            for t in range(tiles):
                cmp_rows_ref[w * tiles + t, r, :] = rows_wp[:, t * LANES:(t + 1) * LANES]
        ks_ref[:, r] = page_refs[p][0, 2].astype(BF16)
        vs_ref[:, r] = page_refs[p][0, 3].astype(BF16)
    new8 = jnp.broadcast_to(kvn_ref[0], (8, 6 * KVW))
    first = lax.broadcasted_iota(jnp.int32, (KVW, LANES), 1) == 0

    def new_col(w):
        return new8[:, w * KVW:(w + 1) * KVW].T[:, 0:1]

    put = lambda w: jnp.where(first, new_col(w), 0.0).astype(BF16)
    ks_ref[:, past:past + LANES] = put(2)
    vs_ref[:, past:past + LANES] = put(3)
    kw_ref[:, 0:wbuf] = win_ref[0, 0].astype(BF16)
    vw_ref[:, 0:wbuf] = win_ref[0, 1].astype(BF16)
    kw_ref[:, wbuf:wbuf + LANES] = put(4)
    vw_ref[:, wbuf:wbuf + LANES] = put(5)
    last = lax.broadcasted_iota(jnp.int32, (KVW, wbuf), 1) == wbuf - 1
    for w in range(2):
        wino_ref[0, w] = jnp.where(last, new_col(4 + w), pltpu.roll(win_ref[0, w], wbuf - 1, 1))

    for w, dst in ((0, kc_ref), (1, vc_ref)):
        for t in range(tiles):
            def load_strip(l, w=w, t=t):
                return cmp_rows_ref[w * tiles + t, pl.ds(l, ncb, stride=CMP_STRIDE), :]

            def store(g, val, t=t, dst=dst):
                c0 = (t * GPT + g) * dh
                dst[:, c0:c0 + dh] = val.astype(BF16)

            sub = lambda ref: ref.at[pl.ds(w, 1)]
            _compress_body(load_strip, ncb, sub(w1ab_ref), sub(w1flat_ref), sub(pe_ref), sub(b1_ref), sub(w2_ref),
                           sub(b2_ref), bsh_ref, store)

    head_r = lax.broadcasted_iota(jnp.int32, (H, KVW), 0) // NSA_HPG
    lane_g = lax.broadcasted_iota(jnp.int32, (H, KVW), 1) // dh
    diag = head_r == lane_g
    q16 = q_ref[0] * (dh ** -0.5)
    q_bd = jnp.where(diag, jnp.concatenate([q16] * G, axis=1), 0.0).astype(BF16)
    slopes = slope_ref[...]
    gates = _sigmoid(gate_ref[0])

    def fold(o):
        o = jnp.where(diag, o, 0.0)
        return sum(o[:, g * dh:(g + 1) * dh] for g in range(G))

    d_c = past - (lax.broadcasted_iota(jnp.int32, (H, ncb), 1) * CMP_STRIDE + (CMP_BLOCK - 1))
    p_c = _masked_softmax_rows(_dot_nt(q_bd, kc_ref[...]) - slopes * d_c.astype(F32), d_c >= 0)
    o_c = fold(_dot(p_c.astype(BF16), vc_ref[...]))

    p_grp = jnp.concatenate(
        [jnp.sum(p_c[g * NSA_HPG:(g + 1) * NSA_HPG], axis=0, keepdims=True) for g in range(G)]
        + [jnp.zeros((8 - G, ncb), F32)], axis=0)
    imp = sum(_dot(part, ovl_ref[...]) for part in _split3(p_grp))
    m_l = lax.broadcasted_iota(jnp.int32, (8, nsb), 1)
    valid = m_l * SLC_BLOCK <= past
    cur = past // SLC_BLOCK
    forced = (m_l == 0) | (m_l == cur) | (m_l == cur - 1)
    v = jnp.where(valid, jnp.where(forced, jnp.inf, imp), -jnp.inf)
    v_t = v.T
    mp_i = lax.broadcasted_iota(jnp.int32, (nsb, nsb), 0)
    m_i = lax.broadcasted_iota(jnp.int32, (nsb, nsb), 1)
    sel_rows = []
    for g in range(G):
        v_col = jnp.broadcast_to(v_t[:, g:g + 1], (nsb, nsb))
        v_row = jnp.broadcast_to(v[g:g + 1, :], (nsb, nsb))
        ahead = (v_col > v_row) | ((v_col == v_row) & (mp_i < m_i))
        rank = jnp.sum(jnp.where(ahead, 1.0, 0.0), axis=0, keepdims=True)
        sel = jnp.where((rank < SLC_TOPK) & valid[0:1, :], 1.0, 0.0)
        sel_rows.append(jnp.broadcast_to(sel, (NSA_HPG, nsb)))
    sel_h = jnp.concatenate(sel_rows, axis=0).astype(BF16)

    nk = past + LANES
    expand = jnp.where(lax.broadcasted_iota(jnp.int32, (nsb, nk), 0)
                       == lax.broadcasted_iota(jnp.int32, (nsb, nk), 1) // SLC_BLOCK, 1.0, 0.0).astype(BF16)
    d_s = past - lax.broadcasted_iota(jnp.int32, (H, nk), 1)
    ok = (_dot(sel_h, expand) > 0.5) & (d_s >= 0)
    p_s = _masked_softmax_rows(_dot(q_bd, ks_ref[...]) - slopes * d_s.astype(F32), ok)
    o_s = fold(_dot_nt(p_s.astype(BF16), vs_ref[...]))

    nw = wbuf + LANES
    d_w = wbuf - lax.broadcasted_iota(jnp.int32, (H, nw), 1)
    p_w = _masked_softmax_rows(_dot(q_bd, kw_ref[...]) - slopes * d_w.astype(F32), (d_w >= 0) & (d_w < WINDOW))
    o_w = fold(_dot_nt(p_w.astype(BF16), vw_ref[...]))

    o_ref[0] = (gates[:, 0:1] * o_c + gates[:, 1:2] * o_s + gates[:, 2:3] * o_w).astype(o_ref.dtype)


def _overlap_sample(ncb, nsb):
    i = np.arange(ncb)[:, None] * CMP_STRIDE
    m = np.arange(nsb)[None, :] * SLC_BLOCK
    ok = (i < m + SLC_BLOCK) & (i + CMP_BLOCK > m) & (np.arange(ncb)[:, None] < ncb - 1)
    return ok.astype(np.float32)


def _nsa_sample(q, kv_new, small, cache_kv_paged, page_table, cache_kv_win, cw):
    db = q.shape[0]
    H, dh, G = NSA_HEADS, NSA_HEAD_DIM, NSA_KV_HEADS
    n_pages = page_table.shape[1]
    past = n_pages * PAGE_SIZE
    wbuf = cache_kv_win.shape[1]
    ncb = past // CMP_STRIDE
    nsb = LANES * pl.cdiv(pl.cdiv(past + 1, SLC_BLOCK), LANES)
    pages = cache_kv_paged.transpose(0, 2, 3, 4, 1).reshape(cache_kv_paged.shape[0], 4, KVW, PAGE_SIZE)
    win = cache_kv_win.transpose(0, 2, 3, 4, 1).reshape(db, 2, KVW, wbuf)
    gates_raw = small[:, SSD_HEADS:SSD_HEADS + 3 * H].reshape(db, H, 3)
    slopes = jnp.asarray(_alibi_slopes())[:, None]
    ovl = jnp.asarray(_overlap_sample(ncb, nsb), BF16)
    consts = (slopes, ovl) + tuple(cw)
    per_seq = lambda shape: pl.BlockSpec((1,) + shape, lambda i, pt: (i,) + (0,) * len(shape))
    full = lambda a: pl.BlockSpec(a.shape, lambda i, pt: (0,) * a.ndim)
    page_spec = lambda p: pl.BlockSpec((1, 4, KVW, PAGE_SIZE), lambda i, pt: (pt[i, p], 0, 0, 0))
    grid_spec = pltpu.PrefetchScalarGridSpec(
        num_scalar_prefetch=1,
        grid=(db,),
        in_specs=[page_spec(p) for p in range(n_pages)]
        + [per_seq((2, KVW, wbuf)), per_seq((H, dh)), per_seq((1, 6 * KVW)), per_seq((H, 3))]
        + [full(a) for a in consts],
        out_specs=[per_seq((H, dh)), per_seq((2, KVW, wbuf))],
        scratch_shapes=[pltpu.VMEM((2 * KVW // LANES, past, LANES), F32),
                        pltpu.VMEM((KVW, past + LANES), BF16), pltpu.VMEM((KVW, past + LANES), BF16),
                        pltpu.VMEM((KVW, wbuf + LANES), BF16), pltpu.VMEM((KVW, wbuf + LANES), BF16),
                        pltpu.VMEM((ncb + 8, CMP_HIDDEN), F32),
                        pltpu.VMEM((ncb, KVW), BF16), pltpu.VMEM((ncb, KVW), BF16)],
    )
    out, win_new = pl.pallas_call(
        functools.partial(_nsa_sample_kernel, n_pages),
        grid_spec=grid_spec,
        out_shape=[jax.ShapeDtypeStruct((db, H, dh), BF16), jax.ShapeDtypeStruct((db, 2, KVW, wbuf), cache_kv_win.dtype)],
        compiler_params=_params("arbitrary"),
        name="nsa_sample",
    )(page_table.astype(jnp.int32), *([pages] * n_pages), win, q.reshape(db, H, dh), kv_new.reshape(db, 1, 6 * KVW),
      gates_raw, *consts)
    win_new = win_new.reshape(db, 2, G, dh, wbuf).transpose(0, 4, 1, 2, 3)
    return out.reshape(db, H * dh), win_new


def kernel(x_prompt, x_sample, cache_kv_paged, cache_kv_win, state_ssm, state_conv, page_table, w_in, b_in, conv_w, conv_b, dt_bias, a_log, d_skip, ssd_norm_w, cmp_w1, cmp_b1, cmp_w2, cmp_b2, cmp_pe, w_ssd_down, w_nsa_down, w_out, b_out, ln1_g, ln1_b, w_router, b_router, w_e1, w_e3, w_e2, w_s1, w_s3, w_s2, ln2_g, ln2_b):
    b, L, _ = x_prompt.shape
    db = x_sample.shape[0]
    G, dh = NSA_KV_HEADS, NSA_HEAD_DIM
    wbuf = cache_kv_win.shape[1]
    wp, bp = _pack_w_in(w_in, b_in)
    cw = _compress_weights(cmp_w1, cmp_b1, cmp_w2, cmp_b2, cmp_pe)
    fw = _finish_weights(w_ssd_down, w_nsa_down, w_out, b_out, ln1_g, ln1_b, w_router, b_router)
    mw = _moe_weights(w_e1, w_e3, w_e2, w_s1, w_s3, w_s2, ln2_g, ln2_b)
    ssd_w = (conv_w, conv_b, dt_bias, a_log, d_skip, ssd_norm_w)
    tm = min(256, b * L)
    per_seq = lambda a: a.reshape(b, L, a.shape[-1])

    xp = x_prompt.reshape(b * L, D_MODEL)
    z, xbc, q, kv, gm, small = _in_proj(xp, wp, bp, tm)
    ssd_y, st_t = _ssd_prompt(per_seq(xbc), per_seq(z), per_seq(small), *ssd_w)
    nsa_y = _nsa_prompt_branch(per_seq(q), per_seq(kv), per_seq(small), cw)
    yp = _finish_and_moe(xp, ssd_y.reshape(b * L, -1), nsa_y.reshape(b * L, -1), gm, fw, mw, tm).reshape(b, L, D_MODEL)
    kv6 = kv.reshape(b, L, 6, G, dh)
    kv_rows_p = kv6[:, :, :4]
    win_p = kv6[:, L - wbuf:, 4:]
    ssm_p = st_t.reshape(b, SSD_STATE, SSD_HEADS, SSD_HEAD_DIM).transpose(0, 2, 3, 1).astype(state_ssm.dtype)
    conv_p = per_seq(xbc)[:, L - (SSD_CONV - 1):]

    xs = x_sample.reshape(db, D_MODEL)
    z, xbc, q, kv, gm, small = _in_proj(xs, wp, bp, db)
    ssd_y, ssm_s = _ssd_sample(xbc, state_conv, z, small, state_ssm, *ssd_w)
    nsa_y, win_s = _nsa_sample(q, kv, small, cache_kv_paged, page_table, cache_kv_win, cw)
    ys = _finish_and_moe(xs, ssd_y, nsa_y, gm, fw, mw, db).reshape(db, 1, D_MODEL)
    kv6 = kv.reshape(db, 1, 6, G, dh)
    kv_rows_s = kv6[:, :, :4]
    conv_s = jnp.concatenate([state_conv[:, 1:].astype(xbc.dtype), xbc[:, None, :]], axis=1)
    return (yp, ys, kv_rows_p, kv_rows_s, win_p, win_s, ssm_p, ssm_s, conv_p, conv_s)
```

```python
import functools

import numpy as np
import jax
import jax.numpy as jnp
from jax import lax
from jax.experimental import pallas as pl
from jax.experimental.pallas import tpu as pltpu

F32 = jnp.float32
BF16 = jnp.bfloat16

D_MODEL = 1024
PAGE_SIZE = 128
SSD_D_INNER = 2048
SSD_HEAD_DIM = 64
SSD_HEADS = 32
SSD_GROUPS = 4
SSD_HPG = 8
SSD_STATE = 128
SSD_CONV = 4
SSD_CHUNK = 128
SSD_CONV_CH = SSD_D_INNER + 2 * SSD_GROUPS * SSD_STATE
NSA_HEAD_DIM = 64
NSA_HEADS = 16
NSA_KV_HEADS = 4
NSA_HPG = 4
CMP_BLOCK = 32
CMP_STRIDE = 16
CMP_HIDDEN = 128
SLC_BLOCK = 64
SLC_TOPK = 16
WINDOW = 512
Q_BLOCK = 128
N_EXPERTS = 64
TOP_K = 8
N_ROUTE_GROUPS = 8
TOPK_ROUTE_GROUPS = 4
EXPERT_FF = 256
ROUTE_SCALE = 2.5
ALPHA = 2.0 ** 0.25
NORM_EPS = 1e-5
IN_SPLITS = (SSD_D_INNER, SSD_CONV_CH, SSD_HEADS, NSA_HEADS * NSA_HEAD_DIM,
             6 * NSA_KV_HEADS * NSA_HEAD_DIM, 3 * NSA_HEADS, 2 * D_MODEL)
IN_OFFSETS = tuple(int(v) for v in np.cumsum(IN_SPLITS)[:-1])

LANES = 128
VMEM_LIMIT = 56 * 1024 * 1024

SEG_Z = (0, 2048)
SEG_XBC = (2048, 3072)
SEG_Q = (5120, 1024)
SEG_KV = (6144, 1536)
SEG_GM = (7680, 2048)
SEG_SMALL = (9728, 128)
N_PACKED = 9856


def _params(*sem):
    return pltpu.CompilerParams(dimension_semantics=sem, vmem_limit_bytes=VMEM_LIMIT)


def _silu(x):
    return x * (1.0 / (1.0 + jnp.exp(-x)))


def _sigmoid(x):
    return 1.0 / (1.0 + jnp.exp(-x))


def _softplus(x):
    return jnp.maximum(x, 0.0) + jnp.log(1.0 + jnp.exp(-jnp.abs(x)))


def _split2(x):
    hi = x.astype(BF16)
    lo = (x - hi.astype(F32)).astype(BF16)
    return hi, lo


def _split3(x):
    hi = x.astype(BF16)
    r = x - hi.astype(F32)
    mid = r.astype(BF16)
    lo = (r - mid.astype(F32)).astype(BF16)
    return hi, mid, lo


def _dot(a, b):
    return jnp.dot(a, b, preferred_element_type=F32)


def _dot_nt(a, b):
    return lax.dot_general(a, b, (((1,), (1,)), ((), ())), preferred_element_type=F32)


def _dot_tn(a, b):
    return lax.dot_general(a, b, (((0,), (0,)), ((), ())), preferred_element_type=F32)


def _dot_exact_rhs(x, sel):
    a, b, c = _split3(x)
    return _dot(a, sel) + _dot(b, sel) + _dot(c, sel)


def _pack_w_in(w_in, b_in):
    def pack(m):
        z, xbc, dt, q, kv, gn, gm = jnp.split(m, list(IN_OFFSETS), axis=-1)
        pad = jnp.zeros(m.shape[:-1] + (LANES - SSD_HEADS - 3 * NSA_HEADS,), m.dtype)
        return jnp.concatenate([z, xbc, q, kv, gm, dt, gn, pad], axis=-1)
    return pack(w_in).astype(BF16), pack(b_in[None, :])


def _in_proj_kernel(x_ref, w_ref, b_ref, z_ref, xbc_ref, q_ref, kv_ref, gm_ref, sm_ref):
    x = x_ref[...].astype(BF16)
    for ref, (off, width) in ((z_ref, SEG_Z), (xbc_ref, SEG_XBC), (q_ref, SEG_Q), (kv_ref, SEG_KV),
                              (gm_ref, SEG_GM), (sm_ref, SEG_SMALL)):
        ref[...] = _dot(x, w_ref[:, off:off + width]) + b_ref[:, off:off + width]


def _in_proj(x2d, w_packed, b_packed, tm):
    t = x2d.shape[0]
    segs = (SEG_Z, SEG_XBC, SEG_Q, SEG_KV, SEG_GM, SEG_SMALL)
    return pl.pallas_call(
        _in_proj_kernel,
        grid=(t // tm,),
        in_specs=[pl.BlockSpec((tm, D_MODEL), lambda i: (i, 0)),
                  pl.BlockSpec((D_MODEL, N_PACKED), lambda i: (0, 0), pipeline_mode=pl.Buffered(1)),
                  pl.BlockSpec((1, N_PACKED), lambda i: (0, 0))],
        out_specs=[pl.BlockSpec((tm, w), lambda i: (i, 0)) for _, w in segs],
        out_shape=[jax.ShapeDtypeStruct((t, w), F32) for _, w in segs],
        compiler_params=_params("arbitrary"),
        name="in_proj",
    )(x2d, w_packed, b_packed)


def _kv_channel_major_kernel(x_ref, wt_ref, bt_ref, o_ref):
    o_ref[0] = _dot_nt(wt_ref[...], x_ref[0].astype(BF16)) + bt_ref[...]


def _kv_channel_major(x, w_in, b_in, tm):
    b, L, _ = x.shape
    lo, n = IN_OFFSETS[3], IN_SPLITS[4]
    wt = w_in[:, lo:lo + n].T.astype(BF16)
    bt = b_in[lo:lo + n].astype(F32)[:, None]
    return pl.pallas_call(
        _kv_channel_major_kernel,
        grid=(b, L // tm),
        in_specs=[pl.BlockSpec((1, tm, D_MODEL), lambda i, j: (i, j, 0)),
                  pl.BlockSpec((n, D_MODEL), lambda i, j: (0, 0)),
                  pl.BlockSpec((n, 1), lambda i, j: (0, 0))],
        out_specs=pl.BlockSpec((1, n, tm), lambda i, j: (i, 0, j)),
        out_shape=jax.ShapeDtypeStruct((b, n, L), F32),
        compiler_params=_params("arbitrary", "arbitrary"),
        name="kv_channel_major",
    )(x, wt, bt)


def _head_expand_matrix():
    h = np.arange(LANES)[:, None]
    c = np.arange(SSD_D_INNER)[None, :] // SSD_HEAD_DIM
    return jnp.asarray((h == c).astype(np.float32), BF16)


def _ssd_prompt_kernel(xbc_ref, z_ref, sm_ref, cw_ref, cb_ref, dtb_ref, alog_ref, dskip_ref, nw_ref, e_ref,
                       y_ref, st_ref, xh_ref, state_ref, ybuf_ref):
    c = pl.program_id(1)
    L = SSD_CHUNK
    G, N, P = SSD_GROUPS, SSD_STATE, SSD_HEAD_DIM
    GW = SSD_HPG * P

    @pl.when(c == 0)
    def _():
        xh_ref[0:8, :] = jnp.zeros((8, SSD_CONV_CH), F32)
        state_ref[...] = jnp.zeros_like(state_ref)

    xh_ref[8:8 + L, :] = xbc_ref[0]
    conv = cb_ref[...] + xh_ref[5:5 + L, :] * cw_ref[0:1, :]
    for k in range(1, SSD_CONV):
        conv = conv + xh_ref[5 + k:5 + k + L, :] * cw_ref[k:k + 1, :]
    xh_ref[0:8, :] = xh_ref[L:L + 8, :]
    act = _silu(conv)
    xs = act[:, :SSD_D_INNER]
    bm = act[:, SSD_D_INNER:SSD_D_INNER + G * N].astype(BF16)
    cm = act[:, SSD_D_INNER + G * N:]

    dt = _softplus(sm_ref[0] + dtb_ref[...])
    da = dt * (-jnp.exp(alog_ref[...]))
    row_i = lax.broadcasted_iota(jnp.int32, (L, L), 0)
    col_i = lax.broadcasted_iota(jnp.int32, (L, L), 1)
    tri = row_i >= col_i
    a_cs = _dot_exact_rhs_left(tri, da)
    a_cs_t = a_cs.T
    dt_t = dt.T
    a_last = a_cs[L - 1:L, :]
    w_end = dt * jnp.exp(a_last - a_cs)
    chunk_decay = jnp.broadcast_to(jnp.exp(a_last), (8, LANES))
    e_mat = e_ref[...]
    w_exp = _dot_exact_rhs(w_end, e_mat)
    dec_exp = _dot_exact_rhs(chunk_decay, e_mat)[0:1, :]
    xd = (xs * w_exp).astype(BF16)

    for g in range(G):
        cg = cm[:, g * N:(g + 1) * N]
        bg = bm[:, g * N:(g + 1) * N]
        cb = _dot_nt(cg.astype(BF16), bg)
        for j in range(SSD_HPG):
            h = g * SSD_HPG + j
            col = jnp.broadcast_to(a_cs[:, h:h + 1], (L, L))
            row = a_cs_t[h:h + 1, :]
            lm = jnp.where(tri, jnp.exp(col - row), 0.0)
            m = (cb * lm * dt_t[h:h + 1, :]).astype(BF16)
            eac = (jnp.exp(col) * cg).astype(BF16)
            lhs = jnp.concatenate([m, eac], axis=1)
            rhs = jnp.concatenate([xs[:, h * P:(h + 1) * P].astype(BF16),
                                   state_ref[:, h * P:(h + 1) * P].astype(BF16)], axis=0)
            ybuf_ref[:, h * P:(h + 1) * P] = _dot(lhs, rhs)
        sl = slice(g * GW, (g + 1) * GW)
        state_ref[:, sl] = state_ref[:, sl] * dec_exp[:, sl] + _dot_tn(bg, xd[:, sl])

    y = ybuf_ref[...] + dskip_ref[...] * xs
    y = y * _silu(z_ref[0])
    for g in range(G):
        sl = slice(g * GW, (g + 1) * GW)
        yg = y[:, sl]
        rs = lax.rsqrt(jnp.mean(yg * yg, axis=-1, keepdims=True) + NORM_EPS)
        y_ref[0, :, sl] = (yg * rs * nw_ref[:, sl]).astype(y_ref.dtype)

    @pl.when(c == pl.num_programs(1) - 1)
    def _():
        st_ref[0] = state_ref[...]


def _dot_exact_rhs_left(mask, x):
    sel = jnp.where(mask, 1.0, 0.0).astype(BF16)
    a, b, c = _split3(x)
    return _dot(sel, a) + _dot(sel, b) + _dot(sel, c)


def _ssd_prompt(xbc, z, small, conv_w, conv_b, dt_bias, a_log, d_skip, norm_w):
    b, L, _ = xbc.shape
    nc = L // SSD_CHUNK
    pad = LANES - SSD_HEADS
    dtb = jnp.pad(dt_bias.astype(F32), (0, pad))[None, :]
    alog = jnp.pad(a_log.astype(F32), (0, pad))[None, :]
    dskip = jnp.repeat(d_skip.astype(F32), SSD_HEAD_DIM)[None, :]
    full = lambda shape: pl.BlockSpec(shape, lambda i, j: (0,) * len(shape))
    y, st = pl.pallas_call(
        _ssd_prompt_kernel,
        grid=(b, nc),
        in_specs=[pl.BlockSpec((1, SSD_CHUNK, SSD_CONV_CH), lambda i, j: (i, j, 0)),
                  pl.BlockSpec((1, SSD_CHUNK, SSD_D_INNER), lambda i, j: (i, j, 0)),
                  pl.BlockSpec((1, SSD_CHUNK, LANES), lambda i, j: (i, j, 0)),
                  full((SSD_CONV, SSD_CONV_CH)), full((1, SSD_CONV_CH)), full((1, LANES)), full((1, LANES)),
                  full((1, SSD_D_INNER)), full((1, SSD_D_INNER)), full((LANES, SSD_D_INNER))],
        out_specs=[pl.BlockSpec((1, SSD_CHUNK, SSD_D_INNER), lambda i, j: (i, j, 0)),
                   pl.BlockSpec((1, SSD_STATE, SSD_D_INNER), lambda i, j: (i, 0, 0))],
        out_shape=[jax.ShapeDtypeStruct((b, L, SSD_D_INNER), BF16),
                   jax.ShapeDtypeStruct((b, SSD_STATE, SSD_D_INNER), F32)],
        scratch_shapes=[pltpu.VMEM((SSD_CHUNK + 8, SSD_CONV_CH), F32),
                        pltpu.VMEM((SSD_STATE, SSD_D_INNER), F32),
                        pltpu.VMEM((SSD_CHUNK, SSD_D_INNER), F32)],
        compiler_params=_params("arbitrary", "arbitrary"),
        name="ssd_prompt",
    )(xbc, z, small, conv_w.astype(F32), conv_b.astype(F32)[None, :], dtb, alog, dskip,
      norm_w.astype(F32)[None, :], _head_expand_matrix())
    return y, st


HALF = CMP_BLOCK // 2
KVW = NSA_KV_HEADS * NSA_HEAD_DIM
ROWS_PER_DOT = 2


def _compress_weights(cmp_w1, cmp_b1, cmp_w2, cmp_b2, cmp_pe):
    w1 = cmp_w1.astype(BF16)
    w1ab = jnp.concatenate([w1[:, :HALF], w1[:, HALF:]], axis=-1)
    zero = jnp.zeros_like(w1ab)
    per_head = [jnp.stack([w1ab if h == g else zero for h in range(GPT)], axis=2) for g in range(GPT)]
    w1ab = jnp.stack(per_head, axis=1).reshape(2, GPT, HALF // ROWS_PER_DOT, ROWS_PER_DOT * LANES, 2 * CMP_HIDDEN)
    w1flat = w1.reshape(2, CMP_BLOCK * NSA_HEAD_DIM, CMP_HIDDEN)
    pe8 = jnp.broadcast_to(cmp_pe.reshape(2, 1, CMP_BLOCK * NSA_HEAD_DIM), (2, 8, CMP_BLOCK * NSA_HEAD_DIM))
    return (w1ab, w1flat, pe8.astype(F32), cmp_b1.astype(F32)[:, None, :], cmp_w2.astype(BF16),
            cmp_b2.astype(F32)[:, None, :])


GPT = LANES // NSA_HEAD_DIM


def _compress_body(load_strip, ns, w1ab_ref, w1flat_ref, pe_ref, b1_ref, w2_ref, b2_ref, bsh_ref, store):
    accs = [jnp.zeros((ns, 2 * CMP_HIDDEN), F32) for _ in range(GPT)]
    for c in range(HALF // ROWS_PER_DOT):
        lhs = jnp.concatenate([load_strip(c * ROWS_PER_DOT + i).astype(BF16) for i in range(ROWS_PER_DOT)], axis=1)
        for g in range(GPT):
            accs[g] = accs[g] + _dot(lhs, w1ab_ref[0, g, c])
    cpe = _dot(pe_ref[0].astype(BF16), w1flat_ref[0])[0:1, :] + b1_ref[0]
    bsh_ref[ns:ns + 8, :] = jnp.zeros((8, CMP_HIDDEN), F32)
    for g in range(GPT):
        bsh_ref[0:ns, :] = accs[g][:, CMP_HIDDEN:]
        hid = _silu(accs[g][:, :CMP_HIDDEN] + bsh_ref[1:ns + 1, :] + cpe)
        store(g, _dot(hid.astype(BF16), w2_ref[0]) + b2_ref[0])


def _compress_prompt_kernel(kv_ref, w1ab_ref, w1flat_ref, pe_ref, b1_ref, w2_ref, b2_ref, out_ref, bsh_ref):
    ns = out_ref.shape[3]

    def load_strip(l):
        return kv_ref[0, pl.ds(l, ns, stride=CMP_STRIDE), :]

    def store(g, val):
        out_ref[0, 0, g] = val

    _compress_body(load_strip, ns, w1ab_ref, w1flat_ref, pe_ref, b1_ref, w2_ref, b2_ref, bsh_ref, store)


def _compress_prompt(kv, cw):
    b, L, _ = kv.shape
    ns = L // CMP_STRIDE
    w1ab, w1flat, pe8, b1, w2, b2 = cw
    tiles = KVW // LANES
    per_which = lambda shape: pl.BlockSpec((1,) + shape, lambda i, w, t: (w,) + (0,) * len(shape))
    return pl.pallas_call(
        _compress_prompt_kernel,
        grid=(b, 2, tiles),
        in_specs=[pl.BlockSpec((1, L, LANES), lambda i, w, t: (i, 0, w * tiles + t)),
                  per_which((GPT, HALF // ROWS_PER_DOT, ROWS_PER_DOT * LANES, 2 * CMP_HIDDEN)),
                  per_which((CMP_BLOCK * NSA_HEAD_DIM, CMP_HIDDEN)),
                  per_which((8, CMP_BLOCK * NSA_HEAD_DIM)),
                  per_which((1, CMP_HIDDEN)),
                  per_which((CMP_HIDDEN, NSA_HEAD_DIM)),
                  per_which((1, NSA_HEAD_DIM))],
        out_specs=pl.BlockSpec((1, 1, GPT, ns, NSA_HEAD_DIM), lambda i, w, t: (i, w, t, 0, 0)),
        out_shape=jax.ShapeDtypeStruct((b, 2, NSA_KV_HEADS, ns, NSA_HEAD_DIM), F32),
        scratch_shapes=[pltpu.VMEM((ns + 8, CMP_HIDDEN), F32)],
        compiler_params=_params("arbitrary", "arbitrary", "arbitrary"),
        name="compress_prompt",
    )(kv, w1ab, w1flat, pe8, b1, w2, b2)


NEG = -1e30
SEL_TILE = 512
WIN_KEYS = WINDOW + Q_BLOCK


def _alibi_slopes():
    h = np.arange(1, NSA_HEADS + 1, dtype=np.float32)
    return (2.0 ** (-8.0 * h / NSA_HEADS)).astype(np.float32)


def _overlap_t(ncb, ns):
    i = np.arange(ncb)[None, :] * CMP_STRIDE
    m = np.arange(ns)[:, None] * SLC_BLOCK
    return ((i < m + SLC_BLOCK) & (i + CMP_BLOCK > m)).astype(np.float32)


def _masked_softmax_rows(s, mask):
    sm = jnp.where(mask, s, NEG)
    mx = jnp.max(sm, axis=-1, keepdims=True)
    p = jnp.where(mask, jnp.exp(sm - mx), 0.0)
    return p / jnp.maximum(jnp.sum(p, axis=-1, keepdims=True), 1e-30)


def _select_blocks_t(imp_t, qpos_row, v_ref, n_live):
    ns, nq = imp_t.shape
    m_i = lax.broadcasted_iota(jnp.int32, (ns, nq), 0)
    valid = m_i * SLC_BLOCK <= qpos_row
    cur = qpos_row // SLC_BLOCK
    forced = (m_i == 0) | (m_i == cur) | (m_i == cur - 1)
    v = jnp.where(forced, jnp.inf, imp_t)
    v = jnp.where(valid, v, -jnp.inf)
    v_ref[...] = v

    def body(mp, rank):
        r = v_ref[pl.ds(mp, 1), :]
        tie = jnp.where(mp < m_i, 1.0, 0.0)
        return rank + jnp.where(r > v, 1.0, jnp.where(r == v, tie, 0.0))

    rank = lax.fori_loop(0, n_live, body, jnp.zeros((ns, nq), F32))
    return jnp.where(valid, jnp.where(rank < SLC_TOPK, 1.0, 0.0), 0.0)


BIG = 2.0 ** 20
POS_PERIOD = SEL_TILE


def _nsa_prompt_kernel(slopes_ref, q_ref, gate_ref, qpc_ref, kct_ref, vc_ref, kst_ref, va_ref, kwt_ref, vwp_ref,
                       pos_ref, blk_ref, ovl_ref, o_ref, v_ref, kaug_ref, kwaug_ref, m_ref, acc_ref):
    g = pl.program_id(1)
    qi = pl.program_id(2)
    nq, dh, J = Q_BLOCK, NSA_HEAD_DIM, NSA_HPG
    ns, ncb = ovl_ref.shape
    L = kst_ref.shape[2]
    start = qi * nq
    slopes = [slopes_ref[g * J + j] for j in range(J)]
    rows = [slice(j * nq, (j + 1) * nq) for j in range(J)]

    @pl.when(qi == 0)
    def _():
        kaug_ref[0:dh, :] = kst_ref[0].astype(BF16)
        kaug_ref[dh:LANES, :] = pos_ref[...]
        kaug_ref[LANES:, :] = blk_ref[...]
        kwaug_ref[0:dh, 0:WINDOW] = jnp.zeros((dh, WINDOW), BF16)
        kwaug_ref[dh:LANES, 0:WINDOW] = pos_ref[:, 0:WINDOW]
        kwaug_ref[0:dh, WINDOW:] = kwt_ref[0].astype(BF16)
        kwaug_ref[dh:LANES, WINDOW:] = pos_ref[...]

    qb = q_ref[0] * (dh ** -0.5)
    q_all = jnp.concatenate([qb[:, j * dh:(j + 1) * dh].astype(BF16) for j in range(J)], axis=0)
    q_pos = jnp.concatenate([q_all, qpc_ref[0].astype(BF16)], axis=1)
    gates = _sigmoid(gate_ref[0, 0])

    s_all = _dot(q_all, kct_ref[0, 0])
    rel_end = (lax.broadcasted_iota(jnp.int32, (1, ncb), 1) * CMP_STRIDE + (CMP_BLOCK - 1)) - start
    mask_c = lax.broadcasted_iota(jnp.int32, (nq, ncb), 0) >= rel_end
    rel_endf = rel_end.astype(F32)
    vc = vc_ref[0, 0]
    o_c, p_sum = [], jnp.zeros((nq, ncb), F32)
    for j in range(J):
        p = _masked_softmax_rows(s_all[rows[j]] + slopes[j] * rel_endf, mask_c)
        p_sum = p_sum + p
        o_c.append(_dot(p.astype(BF16), vc))

    ovl = ovl_ref[...]
    imp_t = sum(_dot_nt(ovl, part) for part in _split3(p_sum))
    qpos_row = start + lax.broadcasted_iota(jnp.int32, (ns, nq), 1)
    sel_t = _select_blocks_t(imp_t, qpos_row, v_ref, start // SLC_BLOCK + nq // SLC_BLOCK)
    not_q = ((sel_t.T - 1.0) * BIG).astype(BF16)
    q_aug = jnp.concatenate([q_pos, jnp.concatenate([not_q] * J, axis=0)], axis=1)

    tk = SEL_TILE
    m_ref[...] = jnp.full(m_ref.shape, NEG, F32)
    acc_ref[...] = jnp.zeros(acc_ref.shape, F32)
    causal = (lax.broadcasted_iota(jnp.int32, (nq, tk), 0) - lax.broadcasted_iota(jnp.int32, (nq, tk), 1))

    def sel_tile(k0, on_diagonal):
        s = _dot(q_aug, kaug_ref[:, pl.ds(k0, tk)])
        vt = va_ref[0, 0, pl.ds(k0, tk), :]
        shift = (k0 - start).astype(F32)
        for j in range(J):
            sj = s[rows[j]]
            if on_diagonal:
                sj = jnp.where(causal + (start - k0) >= 0, sj, -BIG)
            c = slopes[j] * shift
            m_old = m_ref[rows[j], :]
            m_new = jnp.maximum(m_old, jnp.max(sj, axis=-1, keepdims=True) + c)
            p = jnp.exp(sj - (m_new - c))
            acc_ref[rows[j], :] = acc_ref[rows[j], :] * jnp.exp(m_old - m_new) + _dot(p.astype(BF16), vt)
            m_ref[rows[j], :] = m_new

    n_full = start // tk
    bpt = tk // SLC_BLOCK
    v_ref[...] = sel_t

    def full_tile(kt, carry):
        chosen_here = jnp.max(v_ref[pl.ds(pl.multiple_of(kt * bpt, bpt), bpt), :])

        @pl.when(chosen_here > 0.5)
        def _():
            sel_tile(pl.multiple_of(kt * tk, tk), False)

        return carry

    lax.fori_loop(0, n_full, full_tile, 0)
    sel_tile(pl.multiple_of(n_full * tk, tk), True)

    s_w = _dot(q_pos, kwaug_ref[:, pl.ds(pl.multiple_of(start, nq), WIN_KEYS)])
    vw = vwp_ref[0, 0, pl.ds(pl.multiple_of(start, nq), WIN_KEYS), :]
    qrow = lax.broadcasted_iota(jnp.int32, (nq, WIN_KEYS), 0)
    wcol = lax.broadcasted_iota(jnp.int32, (nq, WIN_KEYS), 1)
    mask_w = (wcol > qrow) & (wcol - WINDOW <= qrow)
    chunk = lax.broadcasted_iota(jnp.int32, (1, WIN_KEYS), 1) // nq
    p0 = start - WINDOW + chunk * nq
    origin = (jnp.maximum(p0, 0) // POS_PERIOD) * POS_PERIOD - start
    for j in range(J):
        bias = jnp.where(p0 >= 0, slopes[j] * origin.astype(F32), -BIG)
        sm = jnp.where(mask_w, s_w[rows[j]] + bias, NEG)
        p = jnp.exp(sm - jnp.max(sm, axis=-1, keepdims=True))
        o_w = _dot(p.astype(BF16), vw) / jnp.sum(p, axis=-1, keepdims=True)
        acc = acc_ref[rows[j], :]
        o_s = acc[:, 0:dh] / jnp.maximum(acc[:, dh:dh + 1], 1e-30)
        out = (gates[:, 3 * j:3 * j + 1] * o_c[j] + gates[:, 3 * j + 1:3 * j + 2] * o_s
               + gates[:, 3 * j + 2:3 * j + 3] * o_w)
        o_ref[0, :, j * dh:(j + 1) * dh] = out.astype(o_ref.dtype)


def _slope_columns():
    s = jnp.asarray(_alibi_slopes())
    pieces = jnp.stack([p.astype(F32) for p in _split3(s)], axis=1)
    cols = jnp.concatenate([pieces * SLC_BLOCK, pieces,
                            jnp.zeros((NSA_HEADS, NSA_HEAD_DIM - 6), F32)], axis=1)
    return jnp.repeat(cols, Q_BLOCK, axis=0).reshape(NSA_KV_HEADS, NSA_HPG * Q_BLOCK, NSA_HEAD_DIM)


def _position_rows(L):
    t = np.arange(L) % POS_PERIOD
    rows = np.zeros((NSA_HEAD_DIM, L), np.float32)
    rows[0:3] = t // SLC_BLOCK
    rows[3:6] = t % SLC_BLOCK
    return jnp.asarray(rows, BF16)


def _block_rows(ns, L):
    return jnp.asarray((np.arange(ns)[:, None] == np.arange(L)[None, :] // SLC_BLOCK).astype(np.float32), BF16)


def _nsa_prompt(q, gates_raw, kct, vc, kv_t, va, vwp):
    b, L, _ = q.shape
    G, dh, J = NSA_KV_HEADS, NSA_HEAD_DIM, NSA_HPG
    ncb, ns = L // CMP_STRIDE, L // SLC_BLOCK
    nqb = L // Q_BLOCK
    per_bg = lambda shape: pl.BlockSpec((1, 1) + shape, lambda i, g, t, s: (i, g, 0, 0))
    const = lambda shape: pl.BlockSpec(shape, lambda i, g, t, s: (0,) * len(shape))
    kv_rows = lambda w: pl.BlockSpec((1, dh, L), lambda i, g, t, s: (i, w * G + g, 0))
    grid_spec = pltpu.PrefetchScalarGridSpec(
        num_scalar_prefetch=1,
        grid=(b, G, nqb),
        in_specs=[pl.BlockSpec((1, Q_BLOCK, J * dh), lambda i, g, t, s: (i, t, g)),
                  pl.BlockSpec((1, 1, Q_BLOCK, 3 * J), lambda i, g, t, s: (i, g, t, 0)),
                  pl.BlockSpec((1, J * Q_BLOCK, dh), lambda i, g, t, s: (g, 0, 0)),
                  per_bg((dh, ncb)), per_bg((ncb, dh)), kv_rows(2), per_bg((L, LANES)),
                  kv_rows(4), per_bg((WINDOW + L, dh)),
                  const((dh, L)), const((ns, L)), const((ns, ncb))],
        out_specs=pl.BlockSpec((1, Q_BLOCK, J * dh), lambda i, g, t, s: (i, t, g)),
        scratch_shapes=[pltpu.VMEM((ns, Q_BLOCK), F32),
                        pltpu.VMEM((LANES + ns, L), BF16), pltpu.VMEM((LANES, WINDOW + L), BF16),
                        pltpu.VMEM((J * Q_BLOCK, 1), F32), pltpu.VMEM((J * Q_BLOCK, LANES), F32)],
    )
    return pl.pallas_call(
        _nsa_prompt_kernel,
        grid_spec=grid_spec,
        out_shape=jax.ShapeDtypeStruct((b, L, G * J * dh), BF16),
        compiler_params=_params("arbitrary", "arbitrary", "arbitrary"),
        name="nsa_prompt",
    )(jnp.asarray(_alibi_slopes()), q, gates_raw, _slope_columns(), kct, vc, kv_t, va, kv_t, vwp,
      _position_rows(L), _block_rows(ns, L), jnp.asarray(_overlap_t(ncb, ns), BF16))


def _nsa_prompt_branch(q, kv, kv_t, small, cw):
    b, L, _ = q.shape
    G, dh = NSA_KV_HEADS, NSA_HEAD_DIM
    cmp = _compress_prompt(kv, cw).astype(BF16)
    kct = cmp[:, 0].transpose(0, 1, 3, 2)
    vc = cmp[:, 1]
    rows = lambda w: kv[:, :, w * KVW:(w + 1) * KVW].astype(BF16).reshape(b, L, G, dh).transpose(0, 2, 1, 3)
    va = jnp.concatenate([rows(3), jnp.ones((b, G, L, 1), BF16), jnp.zeros((b, G, L, LANES - dh - 1), BF16)], axis=-1)
    vwp = jnp.pad(rows(5), ((0, 0), (0, 0), (WINDOW, 0), (0, 0)))
    gates_raw = small[:, :, SSD_HEADS:SSD_HEADS + 3 * NSA_HEADS].reshape(b, L, G, 3 * NSA_HPG).transpose(0, 2, 1, 3)
    return _nsa_prompt(q, gates_raw, kct, vc, kv_t, va, vwp)


def _layer_norm(v, g, b):
    mu = jnp.mean(v, axis=-1, keepdims=True)
    c = v - mu
    var = jnp.mean(c * c, axis=-1, keepdims=True)
    return c * lax.rsqrt(var + NORM_EPS) * g + b


def _route_t(logits_t, bias_col, v_ref):
    e_n, n = logits_t.shape
    gsz = e_n // N_ROUTE_GROUPS
    scores = _sigmoid(logits_t)
    biased = scores + bias_col
    grp = []
    for a in range(N_ROUTE_GROUPS):
        blk = biased[a * gsz:(a + 1) * gsz]
        m1 = jnp.max(blk, axis=0, keepdims=True)
        cnt = jnp.sum(jnp.where(blk == m1, 1.0, 0.0), axis=0, keepdims=True)
        m2 = jnp.max(jnp.where(blk < m1, blk, -jnp.inf), axis=0, keepdims=True)
        grp.append(m1 + jnp.where(cnt >= 2.0, m1, m2))
    pieces = []
    for a in range(N_ROUTE_GROUPS):
        rank = jnp.zeros((1, n), jnp.int32)
        for b in range(N_ROUTE_GROUPS):
            if b != a:
                ahead = (grp[b] > grp[a]) | ((grp[b] == grp[a]) & (b < a))
                rank = rank + jnp.where(ahead, 1, 0)
        pieces.append(jnp.where(rank < TOPK_ROUTE_GROUPS, biased[a * gsz:(a + 1) * gsz], -jnp.inf))
    masked = jnp.concatenate(pieces, axis=0)
    v_ref[...] = masked
    e_i = lax.broadcasted_iota(jnp.int32, (e_n, n), 0)

    def body(ep, rank):
        r = v_ref[pl.ds(ep, 1), :]
        ahead = (r > masked) | ((r == masked) & (ep < e_i))
        return rank + jnp.where(ahead, 1, 0)

    rank = lax.fori_loop(0, e_n, body, jnp.zeros((e_n, n), jnp.int32))
    picked = jnp.where(rank < TOP_K, scores, 0.0)
    return picked / jnp.sum(picked, axis=0, keepdims=True) * ROUTE_SCALE


def _finish_kernel(x_ref, sy_ref, ny_ref, gm_ref, wsd_ref, wnd_ref, wo_ref, bo_ref, g1_ref, b1_ref,
                   wrh_ref, wrl_ref, br_ref, h_ref, hb_ref, gt_ref, slot_ref, cnt_ref, v_ref):
    gate = _sigmoid(gm_ref[...])
    mixed = (gate[:, :D_MODEL] * _dot(sy_ref[...], wsd_ref[...])
             + gate[:, D_MODEL:] * _dot(ny_ref[...], wnd_ref[...]))
    o = _dot(mixed.astype(BF16), wo_ref[...]) + bo_ref[...]
    h = _layer_norm(ALPHA * x_ref[...] + o, g1_ref[...], b1_ref[...])
    h_ref[...] = h
    hb_ref[...] = h.astype(BF16)
    h_hi, h_lo = _split2(h)
    logits_t = _dot_nt(wrh_ref[...], h_hi) + _dot_nt(wrh_ref[...], h_lo) + _dot_nt(wrl_ref[...], h_hi)
    gates_t = _route_t(logits_t, br_ref[...], v_ref)
    gt_ref[...] = gates_t
    tm = gates_t.shape[1]
    routed = gates_t > 0.0
    earlier = jnp.where(lax.broadcasted_iota(jnp.int32, (tm, tm), 0) < lax.broadcasted_iota(jnp.int32, (tm, tm), 1),
                        1.0, 0.0).astype(BF16)
    ones = jnp.where(routed, 1.0, 0.0)
    slot_ref[...] = jnp.where(routed, _dot(ones.astype(BF16), earlier) + 1.0, 0.0)
    cnt_ref[0] = jnp.broadcast_to(jnp.sum(ones, axis=1, keepdims=True), (N_EXPERTS, LANES))


def _finish(x2d, ssd_y, nsa_y, gm, fw, tm):
    t = x2d.shape[0]
    wsd, wnd, wo, bo, g1, b1, wrh, wrl, br = fw
    row = lambda w: pl.BlockSpec((tm, w), lambda i: (i, 0))
    full = lambda a: pl.BlockSpec(a.shape, lambda i: (0,) * a.ndim)
    col = pl.BlockSpec((N_EXPERTS, tm), lambda i: (0, i))
    return pl.pallas_call(
        _finish_kernel,
        grid=(t // tm,),
        in_specs=[row(D_MODEL), row(SSD_D_INNER), row(D_MODEL), row(2 * D_MODEL)] + [full(a) for a in fw],
        out_specs=[row(D_MODEL), row(D_MODEL), col, col, pl.BlockSpec((1, N_EXPERTS, LANES), lambda i: (i, 0, 0))],
        out_shape=[jax.ShapeDtypeStruct((t, D_MODEL), F32), jax.ShapeDtypeStruct((t, D_MODEL), BF16),
                   jax.ShapeDtypeStruct((N_EXPERTS, t), F32), jax.ShapeDtypeStruct((N_EXPERTS, t), F32),
                   jax.ShapeDtypeStruct((t // tm, N_EXPERTS, LANES), F32)],
        scratch_shapes=[pltpu.VMEM((N_EXPERTS, tm), F32)],
        compiler_params=_params("arbitrary"),
        name="finish",
    )(x2d, ssd_y, nsa_y, gm, *fw)


def _finish_weights(w_ssd_down, w_nsa_down, w_out, b_out, ln1_g, ln1_b, w_router, b_router):
    wr_t = w_router.astype(F32).T
    wrh = wr_t.astype(BF16)
    wrl = (wr_t - wrh.astype(F32)).astype(BF16)
    return (w_ssd_down.astype(BF16), w_nsa_down.astype(BF16), w_out.astype(BF16), b_out.astype(F32)[None, :],
            ln1_g.astype(F32)[None, :], ln1_b.astype(F32)[None, :], wrh, wrl, b_router.astype(F32)[:, None])


MOE_R = 64
MOE_ALIGN = 16
MOE_BM = 512
MOE_TAIL = MOE_BM
MOE_GROUP = 8


def _swiglu(x, w1, w3):
    return _silu(_dot(x, w1)) * _dot(x, w3)


def _moe_blocks(t, tt):
    rows = (t * TOP_K + (t // tt) * N_EXPERTS * (MOE_ALIGN - 1) + N_EXPERTS * (MOE_R + MOE_BM - 1))
    return pl.cdiv(rows, MOE_BM) + MOE_TAIL // MOE_BM


def _moe_plan(cnt, nb):
    aligned = (cnt + MOE_ALIGN - 1) // MOE_ALIGN * MOE_ALIGN
    before = jnp.cumsum(aligned, axis=0) - aligned
    total = jnp.sum(aligned, axis=0)
    region = (total + MOE_R + MOE_BM - 1) // MOE_BM * MOE_BM
    region_end = jnp.cumsum(region)
    start = (region_end - region)[None, :] + before
    n_pass = jnp.maximum(jnp.max((cnt + MOE_R - 1) // MOE_R, axis=1), 1)
    blk_row0 = jnp.arange(nb, dtype=region_end.dtype) * MOE_BM
    blk_exp = jnp.minimum(jnp.sum(region_end[None, :] <= blk_row0[:, None], axis=1), N_EXPERTS - 1)
    i32 = lambda a: a.astype(jnp.int32)
    return (i32(start.reshape(-1)), i32(cnt.reshape(-1)), i32(n_pass), i32(blk_exp),
            i32(region_end[-1:] // MOE_BM))


def _expert_of_row():
    e = np.arange(N_EXPERTS * MOE_R)[:, None] // MOE_R
    return jnp.asarray((e == np.arange(N_EXPERTS)[None, :]).astype(np.float32), BF16)


def _for_runs(cnt_ref, tile, k, fn):
    def body(e, c):
        @pl.when(cnt_ref[tile * N_EXPERTS + e] > k * MOE_R)
        def _():
            fn(e)
        return c
    lax.fori_loop(0, N_EXPERTS, body, 0)


def _moe_dispatch_kernel(start_ref, cnt_ref, npass_ref, hb_ref, slot_ref, pick_ref, xs_in_ref, xs_ref,
                         buf_ref, sem_ref):
    del xs_in_ref
    i = pl.program_id(0)
    E, R, GR = N_EXPERTS, MOE_R, MOE_GROUP * MOE_R
    half = i % 2
    h = hb_ref[...]
    slot_of_row = _dot(pick_ref[...], slot_ref[...].astype(BF16))
    want = (lax.broadcasted_iota(jnp.int32, (GR, 1), 0) % R + 1).astype(F32)

    def fill(k):
        for g in range(E * R // GR):
            hit = slot_of_row[g * GR:(g + 1) * GR] == want + k * R
            buf_ref[half, g * GR:(g + 1) * GR, :] = _dot(jnp.where(hit, 1.0, 0.0).astype(BF16), h).astype(BF16)

    def copy(tile, e, k, hf):
        dst = pl.multiple_of(start_ref[tile * E + e] + k * R, MOE_ALIGN)
        return pltpu.make_async_copy(buf_ref.at[hf, pl.ds(pl.multiple_of(e * R, R), R), :],
                                     xs_ref.at[pl.ds(dst, R), :], sem_ref.at[hf])

    fill(0)

    @pl.when(i > 0)
    def _():
        k_prev = npass_ref[i - 1] - 1
        _for_runs(cnt_ref, i - 1, k_prev, lambda e: copy(i - 1, e, k_prev, 1 - half).wait())

    _for_runs(cnt_ref, i, 0, lambda e: copy(i, e, 0, half).start())

    def more(k, c):
        _for_runs(cnt_ref, i, k - 1, lambda e: copy(i, e, k - 1, half).wait())
        fill(k)
        _for_runs(cnt_ref, i, k, lambda e: copy(i, e, k, half).start())
        return c

    lax.fori_loop(1, npass_ref[i], more, 0)

    @pl.when(i == pl.num_programs(0) - 1)
    def _():
        k_last = npass_ref[i] - 1
        _for_runs(cnt_ref, i, k_last, lambda e: copy(i, e, k_last, half).wait())


def _moe_dispatch(hb, slots, plan, nb, tt):
    t = hb.shape[0]
    start, cnt, n_pass, _, _ = plan
    rows = nb * MOE_BM
    pick = _expert_of_row()
    grid_spec = pltpu.PrefetchScalarGridSpec(
        num_scalar_prefetch=3,
        grid=(t // tt,),
        in_specs=[pl.BlockSpec((tt, D_MODEL), lambda i, *_: (i, 0)),
                  pl.BlockSpec((N_EXPERTS, tt), lambda i, *_: (0, i)),
                  pl.BlockSpec(pick.shape, lambda i, *_: (0, 0)),
                  pl.BlockSpec(memory_space=pl.ANY)],
        out_specs=pl.BlockSpec(memory_space=pl.ANY),
        scratch_shapes=[pltpu.VMEM((2, N_EXPERTS * MOE_R, D_MODEL), BF16), pltpu.SemaphoreType.DMA((2,))],
    )
    return pl.pallas_call(
        _moe_dispatch_kernel,
        grid_spec=grid_spec,
        out_shape=jax.ShapeDtypeStruct((rows, D_MODEL), BF16),
        input_output_aliases={6: 0},
        compiler_params=_params("arbitrary"),
        name="moe_dispatch",
    )(start, cnt, n_pass, hb, slots, pick, jnp.zeros((rows, D_MODEL), BF16))


def _moe_ffn_kernel(blk_exp_ref, nused_ref, x_ref, w1_ref, w3_ref, w2_ref, y_ref):
    del blk_exp_ref
    in_use = pl.program_id(0) < nused_ref[0]

    @pl.when(in_use)
    def _():
        a = _swiglu(x_ref[...], w1_ref[0], w3_ref[0])
        y_ref[...] = _dot(a.astype(BF16), w2_ref[0]).astype(y_ref.dtype)

    @pl.when(jnp.logical_not(in_use))
    def _():
        y_ref[...] = jnp.zeros(y_ref.shape, y_ref.dtype)


def _moe_ffn(xs, plan, nb, w1, w3, w2):
    _, _, _, blk_exp, n_used = plan
    grid_spec = pltpu.PrefetchScalarGridSpec(
        num_scalar_prefetch=2,
        grid=(nb,),
        in_specs=[pl.BlockSpec((MOE_BM, D_MODEL), lambda j, be, nu: (jnp.minimum(j, nu[0] - 1), 0)),
                  pl.BlockSpec((1, D_MODEL, EXPERT_FF), lambda j, be, nu: (be[j], 0, 0)),
                  pl.BlockSpec((1, D_MODEL, EXPERT_FF), lambda j, be, nu: (be[j], 0, 0)),
                  pl.BlockSpec((1, EXPERT_FF, D_MODEL), lambda j, be, nu: (be[j], 0, 0))],
        out_specs=pl.BlockSpec((MOE_BM, D_MODEL), lambda j, be, nu: (j, 0)),
    )
    return pl.pallas_call(
        _moe_ffn_kernel,
        grid_spec=grid_spec,
        out_shape=jax.ShapeDtypeStruct(xs.shape, BF16),
        compiler_params=_params("arbitrary"),
        name="moe_ffn",
    )(blk_exp, n_used, xs, w1, w3, w2)


def _moe_combine_kernel(start_ref, npass_ref, hb_ref, h_ref, gate_ref, slot_ref, pick_ref, ys_ref,
                        ws1_ref, ws3_ref, ws2_ref, g2_ref, b2_ref, y_ref, buf_ref, sem_ref):
    i = pl.program_id(0)
    n = pl.num_programs(0)
    E, R, GR = N_EXPERTS, MOE_R, MOE_GROUP * MOE_R
    half = i % 2

    def copy(tile, e, k, hf):
        src = pl.multiple_of(start_ref[tile * E + e] + k * R, MOE_ALIGN)
        return pltpu.make_async_copy(ys_ref.at[pl.ds(src, R), :],
                                     buf_ref.at[hf, pl.ds(pl.multiple_of(e * R, R), R), :], sem_ref.at[hf])

    def all_runs(fn):
        def body(e, c):
            fn(e)
            return c
        lax.fori_loop(0, E, body, 0)

    @pl.when(i == 0)
    def _():
        all_runs(lambda e: copy(0, e, 0, 0).start())

    @pl.when(i + 1 < n)
    def _():
        all_runs(lambda e: copy(i + 1, e, 0, 1 - half).start())

    pick = pick_ref[...]
    slot_of_row = _dot(pick, slot_ref[...].astype(BF16))
    gate_of_row = sum(_dot(pick, part) for part in _split3(gate_ref[...]))
    want = (lax.broadcasted_iota(jnp.int32, (GR, 1), 0) % R + 1).astype(F32)

    def gathered(k):
        acc = jnp.zeros(y_ref.shape, F32)
        for g in range(E * R // GR):
            rows = slice(g * GR, (g + 1) * GR)
            hit = slot_of_row[rows] == want + k * R
            gate = jnp.sum(jnp.where(hit, gate_of_row[rows], 0.0), axis=1, keepdims=True)
            weighted = (buf_ref[half, rows, :].astype(F32) * gate).astype(BF16)
            acc = acc + _dot_tn(jnp.where(hit, 1.0, 0.0).astype(BF16), weighted)
        return acc

    all_runs(lambda e: copy(i, e, 0, half).wait())
    routed = gathered(0)

    def more(k, acc):
        all_runs(lambda e: copy(i, e, k, half).start())
        all_runs(lambda e: copy(i, e, k, half).wait())
        return acc + gathered(k)

    routed = lax.fori_loop(1, npass_ref[i], more, routed)
    x = hb_ref[...]
    shared = _dot(_swiglu(x, ws1_ref[...], ws3_ref[...]).astype(BF16), ws2_ref[...])
    y_ref[...] = _layer_norm(ALPHA * h_ref[...] + (routed + shared), g2_ref[...], b2_ref[...])


def _moe_combine(hb, h, gates_t, slots, ys, plan, mw, tt):
    t = h.shape[0]
    start, _, n_pass, _, _ = plan
    _, _, _, ws1, ws3, ws2, g2, b2 = mw
    pick = _expert_of_row()
    consts = (pick,)
    weights = (ws1, ws3, ws2, g2, b2)
    row = lambda w: pl.BlockSpec((tt, w), lambda i, *_: (i, 0))
    col = pl.BlockSpec((N_EXPERTS, tt), lambda i, *_: (0, i))
    full = lambda a: pl.BlockSpec(a.shape, lambda i, *_: (0,) * a.ndim)
    grid_spec = pltpu.PrefetchScalarGridSpec(
        num_scalar_prefetch=2,
        grid=(t // tt,),
        in_specs=[row(D_MODEL), row(D_MODEL), col, col] + [full(a) for a in consts]
        + [pl.BlockSpec(memory_space=pl.ANY)] + [full(a) for a in weights],
        out_specs=row(D_MODEL),
        scratch_shapes=[pltpu.VMEM((2, N_EXPERTS * MOE_R, D_MODEL), BF16), pltpu.SemaphoreType.DMA((2,))],
    )
    return pl.pallas_call(
        _moe_combine_kernel,
        grid_spec=grid_spec,
        out_shape=jax.ShapeDtypeStruct((t, D_MODEL), F32),
        compiler_params=_params("arbitrary"),
        name="moe_combine",
    )(start, n_pass, hb, h, gates_t, slots, *consts, ys, *weights)


def _moe_weights(w_e1, w_e3, w_e2, w_s1, w_s3, w_s2, ln2_g, ln2_b):
    return (w_e1.astype(BF16), w_e3.astype(BF16), w_e2.astype(BF16), w_s1.astype(BF16), w_s3.astype(BF16),
            w_s2.astype(BF16), ln2_g.astype(F32)[None, :], ln2_b.astype(F32)[None, :])


def _finish_and_moe(x2d, ssd_y, nsa_y, gm, fw, mw, tt):
    t = x2d.shape[0]
    h, hb, gates_t, slots, run_len = _finish(x2d, ssd_y, nsa_y, gm, fw, tt)
    nb = _moe_blocks(t, tt)
    plan = _moe_plan(run_len[:, :, 0].astype(jnp.int32), nb)
    xs = _moe_dispatch(hb, slots, plan, nb, tt)
    ys = _moe_ffn(xs, plan, nb, mw[0], mw[1], mw[2])
    return _moe_combine(hb, h, gates_t, slots, ys, plan, mw, tt)


def _ssd_sample_kernel(xbc_ref, sc_ref, z_ref, sm_ref, st_ref, cw_ref, cb_ref, dtb_ref, alog_ref, dskip_ref,
                       nw_ref, e_ref, y_ref, sto_ref, xdt_t_ref, dec_t_ref, b_ref, c_ref, xs_ref, yt_ref):
    i = pl.program_id(0)
    db = xbc_ref.shape[0]
    G, N = SSD_GROUPS, SSD_STATE
    GW = SSD_HPG * SSD_HEAD_DIM

    @pl.when(i == 0)
    def _():
        conv = cb_ref[...] + xbc_ref[...] * cw_ref[SSD_CONV - 1:SSD_CONV, :]
        for k in range(SSD_CONV - 1):
            conv = conv + sc_ref[k] * cw_ref[k:k + 1, :]
        act = _silu(conv)
        xs = act[:, :SSD_D_INNER]
        xs_ref[...] = xs
        b_ref[...] = act[:, SSD_D_INNER:SSD_D_INNER + G * N].astype(BF16)
        c_ref[...] = act[:, SSD_D_INNER + G * N:].astype(BF16)
        dt = _softplus(sm_ref[...] + dtb_ref[...])
        dec = jnp.exp(dt * (-jnp.exp(alog_ref[...])))
        e_mat = e_ref[...]
        xdt_t_ref[...] = (xs * _dot_exact_rhs(dt, e_mat)).T.astype(BF16)
        dec_t_ref[...] = _dot_exact_rhs(dec, e_mat).T
        yt_ref[...] = jnp.zeros_like(yt_ref)

    is_row = lax.broadcasted_iota(jnp.int32, (db, N), 0) == i
    onehot = jnp.where(is_row, 1.0, 0.0).astype(BF16)
    is_lane = lax.broadcasted_iota(jnp.int32, (GW, db), 1) == i
    for g in range(G):
        rows = slice(g * GW, (g + 1) * GW)
        b_g = jnp.where(is_row, b_ref[:, g * N:(g + 1) * N], jnp.zeros((), BF16))
        contrib = _dot(xdt_t_ref[rows, :], b_g)
        decay = _dot_exact_rhs(dec_t_ref[rows, :], onehot)
        new = st_ref[0, rows, :] * decay + contrib
        sto_ref[0, rows, :] = new
        y_all = _dot_nt(new.astype(BF16), c_ref[:, g * N:(g + 1) * N])
        yt_ref[rows, :] += jnp.where(is_lane, y_all, 0.0)

    @pl.when(i == pl.num_programs(0) - 1)
    def _():
        y = yt_ref[...].T + dskip_ref[...] * xs_ref[...]
        y = y * _silu(z_ref[...])
        for g in range(G):
            sl = slice(g * GW, (g + 1) * GW)
            yg = y[:, sl]
            rs = lax.rsqrt(jnp.mean(yg * yg, axis=-1, keepdims=True) + NORM_EPS)
            y_ref[:, sl] = (yg * rs * nw_ref[:, sl]).astype(y_ref.dtype)


def _ssd_sample(xbc, state_conv, z, small, state_ssm, conv_w, conv_b, dt_bias, a_log, d_skip, norm_w):
    db = xbc.shape[0]
    pad = LANES - SSD_HEADS
    dtb = jnp.pad(dt_bias.astype(F32), (0, pad))[None, :]
    alog = jnp.pad(a_log.astype(F32), (0, pad))[None, :]
    dskip = jnp.repeat(d_skip.astype(F32), SSD_HEAD_DIM)[None, :]
    sc = state_conv.astype(F32).transpose(1, 0, 2)
    st = state_ssm.reshape(db, SSD_D_INNER, SSD_STATE)
    args = (xbc, sc, z, small, st, conv_w.astype(F32), conv_b.astype(F32)[None, :], dtb, alog, dskip,
            norm_w.astype(F32)[None, :], _head_expand_matrix())
    full = lambda a: pl.BlockSpec(a.shape, lambda i: (0,) * a.ndim)
    st_spec = pl.BlockSpec((1, SSD_D_INNER, SSD_STATE), lambda i: (i, 0, 0))
    y, st_new = pl.pallas_call(
        _ssd_sample_kernel,
        grid=(db,),
        in_specs=[full(a) for a in args[:4]] + [st_spec] + [full(a) for a in args[5:]],
        out_specs=[pl.BlockSpec((db, SSD_D_INNER), lambda i: (0, 0)), st_spec],
        out_shape=[jax.ShapeDtypeStruct((db, SSD_D_INNER), BF16),
                   jax.ShapeDtypeStruct((db, SSD_D_INNER, SSD_STATE), F32)],
        scratch_shapes=[pltpu.VMEM((SSD_D_INNER, db), BF16), pltpu.VMEM((SSD_D_INNER, db), F32),
                        pltpu.VMEM((db, SSD_GROUPS * SSD_STATE), BF16),
                        pltpu.VMEM((db, SSD_GROUPS * SSD_STATE), BF16),
                        pltpu.VMEM((db, SSD_D_INNER), F32), pltpu.VMEM((SSD_D_INNER, db), F32)],
        compiler_params=_params("arbitrary"),
        name="ssd_sample",
    )(*args)
    return y, st_new.reshape(state_ssm.shape)


def _nsa_sample_kernel(n_pages, *refs):
    pt_ref = refs[0]
    page_refs = refs[1:1 + n_pages]
    (win_ref, q_ref, kvn_ref, gate_ref, slope_ref, ovl_ref, w1ab_ref, w1flat_ref, pe_ref, b1_ref, w2_ref,
     b2_ref, o_ref, wino_ref, cmp_rows_ref, ks_ref, vs_ref, kw_ref, vw_ref, bsh_ref, kc_ref, vc_ref) = refs[1 + n_pages:]
    del pt_ref
    H, dh, G = NSA_HEADS, NSA_HEAD_DIM, NSA_KV_HEADS
    past = n_pages * PAGE_SIZE
    ncb = past // CMP_STRIDE
    nsb = ovl_ref.shape[1]
    wbuf = win_ref.shape[3]
    tiles = KVW // LANES

    spp = PAGE_SIZE // CMP_STRIDE
    src_row = lax.broadcasted_iota(jnp.int32, (PAGE_SIZE, PAGE_SIZE), 1)
    regroup = jnp.where(lax.broadcasted_iota(jnp.int32, (PAGE_SIZE, PAGE_SIZE), 0)
                        == spp * (src_row % CMP_STRIDE) + src_row // CMP_STRIDE, 1.0, 0.0).astype(BF16)
    for p in range(n_pages):
        r = slice(p * PAGE_SIZE, (p + 1) * PAGE_SIZE)
        for w in range(2):
            rows_wp = _dot_nt(regroup, page_refs[p][0, w].astype(BF16))
            for t in range(tiles):
                for l in range(CMP_STRIDE):
                    cmp_rows_ref[w * tiles + t, l, p * spp:(p + 1) * spp, :] = (
                        rows_wp[l * spp:(l + 1) * spp, t * LANES:(t + 1) * LANES])
        ks_ref[:, r] = page_refs[p][0, 2].astype(BF16)
        vs_ref[:, r] = page_refs[p][0, 3].astype(BF16)
    new8 = jnp.broadcast_to(kvn_ref[0], (8, 6 * KVW))
    first = lax.broadcasted_iota(jnp.int32, (KVW, LANES), 1) == 0

    def new_col(w):
        return new8[:, w * KVW:(w + 1) * KVW].T[:, 0:1]

    put = lambda w: jnp.where(first, new_col(w), 0.0).astype(BF16)
    ks_ref[:, past:past + LANES] = put(2)
    vs_ref[:, past:past + LANES] = put(3)
    kw_ref[:, 0:wbuf] = win_ref[0, 0].astype(BF16)
    vw_ref[:, 0:wbuf] = win_ref[0, 1].astype(BF16)
    kw_ref[:, wbuf:wbuf + LANES] = put(4)
    vw_ref[:, wbuf:wbuf + LANES] = put(5)
    last = lax.broadcasted_iota(jnp.int32, (KVW, wbuf), 1) == wbuf - 1
    for w in range(2):
        wino_ref[0, w] = jnp.where(last, new_col(4 + w), pltpu.roll(win_ref[0, w], wbuf - 1, 1))

    def compress(w, dst):
        for t in range(tiles):
            def load_strip(l, t=t):
                return cmp_rows_ref[w * tiles + t, l]

            def store(g, val, t=t):
                c0 = (t * GPT + g) * dh
                dst[:, c0:c0 + dh] = val.astype(BF16)

            sub = lambda ref: ref.at[pl.ds(w, 1)]
            _compress_body(load_strip, ncb, sub(w1ab_ref), sub(w1flat_ref), sub(pe_ref), sub(b1_ref), sub(w2_ref),
                           sub(b2_ref), bsh_ref.at[w], store)

    compress(0, kc_ref)

    head_r = lax.broadcasted_iota(jnp.int32, (H, KVW), 0) // NSA_HPG
    lane_g = lax.broadcasted_iota(jnp.int32, (H, KVW), 1) // dh
    diag = head_r == lane_g
    q16 = q_ref[0] * (dh ** -0.5)
    q_bd = jnp.where(diag, jnp.concatenate([q16] * G, axis=1), 0.0).astype(BF16)
    slopes = slope_ref[...]
    gates = _sigmoid(gate_ref[0])

    def fold(o):
        o = jnp.where(diag, o, 0.0)
        return sum(o[:, g * dh:(g + 1) * dh] for g in range(G))

    d_c = past - (lax.broadcasted_iota(jnp.int32, (H, ncb), 1) * CMP_STRIDE + (CMP_BLOCK - 1))
    p_c = _masked_softmax_rows(_dot_nt(q_bd, kc_ref[...]) - slopes * d_c.astype(F32), d_c >= 0)
    compress(1, vc_ref)
    o_c = fold(_dot(p_c.astype(BF16), vc_ref[...]))

    p_grp = jnp.concatenate(
        [jnp.sum(p_c[g * NSA_HPG:(g + 1) * NSA_HPG], axis=0, keepdims=True) for g in range(G)]
        + [jnp.zeros((8 - G, ncb), F32)], axis=0)
    imp = sum(_dot(part, ovl_ref[...]) for part in _split3(p_grp))
    m_l = lax.broadcasted_iota(jnp.int32, (8, nsb), 1)
    valid = m_l * SLC_BLOCK <= past
    cur = past // SLC_BLOCK
    forced = (m_l == 0) | (m_l == cur) | (m_l == cur - 1)
    v = jnp.where(valid, jnp.where(forced, jnp.inf, imp), -jnp.inf)
    v_t = v.T
    mp_i = lax.broadcasted_iota(jnp.int32, (nsb, nsb), 0)
    m_i = lax.broadcasted_iota(jnp.int32, (nsb, nsb), 1)
    sel_rows = []
    for g in range(G):
        v_col = jnp.broadcast_to(v_t[:, g:g + 1], (nsb, nsb))
        v_row = jnp.broadcast_to(v[g:g + 1, :], (nsb, nsb))
        ahead = (v_col > v_row) | ((v_col == v_row) & (mp_i < m_i))
        rank = jnp.sum(jnp.where(ahead, 1.0, 0.0), axis=0, keepdims=True)
        sel = jnp.where((rank < SLC_TOPK) & valid[0:1, :], 1.0, 0.0)
        sel_rows.append(jnp.broadcast_to(sel, (NSA_HPG, nsb)))
    sel_h = jnp.concatenate(sel_rows, axis=0).astype(BF16)

    nk = past + LANES
    expand = jnp.where(lax.broadcasted_iota(jnp.int32, (nsb, nk), 0)
                       == lax.broadcasted_iota(jnp.int32, (nsb, nk), 1) // SLC_BLOCK, 1.0, 0.0).astype(BF16)
    d_s = past - lax.broadcasted_iota(jnp.int32, (H, nk), 1)
    ok = (_dot(sel_h, expand) > 0.5) & (d_s >= 0)
    p_s = _masked_softmax_rows(_dot(q_bd, ks_ref[...]) - slopes * d_s.astype(F32), ok)
    o_s = fold(_dot_nt(p_s.astype(BF16), vs_ref[...]))

    nw = wbuf + LANES
    d_w = wbuf - lax.broadcasted_iota(jnp.int32, (H, nw), 1)
    p_w = _masked_softmax_rows(_dot(q_bd, kw_ref[...]) - slopes * d_w.astype(F32), (d_w >= 0) & (d_w < WINDOW))
    o_w = fold(_dot_nt(p_w.astype(BF16), vw_ref[...]))

    o_ref[0] = (gates[:, 0:1] * o_c + gates[:, 1:2] * o_s + gates[:, 2:3] * o_w).astype(o_ref.dtype)


def _overlap_sample(ncb, nsb):
    i = np.arange(ncb)[:, None] * CMP_STRIDE
    m = np.arange(nsb)[None, :] * SLC_BLOCK
    ok = (i < m + SLC_BLOCK) & (i + CMP_BLOCK > m) & (np.arange(ncb)[:, None] < ncb - 1)
    return ok.astype(np.float32)


def _nsa_sample(q, kv_new, small, cache_kv_paged, page_table, cache_kv_win, cw):
    db = q.shape[0]
    H, dh, G = NSA_HEADS, NSA_HEAD_DIM, NSA_KV_HEADS
    n_pages = page_table.shape[1]
    past = n_pages * PAGE_SIZE
    wbuf = cache_kv_win.shape[1]
    ncb = past // CMP_STRIDE
    nsb = LANES * pl.cdiv(pl.cdiv(past + 1, SLC_BLOCK), LANES)
    pages = cache_kv_paged.transpose(0, 2, 3, 4, 1).reshape(cache_kv_paged.shape[0], 4, KVW, PAGE_SIZE)
    win = cache_kv_win.transpose(0, 2, 3, 4, 1).reshape(db, 2, KVW, wbuf)
    gates_raw = small[:, SSD_HEADS:SSD_HEADS + 3 * H].reshape(db, H, 3)
    slopes = jnp.asarray(_alibi_slopes())[:, None]
    ovl = jnp.asarray(_overlap_sample(ncb, nsb), BF16)
    consts = (slopes, ovl) + tuple(cw)
    per_seq = lambda shape: pl.BlockSpec((1,) + shape, lambda i, pt: (i,) + (0,) * len(shape))
    full = lambda a: pl.BlockSpec(a.shape, lambda i, pt: (0,) * a.ndim)
    page_spec = lambda p: pl.BlockSpec((1, 4, KVW, PAGE_SIZE), lambda i, pt: (pt[i, p], 0, 0, 0))
    grid_spec = pltpu.PrefetchScalarGridSpec(
        num_scalar_prefetch=1,
        grid=(db,),
        in_specs=[page_spec(p) for p in range(n_pages)]
        + [per_seq((2, KVW, wbuf)), per_seq((H, dh)), per_seq((1, 6 * KVW)), per_seq((H, 3))]
        + [full(a) for a in consts],
        out_specs=[per_seq((H, dh)), per_seq((2, KVW, wbuf))],
        scratch_shapes=[pltpu.VMEM((2 * KVW // LANES, CMP_STRIDE, ncb, LANES), F32),
                        pltpu.VMEM((KVW, past + LANES), BF16), pltpu.VMEM((KVW, past + LANES), BF16),
                        pltpu.VMEM((KVW, wbuf + LANES), BF16), pltpu.VMEM((KVW, wbuf + LANES), BF16),
                        pltpu.VMEM((2, ncb + 8, CMP_HIDDEN), F32),
                        pltpu.VMEM((ncb, KVW), BF16), pltpu.VMEM((ncb, KVW), BF16)],
    )
    out, win_new = pl.pallas_call(
        functools.partial(_nsa_sample_kernel, n_pages),
        grid_spec=grid_spec,
        out_shape=[jax.ShapeDtypeStruct((db, H, dh), BF16), jax.ShapeDtypeStruct((db, 2, KVW, wbuf), cache_kv_win.dtype)],
        compiler_params=_params("arbitrary"),
        name="nsa_sample",
    )(page_table.astype(jnp.int32), *([pages] * n_pages), win, q.reshape(db, H, dh), kv_new.reshape(db, 1, 6 * KVW),
      gates_raw, *consts)
    win_new = win_new.reshape(db, 2, G, dh, wbuf).transpose(0, 4, 1, 2, 3)
    return out.reshape(db, H * dh), win_new


def kernel(x_prompt, x_sample, cache_kv_paged, cache_kv_win, state_ssm, state_conv, page_table, w_in, b_in, conv_w, conv_b, dt_bias, a_log, d_skip, ssd_norm_w, cmp_w1, cmp_b1, cmp_w2, cmp_b2, cmp_pe, w_ssd_down, w_nsa_down, w_out, b_out, ln1_g, ln1_b, w_router, b_router, w_e1, w_e3, w_e2, w_s1, w_s3, w_s2, ln2_g, ln2_b):
    b, L, _ = x_prompt.shape
    db = x_sample.shape[0]
    G, dh = NSA_KV_HEADS, NSA_HEAD_DIM
    wbuf = cache_kv_win.shape[1]
    wp, bp = _pack_w_in(w_in, b_in)
    cw = _compress_weights(cmp_w1, cmp_b1, cmp_w2, cmp_b2, cmp_pe)
    fw = _finish_weights(w_ssd_down, w_nsa_down, w_out, b_out, ln1_g, ln1_b, w_router, b_router)
    mw = _moe_weights(w_e1, w_e3, w_e2, w_s1, w_s3, w_s2, ln2_g, ln2_b)
    ssd_w = (conv_w, conv_b, dt_bias, a_log, d_skip, ssd_norm_w)
    tm = min(256, b * L)
    per_seq = lambda a: a.reshape(b, L, a.shape[-1])

    xp = x_prompt.reshape(b * L, D_MODEL)
    z, xbc, q, kv, gm, small = _in_proj(xp, wp, bp, tm)
    ssd_y, st_t = _ssd_prompt(per_seq(xbc), per_seq(z), per_seq(small), *ssd_w)
    kv_t = _kv_channel_major(x_prompt, w_in, b_in, min(512, L))
    nsa_y = _nsa_prompt_branch(per_seq(q), per_seq(kv), kv_t, per_seq(small), cw)
    yp = _finish_and_moe(xp, ssd_y.reshape(b * L, -1), nsa_y.reshape(b * L, -1), gm, fw, mw, tm).reshape(b, L, D_MODEL)
    kv6_t = kv_t.reshape(b, 6, G, dh, L)
    kv_rows_p = kv6_t[:, :4].transpose(0, 4, 1, 2, 3)
    win_p = kv6_t[:, 4:, :, :, L - wbuf:].transpose(0, 4, 1, 2, 3)
    ssm_p = st_t.reshape(b, SSD_STATE, SSD_HEADS, SSD_HEAD_DIM).transpose(0, 2, 3, 1).astype(state_ssm.dtype)
    conv_p = per_seq(xbc)[:, L - (SSD_CONV - 1):]

    xs = x_sample.reshape(db, D_MODEL)
    z, xbc, q, kv, gm, small = _in_proj(xs, wp, bp, db)
    ssd_y, ssm_s = _ssd_sample(xbc, state_conv, z, small, state_ssm, *ssd_w)
    nsa_y, win_s = _nsa_sample(q, kv, small, cache_kv_paged, page_table, cache_kv_win, cw)
    ys = _finish_and_moe(xs, ssd_y, nsa_y, gm, fw, mw, db).reshape(db, 1, D_MODEL)
    kv6 = kv.reshape(db, 1, 6, G, dh)
    kv_rows_s = kv6[:, :, :4]
    conv_s = jnp.concatenate([state_conv[:, 1:].astype(xbc.dtype), xbc[:, None, :]], axis=1)
    return (yp, ys, kv_rows_p, kv_rows_s, win_p, win_s, ssm_p, ssm_s, conv_p, conv_s)
```

```python
import functools

import numpy as np
import jax
import jax.numpy as jnp
from jax import lax
from jax.experimental import pallas as pl
from jax.experimental.pallas import tpu as pltpu

F32 = jnp.float32
BF16 = jnp.bfloat16

D_MODEL = 1024
PAGE_SIZE = 128
SSD_D_INNER = 2048
SSD_HEAD_DIM = 64
SSD_HEADS = 32
SSD_GROUPS = 4
SSD_HPG = 8
SSD_STATE = 128
SSD_CONV = 4
SSD_CHUNK = 128
SSD_CONV_CH = SSD_D_INNER + 2 * SSD_GROUPS * SSD_STATE
NSA_HEAD_DIM = 64
NSA_HEADS = 16
NSA_KV_HEADS = 4
NSA_HPG = 4
CMP_BLOCK = 32
CMP_STRIDE = 16
CMP_HIDDEN = 128
SLC_BLOCK = 64
SLC_TOPK = 16
WINDOW = 512
NSA_NQ = 128
N_EXPERTS = 64
TOP_K = 8
N_ROUTE_GROUPS = 8
TOPK_ROUTE_GROUPS = 4
EXPERT_FF = 256
ROUTE_SCALE = 2.5
ALPHA = 2.0 ** 0.25
NORM_EPS = 1e-5
IN_SPLITS = (SSD_D_INNER, SSD_CONV_CH, SSD_HEADS, NSA_HEADS * NSA_HEAD_DIM,
             6 * NSA_KV_HEADS * NSA_HEAD_DIM, 3 * NSA_HEADS, 2 * D_MODEL)
IN_OFFSETS = tuple(int(v) for v in np.cumsum(IN_SPLITS)[:-1])

LANES = 128
VMEM_LIMIT = 56 * 1024 * 1024

SEG_Z = (0, 2048)
SEG_XBC = (2048, 3072)
SEG_Q = (5120, 1024)
SEG_KV = (6144, 1536)
SEG_GM = (7680, 2048)
SEG_SMALL = (9728, 128)
N_PACKED = 9856


def _params(*sem):
    return pltpu.CompilerParams(dimension_semantics=sem, vmem_limit_bytes=VMEM_LIMIT)


def _silu(x):
    return x * (1.0 / (1.0 + jnp.exp(-x)))


def _sigmoid(x):
    return 1.0 / (1.0 + jnp.exp(-x))


def _softplus(x):
    return jnp.maximum(x, 0.0) + jnp.log(1.0 + jnp.exp(-jnp.abs(x)))


def _split2(x):
    hi = x.astype(BF16)
    lo = (x - hi.astype(F32)).astype(BF16)
    return hi, lo


def _split3(x):
    hi = x.astype(BF16)
    r = x - hi.astype(F32)
    mid = r.astype(BF16)
    lo = (r - mid.astype(F32)).astype(BF16)
    return hi, mid, lo


def _dot(a, b):
    return jnp.dot(a, b, preferred_element_type=F32)


def _dot_nt(a, b):
    return lax.dot_general(a, b, (((1,), (1,)), ((), ())), preferred_element_type=F32)


def _dot_tn(a, b):
    return lax.dot_general(a, b, (((0,), (0,)), ((), ())), preferred_element_type=F32)


def _dot_exact_rhs(x, sel):
    a, b, c = _split3(x)
    return _dot(a, sel) + _dot(b, sel) + _dot(c, sel)


def _pack_w_in(w_in, b_in):
    def pack(m):
        z, xbc, dt, q, kv, gn, gm = jnp.split(m, list(IN_OFFSETS), axis=-1)
        pad = jnp.zeros(m.shape[:-1] + (LANES - SSD_HEADS - 3 * NSA_HEADS,), m.dtype)
        return jnp.concatenate([z, xbc, q, kv, gm, dt, gn, pad], axis=-1)
    return pack(w_in).astype(BF16), pack(b_in[None, :])


def _in_proj_kernel(x_ref, w_ref, b_ref, z_ref, xbc_ref, q_ref, kv_ref, gm_ref, sm_ref):
    x = x_ref[...].astype(BF16)
    for ref, (off, width) in ((z_ref, SEG_Z), (xbc_ref, SEG_XBC), (q_ref, SEG_Q), (kv_ref, SEG_KV),
                              (gm_ref, SEG_GM), (sm_ref, SEG_SMALL)):
        ref[...] = _dot(x, w_ref[:, off:off + width]) + b_ref[:, off:off + width]


def _in_proj(x2d, w_packed, b_packed, tm):
    t = x2d.shape[0]
    segs = (SEG_Z, SEG_XBC, SEG_Q, SEG_KV, SEG_GM, SEG_SMALL)
    return pl.pallas_call(
        _in_proj_kernel,
        grid=(t // tm,),
        in_specs=[pl.BlockSpec((tm, D_MODEL), lambda i: (i, 0)),
                  pl.BlockSpec((D_MODEL, N_PACKED), lambda i: (0, 0), pipeline_mode=pl.Buffered(1)),
                  pl.BlockSpec((1, N_PACKED), lambda i: (0, 0))],
        out_specs=[pl.BlockSpec((tm, w), lambda i: (i, 0)) for _, w in segs],
        out_shape=[jax.ShapeDtypeStruct((t, w), F32) for _, w in segs],
        compiler_params=_params("arbitrary"),
        name="in_proj",
    )(x2d, w_packed, b_packed)


def _kv_channel_major_kernel(x_ref, wt_ref, bt_ref, o_ref):
    o_ref[0] = _dot_nt(wt_ref[...], x_ref[0].astype(BF16)) + bt_ref[...]


def _kv_channel_major(x, w_in, b_in, tm):
    b, L, _ = x.shape
    lo, n = IN_OFFSETS[3], IN_SPLITS[4]
    wt = w_in[:, lo:lo + n].T.astype(BF16)
    bt = b_in[lo:lo + n].astype(F32)[:, None]
    return pl.pallas_call(
        _kv_channel_major_kernel,
        grid=(b, L // tm),
        in_specs=[pl.BlockSpec((1, tm, D_MODEL), lambda i, j: (i, j, 0)),
                  pl.BlockSpec((n, D_MODEL), lambda i, j: (0, 0)),
                  pl.BlockSpec((n, 1), lambda i, j: (0, 0))],
        out_specs=pl.BlockSpec((1, n, tm), lambda i, j: (i, 0, j)),
        out_shape=jax.ShapeDtypeStruct((b, n, L), F32),
        compiler_params=_params("arbitrary", "arbitrary"),
        name="kv_channel_major",
    )(x, wt, bt)


def _head_expand_matrix():
    h = np.arange(LANES)[:, None]
    c = np.arange(SSD_D_INNER)[None, :] // SSD_HEAD_DIM
    return jnp.asarray((h == c).astype(np.float32), BF16)


def _ssd_prompt_kernel(xbc_ref, z_ref, sm_ref, cw_ref, cb_ref, dtb_ref, alog_ref, dskip_ref, nw_ref, e_ref,
                       y_ref, st_ref, xh_ref, state_ref, ybuf_ref):
    c = pl.program_id(1)
    L = SSD_CHUNK
    G, N, P = SSD_GROUPS, SSD_STATE, SSD_HEAD_DIM
    GW = SSD_HPG * P

    @pl.when(c == 0)
    def _():
        xh_ref[0:8, :] = jnp.zeros((8, SSD_CONV_CH), F32)
        state_ref[...] = jnp.zeros_like(state_ref)

    xh_ref[8:8 + L, :] = xbc_ref[0]
    conv = cb_ref[...] + xh_ref[5:5 + L, :] * cw_ref[0:1, :]
    for k in range(1, SSD_CONV):
        conv = conv + xh_ref[5 + k:5 + k + L, :] * cw_ref[k:k + 1, :]
    xh_ref[0:8, :] = xh_ref[L:L + 8, :]
    act = _silu(conv)
    xs = act[:, :SSD_D_INNER]
    bm = act[:, SSD_D_INNER:SSD_D_INNER + G * N].astype(BF16)
    cm = act[:, SSD_D_INNER + G * N:]

    dt = _softplus(sm_ref[0] + dtb_ref[...])
    da = dt * (-jnp.exp(alog_ref[...]))
    row_i = lax.broadcasted_iota(jnp.int32, (L, L), 0)
    col_i = lax.broadcasted_iota(jnp.int32, (L, L), 1)
    tri = row_i >= col_i
    a_cs = _dot_exact_rhs_left(tri, da)
    a_cs_t = a_cs.T
    dt_t = dt.T
    a_last = a_cs[L - 1:L, :]
    w_end = dt * jnp.exp(a_last - a_cs)
    chunk_decay = jnp.broadcast_to(jnp.exp(a_last), (8, LANES))
    e_mat = e_ref[...]
    w_exp = _dot_exact_rhs(w_end, e_mat)
    dec_exp = _dot_exact_rhs(chunk_decay, e_mat)[0:1, :]
    xd = (xs * w_exp).astype(BF16)

    for g in range(G):
        cg = cm[:, g * N:(g + 1) * N]
        bg = bm[:, g * N:(g + 1) * N]
        cb = _dot_nt(cg.astype(BF16), bg)
        for j in range(SSD_HPG):
            h = g * SSD_HPG + j
            col = jnp.broadcast_to(a_cs[:, h:h + 1], (L, L))
            row = a_cs_t[h:h + 1, :]
            lm = jnp.where(tri, jnp.exp(col - row), 0.0)
            m = (cb * lm * dt_t[h:h + 1, :]).astype(BF16)
            eac = (jnp.exp(col) * cg).astype(BF16)
            lhs = jnp.concatenate([m, eac], axis=1)
            rhs = jnp.concatenate([xs[:, h * P:(h + 1) * P].astype(BF16),
                                   state_ref[:, h * P:(h + 1) * P].astype(BF16)], axis=0)
            ybuf_ref[:, h * P:(h + 1) * P] = _dot(lhs, rhs)
        sl = slice(g * GW, (g + 1) * GW)
        state_ref[:, sl] = state_ref[:, sl] * dec_exp[:, sl] + _dot_tn(bg, xd[:, sl])

    y = ybuf_ref[...] + dskip_ref[...] * xs
    y = y * _silu(z_ref[0])
    for g in range(G):
        sl = slice(g * GW, (g + 1) * GW)
        yg = y[:, sl]
        rs = lax.rsqrt(jnp.mean(yg * yg, axis=-1, keepdims=True) + NORM_EPS)
        y_ref[0, :, sl] = (yg * rs * nw_ref[:, sl]).astype(y_ref.dtype)

    @pl.when(c == pl.num_programs(1) - 1)
    def _():
        st_ref[0] = state_ref[...]


def _dot_exact_rhs_left(mask, x):
    sel = jnp.where(mask, 1.0, 0.0).astype(BF16)
    a, b, c = _split3(x)
    return _dot(sel, a) + _dot(sel, b) + _dot(sel, c)


def _ssd_prompt(xbc, z, small, conv_w, conv_b, dt_bias, a_log, d_skip, norm_w):
    b, L, _ = xbc.shape
    nc = L // SSD_CHUNK
    pad = LANES - SSD_HEADS
    dtb = jnp.pad(dt_bias.astype(F32), (0, pad))[None, :]
    alog = jnp.pad(a_log.astype(F32), (0, pad))[None, :]
    dskip = jnp.repeat(d_skip.astype(F32), SSD_HEAD_DIM)[None, :]
    full = lambda shape: pl.BlockSpec(shape, lambda i, j: (0,) * len(shape))
    y, st = pl.pallas_call(
        _ssd_prompt_kernel,
        grid=(b, nc),
        in_specs=[pl.BlockSpec((1, SSD_CHUNK, SSD_CONV_CH), lambda i, j: (i, j, 0)),
                  pl.BlockSpec((1, SSD_CHUNK, SSD_D_INNER), lambda i, j: (i, j, 0)),
                  pl.BlockSpec((1, SSD_CHUNK, LANES), lambda i, j: (i, j, 0)),
                  full((SSD_CONV, SSD_CONV_CH)), full((1, SSD_CONV_CH)), full((1, LANES)), full((1, LANES)),
                  full((1, SSD_D_INNER)), full((1, SSD_D_INNER)), full((LANES, SSD_D_INNER))],
        out_specs=[pl.BlockSpec((1, SSD_CHUNK, SSD_D_INNER), lambda i, j: (i, j, 0)),
                   pl.BlockSpec((1, SSD_STATE, SSD_D_INNER), lambda i, j: (i, 0, 0))],
        out_shape=[jax.ShapeDtypeStruct((b, L, SSD_D_INNER), BF16),
                   jax.ShapeDtypeStruct((b, SSD_STATE, SSD_D_INNER), F32)],
        scratch_shapes=[pltpu.VMEM((SSD_CHUNK + 8, SSD_CONV_CH), F32),
                        pltpu.VMEM((SSD_STATE, SSD_D_INNER), F32),
                        pltpu.VMEM((SSD_CHUNK, SSD_D_INNER), F32)],
        compiler_params=_params("arbitrary", "arbitrary"),
        name="ssd_prompt",
    )(xbc, z, small, conv_w.astype(F32), conv_b.astype(F32)[None, :], dtb, alog, dskip,
      norm_w.astype(F32)[None, :], _head_expand_matrix())
    return y, st


HALF = CMP_BLOCK // 2
KVW = NSA_KV_HEADS * NSA_HEAD_DIM
ROWS_PER_DOT = 2


def _compress_weights(cmp_w1, cmp_b1, cmp_w2, cmp_b2, cmp_pe):
    w1 = cmp_w1.astype(BF16)
    w1ab = jnp.concatenate([w1[:, :HALF], w1[:, HALF:]], axis=-1)
    zero = jnp.zeros_like(w1ab)
    per_head = [jnp.stack([w1ab if h == g else zero for h in range(GPT)], axis=2) for g in range(GPT)]
    w1ab = jnp.stack(per_head, axis=1).reshape(2, GPT, HALF // ROWS_PER_DOT, ROWS_PER_DOT * LANES, 2 * CMP_HIDDEN)
    w1flat = w1.reshape(2, CMP_BLOCK * NSA_HEAD_DIM, CMP_HIDDEN)
    pe8 = jnp.broadcast_to(cmp_pe.reshape(2, 1, CMP_BLOCK * NSA_HEAD_DIM), (2, 8, CMP_BLOCK * NSA_HEAD_DIM))
    return (w1ab, w1flat, pe8.astype(F32), cmp_b1.astype(F32)[:, None, :], cmp_w2.astype(BF16),
            cmp_b2.astype(F32)[:, None, :])


GPT = LANES // NSA_HEAD_DIM


def _compress_body(load_strip, ns, w1ab_ref, w1flat_ref, pe_ref, b1_ref, w2_ref, b2_ref, bsh_ref, store):
    accs = [jnp.zeros((ns, 2 * CMP_HIDDEN), F32) for _ in range(GPT)]
    for c in range(HALF // ROWS_PER_DOT):
        lhs = jnp.concatenate([load_strip(c * ROWS_PER_DOT + i).astype(BF16) for i in range(ROWS_PER_DOT)], axis=1)
        for g in range(GPT):
            accs[g] = accs[g] + _dot(lhs, w1ab_ref[0, g, c])
    cpe = _dot(pe_ref[0].astype(BF16), w1flat_ref[0])[0:1, :] + b1_ref[0]
    bsh_ref[ns:ns + 8, :] = jnp.zeros((8, CMP_HIDDEN), F32)
    for g in range(GPT):
        bsh_ref[0:ns, :] = accs[g][:, CMP_HIDDEN:]
        hid = _silu(accs[g][:, :CMP_HIDDEN] + bsh_ref[1:ns + 1, :] + cpe)
        store(g, _dot(hid.astype(BF16), w2_ref[0]) + b2_ref[0])


def _compress_prompt_kernel(kv_ref, w1ab_ref, w1flat_ref, pe_ref, b1_ref, w2_ref, b2_ref, out_ref, bsh_ref):
    ns = out_ref.shape[3]

    def load_strip(l):
        return kv_ref[0, pl.ds(l, ns, stride=CMP_STRIDE), :]

    def store(g, val):
        out_ref[0, 0, g] = val

    _compress_body(load_strip, ns, w1ab_ref, w1flat_ref, pe_ref, b1_ref, w2_ref, b2_ref, bsh_ref, store)


def _compress_prompt(kv, cw):
    b, L, _ = kv.shape
    ns = L // CMP_STRIDE
    w1ab, w1flat, pe8, b1, w2, b2 = cw
    tiles = KVW // LANES
    per_which = lambda shape: pl.BlockSpec((1,) + shape, lambda i, w, t: (w,) + (0,) * len(shape))
    return pl.pallas_call(
        _compress_prompt_kernel,
        grid=(b, 2, tiles),
        in_specs=[pl.BlockSpec((1, L, LANES), lambda i, w, t: (i, 0, w * tiles + t)),
                  per_which((GPT, HALF // ROWS_PER_DOT, ROWS_PER_DOT * LANES, 2 * CMP_HIDDEN)),
                  per_which((CMP_BLOCK * NSA_HEAD_DIM, CMP_HIDDEN)),
                  per_which((8, CMP_BLOCK * NSA_HEAD_DIM)),
                  per_which((1, CMP_HIDDEN)),
                  per_which((CMP_HIDDEN, NSA_HEAD_DIM)),
                  per_which((1, NSA_HEAD_DIM))],
        out_specs=pl.BlockSpec((1, 1, GPT, ns, NSA_HEAD_DIM), lambda i, w, t: (i, w, t, 0, 0)),
        out_shape=jax.ShapeDtypeStruct((b, 2, NSA_KV_HEADS, ns, NSA_HEAD_DIM), F32),
        scratch_shapes=[pltpu.VMEM((ns + 8, CMP_HIDDEN), F32)],
        compiler_params=_params("arbitrary", "arbitrary", "arbitrary"),
        name="compress_prompt",
    )(kv, w1ab, w1flat, pe8, b1, w2, b2)


NEG = -1e30
SEL_TILE = 512
WIN_KEYS = WINDOW + NSA_NQ
WIN_CHUNK = 128


def _alibi_slopes():
    h = np.arange(1, NSA_HEADS + 1, dtype=np.float32)
    return (2.0 ** (-8.0 * h / NSA_HEADS)).astype(np.float32)


def _overlap_t(ncb, ns):
    i = np.arange(ncb)[None, :] * CMP_STRIDE
    m = np.arange(ns)[:, None] * SLC_BLOCK
    return ((i < m + SLC_BLOCK) & (i + CMP_BLOCK > m)).astype(np.float32)


def _masked_softmax_rows(s, mask):
    sm = jnp.where(mask, s, NEG)
    mx = jnp.max(sm, axis=-1, keepdims=True)
    p = jnp.where(mask, jnp.exp(sm - mx), 0.0)
    return p / jnp.maximum(jnp.sum(p, axis=-1, keepdims=True), 1e-30)


def _select_blocks_t(imp_t, qpos_row, v_ref, n_live):
    ns, nq = imp_t.shape
    m_i = lax.broadcasted_iota(jnp.int32, (ns, nq), 0)
    valid = m_i * SLC_BLOCK <= qpos_row
    cur = qpos_row // SLC_BLOCK
    forced = (m_i == 0) | (m_i == cur) | (m_i == cur - 1)
    del n_live
    bits = pltpu.bitcast(jnp.where(forced, jnp.inf, imp_t), jnp.int32)
    int_min = jnp.int32(-2 ** 31)
    key = jnp.where(valid, jnp.where(bits < 0, bits ^ jnp.int32(2 ** 31 - 1), bits), int_min)
    v_ref[...] = pltpu.bitcast(key, F32)
    k = float(SLC_TOPK)

    def count_ge(t):
        terms = [jnp.where(pltpu.bitcast(v_ref[8 * s:8 * s + 8, :], jnp.int32) >= t, 1.0, 0.0)
                 for s in range(ns // 8)]
        while len(terms) > 1:
            terms = [a + b for a, b in zip(terms[0::2], terms[1::2])] + terms[len(terms) & ~1:]
        part = terms[0]
        for shift in (4, 2, 1):
            part = part + pltpu.roll(part, shift, 0)
        return part

    def step(i, t):
        cand = t + (jnp.int32(1) << (30 - i))
        return jnp.where(count_ge(cand) >= k, cand, t)

    zero = jnp.zeros((8, nq), jnp.int32)
    t0 = jnp.where(count_ge(zero) >= k, zero, int_min)
    thr = lax.fori_loop(0, 31, step, t0)[0:1, :]
    above = jnp.where(key > thr, 1.0, 0.0)
    tied = jnp.where(key == thr, 1.0, 0.0)
    room = k - jnp.sum(above, axis=0, keepdims=True)
    lower = jnp.where(lax.broadcasted_iota(jnp.int32, (ns, ns), 1) < lax.broadcasted_iota(jnp.int32, (ns, ns), 0),
                      1.0, 0.0).astype(BF16)
    ties_before = _dot(lower, tied.astype(BF16))
    chosen = jnp.where(above > 0.0, 1.0, jnp.where(ties_before < room, tied, 0.0))
    return jnp.where(valid, chosen, 0.0)


BIG = 2.0 ** 20
POS_PERIOD = SEL_TILE


def _nsa_prompt_kernel(slopes_ref, q_ref, gate_ref, qpc_ref, kct_ref, vc_ref, kst_ref, va_ref, kwt_ref, vwp_ref,
                       pos_ref, blk_ref, ovl_ref, o_ref, v_ref, kaug_ref, kwaug_ref, m_ref, acc_ref):
    g = pl.program_id(1)
    qi = pl.program_id(2)
    nq, dh, J = NSA_NQ, NSA_HEAD_DIM, NSA_HPG
    ns, ncb = ovl_ref.shape
    L = kst_ref.shape[2]
    start = qi * nq
    slopes = [slopes_ref[g * J + j] for j in range(J)]
    rows = [slice(j * nq, (j + 1) * nq) for j in range(J)]

    @pl.when(qi == 0)
    def _():
        kaug_ref[0:dh, :] = kst_ref[0].astype(BF16)
        kaug_ref[dh:LANES, :] = pos_ref[...]
        kaug_ref[LANES:, :] = blk_ref[...]
        kwaug_ref[0:dh, 0:WINDOW] = jnp.zeros((dh, WINDOW), BF16)
        kwaug_ref[dh:LANES, 0:WINDOW] = pos_ref[:, 0:WINDOW]
        kwaug_ref[0:dh, WINDOW:] = kwt_ref[0].astype(BF16)
        kwaug_ref[dh:LANES, WINDOW:] = pos_ref[...]

    m_ref[...] = jnp.full(m_ref.shape, NEG, F32)
    acc_ref[...] = jnp.zeros(acc_ref.shape, F32)
    qb = q_ref[0] * (dh ** -0.5)
    q_all = jnp.concatenate([qb[:, j * dh:(j + 1) * dh].astype(BF16) for j in range(J)], axis=0)
    q_pos = jnp.concatenate([q_all, qpc_ref[0].astype(BF16)], axis=1)
    gates = _sigmoid(gate_ref[0, 0])

    s_all = _dot(q_all, kct_ref[0, 0])
    rel_end = (lax.broadcasted_iota(jnp.int32, (1, ncb), 1) * CMP_STRIDE + (CMP_BLOCK - 1)) - start
    mask_c = lax.broadcasted_iota(jnp.int32, (nq, ncb), 0) >= rel_end
    rel_endf = rel_end.astype(F32)
    vc = vc_ref[0, 0]
    o_c, p_sum = [], jnp.zeros((nq, ncb), F32)
    for j in range(J):
        sm = jnp.where(mask_c, s_all[rows[j]] + slopes[j] * rel_endf, NEG)
        mx = jnp.max(sm, axis=-1, keepdims=True)
        e = jnp.exp(sm - jnp.where(mx > 0.5 * NEG, mx, 0.0))
        inv = 1.0 / jnp.maximum(jnp.sum(e, axis=-1, keepdims=True), 1e-30)
        p_sum = p_sum + e * inv
        o_c.append(_dot(e.astype(BF16), vc) * inv)

    ovl = ovl_ref[...]
    imp_t = sum(_dot_nt(ovl, part) for part in _split3(p_sum))
    qpos_row = start + lax.broadcasted_iota(jnp.int32, (ns, nq), 1)
    sel_t = _select_blocks_t(imp_t, qpos_row, v_ref, start // SLC_BLOCK + nq // SLC_BLOCK)
    not_q = ((sel_t.T - 1.0) * BIG).astype(BF16)
    q_aug = jnp.concatenate([q_pos, jnp.concatenate([not_q] * J, axis=0)], axis=1)

    tk = SEL_TILE
    causal = (lax.broadcasted_iota(jnp.int32, (nq, tk), 0) - lax.broadcasted_iota(jnp.int32, (nq, tk), 1))

    def sel_tile(k0, on_diagonal):
        s = _dot(q_aug, kaug_ref[:, pl.ds(k0, tk)])
        vt = va_ref[0, 0, pl.ds(k0, tk), :]
        shift = (k0 - start).astype(F32)
        for j in range(J):
            sj = s[rows[j]]
            if on_diagonal:
                sj = jnp.where(causal + (start - k0) >= 0, sj, -BIG)
            c = slopes[j] * shift
            m_old = m_ref[rows[j], :]
            m_new = jnp.maximum(m_old, jnp.max(sj, axis=-1, keepdims=True) + c)
            p = jnp.exp(sj - (m_new - c))
            acc_ref[rows[j], :] = acc_ref[rows[j], :] * jnp.exp(m_old - m_new) + _dot(p.astype(BF16), vt)
            m_ref[rows[j], :] = m_new

    n_full = start // tk
    bpt = tk // SLC_BLOCK
    v_ref[...] = sel_t

    def full_tile(kt, carry):
        chosen_here = jnp.max(v_ref[pl.ds(pl.multiple_of(kt * bpt, bpt), bpt), :])

        @pl.when(chosen_here > 0.5)
        def _():
            sel_tile(pl.multiple_of(kt * tk, tk), False)

        return carry

    lax.fori_loop(0, n_full, full_tile, 0)
    sel_tile(pl.multiple_of(n_full * tk, tk), True)

    s_w = _dot(q_pos, kwaug_ref[:, pl.ds(pl.multiple_of(start, nq), WIN_KEYS)])
    vw = vwp_ref[0, 0, pl.ds(pl.multiple_of(start, nq), WIN_KEYS), :]
    qrow = lax.broadcasted_iota(jnp.int32, (nq, WIN_KEYS), 0)
    wcol = lax.broadcasted_iota(jnp.int32, (nq, WIN_KEYS), 1)
    mask_w = (wcol > qrow) & (wcol - WINDOW <= qrow)
    chunk = lax.broadcasted_iota(jnp.int32, (1, WIN_KEYS), 1) // WIN_CHUNK
    p0 = start - WINDOW + chunk * WIN_CHUNK
    origin = (jnp.maximum(p0, 0) // POS_PERIOD) * POS_PERIOD - start
    for j in range(J):
        bias = jnp.where(p0 >= 0, slopes[j] * origin.astype(F32), -BIG)
        sm = jnp.where(mask_w, s_w[rows[j]] + bias, NEG)
        p = jnp.exp(sm - jnp.max(sm, axis=-1, keepdims=True))
        o_w = _dot(p.astype(BF16), vw) / jnp.sum(p, axis=-1, keepdims=True)
        acc = acc_ref[rows[j], :]
        o_s = acc[:, 0:dh] / jnp.maximum(acc[:, dh:dh + 1], 1e-30)
        out = (gates[:, 3 * j:3 * j + 1] * o_c[j] + gates[:, 3 * j + 1:3 * j + 2] * o_s
               + gates[:, 3 * j + 2:3 * j + 3] * o_w)
        o_ref[0, :, j * dh:(j + 1) * dh] = out.astype(o_ref.dtype)


def _slope_columns():
    s = jnp.asarray(_alibi_slopes())
    pieces = jnp.stack([p.astype(F32) for p in _split3(s)], axis=1)
    cols = jnp.concatenate([pieces * SLC_BLOCK, pieces,
                            jnp.zeros((NSA_HEADS, NSA_HEAD_DIM - 6), F32)], axis=1)
    return jnp.repeat(cols, NSA_NQ, axis=0).reshape(NSA_KV_HEADS, NSA_HPG * NSA_NQ, NSA_HEAD_DIM)


def _position_rows(L):
    t = np.arange(L) % POS_PERIOD
    rows = np.zeros((NSA_HEAD_DIM, L), np.float32)
    rows[0:3] = t // SLC_BLOCK
    rows[3:6] = t % SLC_BLOCK
    return jnp.asarray(rows, BF16)


def _block_rows(ns, L):
    return jnp.asarray((np.arange(ns)[:, None] == np.arange(L)[None, :] // SLC_BLOCK).astype(np.float32), BF16)


def _nsa_prompt(q, gates_raw, kct, vc, kv_t, va, vwp):
    b, L, _ = q.shape
    G, dh, J = NSA_KV_HEADS, NSA_HEAD_DIM, NSA_HPG
    ncb, ns = L // CMP_STRIDE, L // SLC_BLOCK
    nqb = L // NSA_NQ
    per_bg = lambda shape: pl.BlockSpec((1, 1) + shape, lambda i, g, t, s: (i, g, 0, 0))
    const = lambda shape: pl.BlockSpec(shape, lambda i, g, t, s: (0,) * len(shape))
    kv_rows = lambda w: pl.BlockSpec((1, dh, L), lambda i, g, t, s: (i, w * G + g, 0))
    grid_spec = pltpu.PrefetchScalarGridSpec(
        num_scalar_prefetch=1,
        grid=(b, G, nqb),
        in_specs=[pl.BlockSpec((1, NSA_NQ, J * dh), lambda i, g, t, s: (i, t, g)),
                  pl.BlockSpec((1, 1, NSA_NQ, 3 * J), lambda i, g, t, s: (i, g, t, 0)),
                  pl.BlockSpec((1, J * NSA_NQ, dh), lambda i, g, t, s: (g, 0, 0)),
                  per_bg((dh, ncb)), per_bg((ncb, dh)), kv_rows(2), per_bg((L, LANES)),
                  kv_rows(4), per_bg((WINDOW + L, dh)),
                  const((dh, L)), const((ns, L)), const((ns, ncb))],
        out_specs=pl.BlockSpec((1, NSA_NQ, J * dh), lambda i, g, t, s: (i, t, g)),
        scratch_shapes=[pltpu.VMEM((ns, NSA_NQ), F32),
                        pltpu.VMEM((LANES + ns, L), BF16), pltpu.VMEM((LANES, WINDOW + L), BF16),
                        pltpu.VMEM((J * NSA_NQ, 1), F32), pltpu.VMEM((J * NSA_NQ, LANES), F32)],
    )
    return pl.pallas_call(
        _nsa_prompt_kernel,
        grid_spec=grid_spec,
        out_shape=jax.ShapeDtypeStruct((b, L, G * J * dh), BF16),
        compiler_params=_params("arbitrary", "arbitrary", "arbitrary"),
        name="nsa_prompt",
    )(jnp.asarray(_alibi_slopes()), q, gates_raw, _slope_columns(), kct, vc, kv_t, va, kv_t, vwp,
      _position_rows(L), _block_rows(ns, L), jnp.asarray(_overlap_t(ncb, ns), BF16))


def _nsa_prompt_branch(q, kv, kv_t, small, cw):
    b, L, _ = q.shape
    G, dh = NSA_KV_HEADS, NSA_HEAD_DIM
    cmp = _compress_prompt(kv, cw).astype(BF16)
    kct = cmp[:, 0].transpose(0, 1, 3, 2)
    vc = cmp[:, 1]
    rows = lambda w: kv[:, :, w * KVW:(w + 1) * KVW].astype(BF16).reshape(b, L, G, dh).transpose(0, 2, 1, 3)
    va = jnp.concatenate([rows(3), jnp.ones((b, G, L, 1), BF16), jnp.zeros((b, G, L, LANES - dh - 1), BF16)], axis=-1)
    vwp = jnp.pad(rows(5), ((0, 0), (0, 0), (WINDOW, 0), (0, 0)))
    gates_raw = small[:, :, SSD_HEADS:SSD_HEADS + 3 * NSA_HEADS].reshape(b, L, G, 3 * NSA_HPG).transpose(0, 2, 1, 3)
    return _nsa_prompt(q, gates_raw, kct, vc, kv_t, va, vwp)


def _layer_norm(v, g, b):
    mu = jnp.mean(v, axis=-1, keepdims=True)
    c = v - mu
    var = jnp.mean(c * c, axis=-1, keepdims=True)
    return c * lax.rsqrt(var + NORM_EPS) * g + b


def _route_t(logits_t, bias_col, v_ref):
    e_n, n = logits_t.shape
    gsz = e_n // N_ROUTE_GROUPS
    scores = _sigmoid(logits_t)
    biased = scores + bias_col
    grp = []
    for a in range(N_ROUTE_GROUPS):
        blk = biased[a * gsz:(a + 1) * gsz]
        m1 = jnp.max(blk, axis=0, keepdims=True)
        cnt = jnp.sum(jnp.where(blk == m1, 1.0, 0.0), axis=0, keepdims=True)
        m2 = jnp.max(jnp.where(blk < m1, blk, -jnp.inf), axis=0, keepdims=True)
        grp.append(m1 + jnp.where(cnt >= 2.0, m1, m2))
    pieces = []
    for a in range(N_ROUTE_GROUPS):
        rank = jnp.zeros((1, n), jnp.int32)
        for b in range(N_ROUTE_GROUPS):
            if b != a:
                ahead = (grp[b] > grp[a]) | ((grp[b] == grp[a]) & (b < a))
                rank = rank + jnp.where(ahead, 1, 0)
        pieces.append(jnp.where(rank < TOPK_ROUTE_GROUPS, biased[a * gsz:(a + 1) * gsz], -jnp.inf))
    masked = jnp.concatenate(pieces, axis=0)
    v_ref[...] = masked
    e_i = lax.broadcasted_iota(jnp.int32, (e_n, n), 0)

    def body(ep, rank):
        r = v_ref[pl.ds(ep, 1), :]
        ahead = (r > masked) | ((r == masked) & (ep < e_i))
        return rank + jnp.where(ahead, 1, 0)

    rank = lax.fori_loop(0, e_n, body, jnp.zeros((e_n, n), jnp.int32))
    picked = jnp.where(rank < TOP_K, scores, 0.0)
    return picked / jnp.sum(picked, axis=0, keepdims=True) * ROUTE_SCALE


def _finish_kernel(x_ref, sy_ref, ny_ref, gm_ref, wsd_ref, wnd_ref, wo_ref, bo_ref, g1_ref, b1_ref,
                   wrh_ref, wrl_ref, br_ref, h_ref, hb_ref, gt_ref, slot_ref, cnt_ref, v_ref):
    gate = _sigmoid(gm_ref[...])
    mixed = (gate[:, :D_MODEL] * _dot(sy_ref[...], wsd_ref[...])
             + gate[:, D_MODEL:] * _dot(ny_ref[...], wnd_ref[...]))
    o = _dot(mixed.astype(BF16), wo_ref[...]) + bo_ref[...]
    h = _layer_norm(ALPHA * x_ref[...] + o, g1_ref[...], b1_ref[...])
    h_ref[...] = h
    hb_ref[...] = h.astype(BF16)
    h_hi, h_lo = _split2(h)
    logits_t = _dot_nt(wrh_ref[...], h_hi) + _dot_nt(wrh_ref[...], h_lo) + _dot_nt(wrl_ref[...], h_hi)
    gates_t = _route_t(logits_t, br_ref[...], v_ref)
    gt_ref[...] = gates_t
    tm = gates_t.shape[1]
    routed = gates_t > 0.0
    earlier = jnp.where(lax.broadcasted_iota(jnp.int32, (tm, tm), 0) < lax.broadcasted_iota(jnp.int32, (tm, tm), 1),
                        1.0, 0.0).astype(BF16)
    ones = jnp.where(routed, 1.0, 0.0)
    slot_ref[...] = jnp.where(routed, _dot(ones.astype(BF16), earlier) + 1.0, 0.0)
    cnt_ref[0] = jnp.broadcast_to(jnp.sum(ones, axis=1, keepdims=True), (N_EXPERTS, LANES))


def _finish(x2d, ssd_y, nsa_y, gm, fw, tm):
    t = x2d.shape[0]
    wsd, wnd, wo, bo, g1, b1, wrh, wrl, br = fw
    row = lambda w: pl.BlockSpec((tm, w), lambda i: (i, 0))
    full = lambda a: pl.BlockSpec(a.shape, lambda i: (0,) * a.ndim)
    col = pl.BlockSpec((N_EXPERTS, tm), lambda i: (0, i))
    return pl.pallas_call(
        _finish_kernel,
        grid=(t // tm,),
        in_specs=[row(D_MODEL), row(SSD_D_INNER), row(D_MODEL), row(2 * D_MODEL)] + [full(a) for a in fw],
        out_specs=[row(D_MODEL), row(D_MODEL), col, col, pl.BlockSpec((1, N_EXPERTS, LANES), lambda i: (i, 0, 0))],
        out_shape=[jax.ShapeDtypeStruct((t, D_MODEL), F32), jax.ShapeDtypeStruct((t, D_MODEL), BF16),
                   jax.ShapeDtypeStruct((N_EXPERTS, t), F32), jax.ShapeDtypeStruct((N_EXPERTS, t), F32),
                   jax.ShapeDtypeStruct((t // tm, N_EXPERTS, LANES), F32)],
        scratch_shapes=[pltpu.VMEM((N_EXPERTS, tm), F32)],
        compiler_params=_params("arbitrary"),
        name="finish",
    )(x2d, ssd_y, nsa_y, gm, *fw)


def _finish_weights(w_ssd_down, w_nsa_down, w_out, b_out, ln1_g, ln1_b, w_router, b_router):
    wr_t = w_router.astype(F32).T
    wrh = wr_t.astype(BF16)
    wrl = (wr_t - wrh.astype(F32)).astype(BF16)
    return (w_ssd_down.astype(BF16), w_nsa_down.astype(BF16), w_out.astype(BF16), b_out.astype(F32)[None, :],
            ln1_g.astype(F32)[None, :], ln1_b.astype(F32)[None, :], wrh, wrl, b_router.astype(F32)[:, None])


MOE_R = 64
MOE_ALIGN = 16
MOE_BM = 512
MOE_TAIL = MOE_BM
MOE_GROUP = 8


def _swiglu(x, w1, w3):
    return _silu(_dot(x, w1)) * _dot(x, w3)


def _moe_blocks(t, tt):
    rows = (t * TOP_K + (t // tt) * N_EXPERTS * (MOE_ALIGN - 1) + N_EXPERTS * (MOE_R + MOE_BM - 1))
    return pl.cdiv(rows, MOE_BM) + MOE_TAIL // MOE_BM


def _moe_plan(cnt, nb):
    aligned = (cnt + MOE_ALIGN - 1) // MOE_ALIGN * MOE_ALIGN
    before = jnp.cumsum(aligned, axis=0) - aligned
    total = jnp.sum(aligned, axis=0)
    region = (total + MOE_R + MOE_BM - 1) // MOE_BM * MOE_BM
    region_end = jnp.cumsum(region)
    start = (region_end - region)[None, :] + before
    n_pass = jnp.maximum(jnp.max((cnt + MOE_R - 1) // MOE_R, axis=1), 1)
    blk_row0 = jnp.arange(nb, dtype=region_end.dtype) * MOE_BM
    blk_exp = jnp.minimum(jnp.sum(region_end[None, :] <= blk_row0[:, None], axis=1), N_EXPERTS - 1)
    i32 = lambda a: a.astype(jnp.int32)
    return (i32(start.reshape(-1)), i32(cnt.reshape(-1)), i32(n_pass), i32(blk_exp),
            i32(region_end[-1:] // MOE_BM))


def _chunk_hits(slot_ref, g, k):
    want = (lax.broadcasted_iota(jnp.int32, (MOE_R, 1), 0) + (k * MOE_R + 1)).astype(F32)
    return [slot_ref[e:e + 1, :] == want for e in range(g * MOE_GROUP, (g + 1) * MOE_GROUP)]


def _for_runs(cnt_ref, tile, k, fn):
    def body(e, c):
        @pl.when(cnt_ref[tile * N_EXPERTS + e] > k * MOE_R)
        def _():
            fn(e)
        return c
    lax.fori_loop(0, N_EXPERTS, body, 0)


def _moe_dispatch_kernel(start_ref, cnt_ref, npass_ref, hb_ref, slot_ref, xs_in_ref, xs_ref, buf_ref, sem_ref):
    del xs_in_ref
    i = pl.program_id(0)
    E, R, GR = N_EXPERTS, MOE_R, MOE_GROUP * MOE_R
    half = i % 2
    h = hb_ref[...]

    def fill(k):
        for g in range(E // MOE_GROUP):
            onehot = jnp.concatenate([jnp.where(hit, 1.0, 0.0).astype(BF16) for hit in _chunk_hits(slot_ref, g, k)],
                                     axis=0)
            buf_ref[half, g * GR:(g + 1) * GR, :] = _dot(onehot, h).astype(BF16)

    def copy(tile, e, k, hf):
        dst = pl.multiple_of(start_ref[tile * E + e] + k * R, MOE_ALIGN)
        return pltpu.make_async_copy(buf_ref.at[hf, pl.ds(pl.multiple_of(e * R, R), R), :],
                                     xs_ref.at[pl.ds(dst, R), :], sem_ref.at[hf])

    fill(0)

    @pl.when(i > 0)
    def _():
        k_prev = npass_ref[i - 1] - 1
        _for_runs(cnt_ref, i - 1, k_prev, lambda e: copy(i - 1, e, k_prev, 1 - half).wait())

    _for_runs(cnt_ref, i, 0, lambda e: copy(i, e, 0, half).start())

    def more(k, c):
        _for_runs(cnt_ref, i, k - 1, lambda e: copy(i, e, k - 1, half).wait())
        fill(k)
        _for_runs(cnt_ref, i, k, lambda e: copy(i, e, k, half).start())
        return c

    lax.fori_loop(1, npass_ref[i], more, 0)

    @pl.when(i == pl.num_programs(0) - 1)
    def _():
        k_last = npass_ref[i] - 1
        _for_runs(cnt_ref, i, k_last, lambda e: copy(i, e, k_last, half).wait())


def _moe_dispatch(hb, slots, plan, nb, tt):
    t = hb.shape[0]
    start, cnt, n_pass, _, _ = plan
    rows = nb * MOE_BM
    grid_spec = pltpu.PrefetchScalarGridSpec(
        num_scalar_prefetch=3,
        grid=(t // tt,),
        in_specs=[pl.BlockSpec((tt, D_MODEL), lambda i, *_: (i, 0)),
                  pl.BlockSpec((N_EXPERTS, tt), lambda i, *_: (0, i)),
                  pl.BlockSpec(memory_space=pl.ANY)],
        out_specs=pl.BlockSpec(memory_space=pl.ANY),
        scratch_shapes=[pltpu.VMEM((2, N_EXPERTS * MOE_R, D_MODEL), BF16), pltpu.SemaphoreType.DMA((2,))],
    )
    return pl.pallas_call(
        _moe_dispatch_kernel,
        grid_spec=grid_spec,
        out_shape=jax.ShapeDtypeStruct((rows, D_MODEL), BF16),
        input_output_aliases={5: 0},
        compiler_params=_params("arbitrary"),
        name="moe_dispatch",
    )(start, cnt, n_pass, hb, slots, jnp.zeros((rows, D_MODEL), BF16))


def _moe_ffn_kernel(blk_exp_ref, nused_ref, x_ref, w1_ref, w3_ref, w2_ref, y_ref):
    del blk_exp_ref
    in_use = pl.program_id(0) < nused_ref[0]

    @pl.when(in_use)
    def _():
        a = _swiglu(x_ref[...], w1_ref[0], w3_ref[0])
        y_ref[...] = _dot(a.astype(BF16), w2_ref[0]).astype(y_ref.dtype)

    @pl.when(jnp.logical_not(in_use))
    def _():
        y_ref[...] = jnp.zeros(y_ref.shape, y_ref.dtype)


def _moe_ffn(xs, plan, nb, w1, w3, w2):
    _, _, _, blk_exp, n_used = plan
    grid_spec = pltpu.PrefetchScalarGridSpec(
        num_scalar_prefetch=2,
        grid=(nb,),
        in_specs=[pl.BlockSpec((MOE_BM, D_MODEL), lambda j, be, nu: (jnp.minimum(j, nu[0] - 1), 0)),
                  pl.BlockSpec((1, D_MODEL, EXPERT_FF), lambda j, be, nu: (be[j], 0, 0)),
                  pl.BlockSpec((1, D_MODEL, EXPERT_FF), lambda j, be, nu: (be[j], 0, 0)),
                  pl.BlockSpec((1, EXPERT_FF, D_MODEL), lambda j, be, nu: (be[j], 0, 0))],
        out_specs=pl.BlockSpec((MOE_BM, D_MODEL), lambda j, be, nu: (j, 0)),
    )
    return pl.pallas_call(
        _moe_ffn_kernel,
        grid_spec=grid_spec,
        out_shape=jax.ShapeDtypeStruct(xs.shape, BF16),
        compiler_params=_params("arbitrary"),
        name="moe_ffn",
    )(blk_exp, n_used, xs, w1, w3, w2)


def _moe_combine_kernel(start_ref, npass_ref, hb_ref, h_ref, gate_ref, slot_ref, ys_ref,
                        ws1_ref, ws3_ref, ws2_ref, g2_ref, b2_ref, y_ref, buf_ref, sem_ref):
    i = pl.program_id(0)
    n = pl.num_programs(0)
    E, R, GR = N_EXPERTS, MOE_R, MOE_GROUP * MOE_R
    half = i % 2

    def copy(tile, e, k, hf):
        src = pl.multiple_of(start_ref[tile * E + e] + k * R, MOE_ALIGN)
        return pltpu.make_async_copy(ys_ref.at[pl.ds(src, R), :],
                                     buf_ref.at[hf, pl.ds(pl.multiple_of(e * R, R), R), :], sem_ref.at[hf])

    def all_runs(fn):
        def body(e, c):
            fn(e)
            return c
        lax.fori_loop(0, E, body, 0)

    @pl.when(i == 0)
    def _():
        all_runs(lambda e: copy(0, e, 0, 0).start())

    @pl.when(i + 1 < n)
    def _():
        all_runs(lambda e: copy(i + 1, e, 0, 1 - half).start())

    def gathered(k):
        acc = jnp.zeros(y_ref.shape, F32)
        for g in range(E // MOE_GROUP):
            rows = slice(g * GR, (g + 1) * GR)
            hits = _chunk_hits(slot_ref, g, k)
            gate = jnp.concatenate(
                [jnp.sum(jnp.where(hit, gate_ref[e:e + 1, :], 0.0), axis=1, keepdims=True)
                 for hit, e in zip(hits, range(g * MOE_GROUP, (g + 1) * MOE_GROUP))], axis=0)
            onehot = jnp.concatenate([jnp.where(hit, 1.0, 0.0).astype(BF16) for hit in hits], axis=0)
            weighted = (buf_ref[half, rows, :].astype(F32) * gate).astype(BF16)
            acc = acc + _dot_tn(onehot, weighted)
        return acc

    all_runs(lambda e: copy(i, e, 0, half).wait())
    routed = gathered(0)

    def more(k, acc):
        all_runs(lambda e: copy(i, e, k, half).start())
        all_runs(lambda e: copy(i, e, k, half).wait())
        return acc + gathered(k)

    routed = lax.fori_loop(1, npass_ref[i], more, routed)
    x = hb_ref[...]
    shared = _dot(_swiglu(x, ws1_ref[...], ws3_ref[...]).astype(BF16), ws2_ref[...])
    y_ref[...] = _layer_norm(ALPHA * h_ref[...] + (routed + shared), g2_ref[...], b2_ref[...])


def _moe_combine(hb, h, gates_t, slots, ys, plan, mw, tt):
    t = h.shape[0]
    start, _, n_pass, _, _ = plan
    _, _, _, ws1, ws3, ws2, g2, b2 = mw
    weights = (ws1, ws3, ws2, g2, b2)
    row = lambda w: pl.BlockSpec((tt, w), lambda i, *_: (i, 0))
    col = pl.BlockSpec((N_EXPERTS, tt), lambda i, *_: (0, i))
    full = lambda a: pl.BlockSpec(a.shape, lambda i, *_: (0,) * a.ndim)
    grid_spec = pltpu.PrefetchScalarGridSpec(
        num_scalar_prefetch=2,
        grid=(t // tt,),
        in_specs=[row(D_MODEL), row(D_MODEL), col, col, pl.BlockSpec(memory_space=pl.ANY)]
        + [full(a) for a in weights],
        out_specs=row(D_MODEL),
        scratch_shapes=[pltpu.VMEM((2, N_EXPERTS * MOE_R, D_MODEL), BF16), pltpu.SemaphoreType.DMA((2,))],
    )
    return pl.pallas_call(
        _moe_combine_kernel,
        grid_spec=grid_spec,
        out_shape=jax.ShapeDtypeStruct((t, D_MODEL), F32),
        compiler_params=_params("arbitrary"),
        name="moe_combine",
    )(start, n_pass, hb, h, gates_t, slots, ys, *weights)


def _moe_weights(w_e1, w_e3, w_e2, w_s1, w_s3, w_s2, ln2_g, ln2_b):
    return (w_e1.astype(BF16), w_e3.astype(BF16), w_e2.astype(BF16), w_s1.astype(BF16), w_s3.astype(BF16),
            w_s2.astype(BF16), ln2_g.astype(F32)[None, :], ln2_b.astype(F32)[None, :])


def _finish_and_moe(x2d, ssd_y, nsa_y, gm, fw, mw, tt):
    t = x2d.shape[0]
    h, hb, gates_t, slots, run_len = _finish(x2d, ssd_y, nsa_y, gm, fw, tt)
    nb = _moe_blocks(t, tt)
    plan = _moe_plan(run_len[:, :, 0].astype(jnp.int32), nb)
    xs = _moe_dispatch(hb, slots, plan, nb, tt)
    ys = _moe_ffn(xs, plan, nb, mw[0], mw[1], mw[2])
    return _moe_combine(hb, h, gates_t, slots, ys, plan, mw, tt)


def _ssd_sample_kernel(xbc_ref, sc_ref, z_ref, sm_ref, st_ref, cw_ref, cb_ref, dtb_ref, alog_ref, dskip_ref,
                       nw_ref, e_ref, y_ref, sto_ref, xdt_t_ref, dec_t_ref, b_ref, c_ref, xs_ref, yt_ref):
    i = pl.program_id(0)
    db = xbc_ref.shape[0]
    G, N = SSD_GROUPS, SSD_STATE
    GW = SSD_HPG * SSD_HEAD_DIM

    @pl.when(i == 0)
    def _():
        conv = cb_ref[...] + xbc_ref[...] * cw_ref[SSD_CONV - 1:SSD_CONV, :]
        for k in range(SSD_CONV - 1):
            conv = conv + sc_ref[k] * cw_ref[k:k + 1, :]
        act = _silu(conv)
        xs = act[:, :SSD_D_INNER]
        xs_ref[...] = xs
        b_ref[...] = act[:, SSD_D_INNER:SSD_D_INNER + G * N].astype(BF16)
        c_ref[...] = act[:, SSD_D_INNER + G * N:].astype(BF16)
        dt = _softplus(sm_ref[...] + dtb_ref[...])
        dec = jnp.exp(dt * (-jnp.exp(alog_ref[...])))
        e_mat = e_ref[...]
        xdt_t_ref[...] = (xs * _dot_exact_rhs(dt, e_mat)).T.astype(BF16)
        dec_t_ref[...] = _dot_exact_rhs(dec, e_mat).T
        yt_ref[...] = jnp.zeros_like(yt_ref)

    is_row = lax.broadcasted_iota(jnp.int32, (db, N), 0) == i
    onehot = jnp.where(is_row, 1.0, 0.0).astype(BF16)
    is_lane = lax.broadcasted_iota(jnp.int32, (GW, db), 1) == i
    for g in range(G):
        rows = slice(g * GW, (g + 1) * GW)
        b_g = jnp.where(is_row, b_ref[:, g * N:(g + 1) * N], jnp.zeros((), BF16))
        contrib = _dot(xdt_t_ref[rows, :], b_g)
        decay = _dot_exact_rhs(dec_t_ref[rows, :], onehot)
        new = st_ref[0, rows, :] * decay + contrib
        sto_ref[0, rows, :] = new
        y_all = _dot_nt(new.astype(BF16), c_ref[:, g * N:(g + 1) * N])
        yt_ref[rows, :] += jnp.where(is_lane, y_all, 0.0)

    @pl.when(i == pl.num_programs(0) - 1)
    def _():
        y = yt_ref[...].T + dskip_ref[...] * xs_ref[...]
        y = y * _silu(z_ref[...])
        for g in range(G):
            sl = slice(g * GW, (g + 1) * GW)
            yg = y[:, sl]
            rs = lax.rsqrt(jnp.mean(yg * yg, axis=-1, keepdims=True) + NORM_EPS)
            y_ref[:, sl] = (yg * rs * nw_ref[:, sl]).astype(y_ref.dtype)


def _ssd_sample(xbc, state_conv, z, small, state_ssm, conv_w, conv_b, dt_bias, a_log, d_skip, norm_w):
    db = xbc.shape[0]
    pad = LANES - SSD_HEADS
    dtb = jnp.pad(dt_bias.astype(F32), (0, pad))[None, :]
    alog = jnp.pad(a_log.astype(F32), (0, pad))[None, :]
    dskip = jnp.repeat(d_skip.astype(F32), SSD_HEAD_DIM)[None, :]
    sc = state_conv.astype(F32).transpose(1, 0, 2)
    st = state_ssm.reshape(db, SSD_D_INNER, SSD_STATE)
    args = (xbc, sc, z, small, st, conv_w.astype(F32), conv_b.astype(F32)[None, :], dtb, alog, dskip,
            norm_w.astype(F32)[None, :], _head_expand_matrix())
    full = lambda a: pl.BlockSpec(a.shape, lambda i: (0,) * a.ndim)
    st_spec = pl.BlockSpec((1, SSD_D_INNER, SSD_STATE), lambda i: (i, 0, 0))
    y, st_new = pl.pallas_call(
        _ssd_sample_kernel,
        grid=(db,),
        in_specs=[full(a) for a in args[:4]] + [st_spec] + [full(a) for a in args[5:]],
        out_specs=[pl.BlockSpec((db, SSD_D_INNER), lambda i: (0, 0)), st_spec],
        out_shape=[jax.ShapeDtypeStruct((db, SSD_D_INNER), BF16),
                   jax.ShapeDtypeStruct((db, SSD_D_INNER, SSD_STATE), F32)],
        scratch_shapes=[pltpu.VMEM((SSD_D_INNER, db), BF16), pltpu.VMEM((SSD_D_INNER, db), F32),
                        pltpu.VMEM((db, SSD_GROUPS * SSD_STATE), BF16),
                        pltpu.VMEM((db, SSD_GROUPS * SSD_STATE), BF16),
                        pltpu.VMEM((db, SSD_D_INNER), F32), pltpu.VMEM((SSD_D_INNER, db), F32)],
        compiler_params=_params("arbitrary"),
        name="ssd_sample",
    )(*args)
    return y, st_new.reshape(state_ssm.shape)


def _nsa_sample_kernel(n_pages, *refs):
    pt_ref = refs[0]
    page_refs = refs[1:1 + n_pages]
    (win_ref, q_ref, kvn_ref, gate_ref, slope_ref, ovl_ref, w1ab_ref, w1flat_ref, pe_ref, b1_ref, w2_ref,
     b2_ref, o_ref, wino_ref, cmp_rows_ref, ks_ref, vs_ref, kw_ref, vw_ref, bsh_ref, kc_ref, vc_ref) = refs[1 + n_pages:]
    del pt_ref
    H, dh, G = NSA_HEADS, NSA_HEAD_DIM, NSA_KV_HEADS
    past = n_pages * PAGE_SIZE
    ncb = past // CMP_STRIDE
    nsb = ovl_ref.shape[1]
    wbuf = win_ref.shape[3]
    tiles = KVW // LANES

    spp = PAGE_SIZE // CMP_STRIDE
    src_row = lax.broadcasted_iota(jnp.int32, (PAGE_SIZE, PAGE_SIZE), 1)
    regroup = jnp.where(lax.broadcasted_iota(jnp.int32, (PAGE_SIZE, PAGE_SIZE), 0)
                        == spp * (src_row % CMP_STRIDE) + src_row // CMP_STRIDE, 1.0, 0.0).astype(BF16)
    for p in range(n_pages):
        r = slice(p * PAGE_SIZE, (p + 1) * PAGE_SIZE)
        for w in range(2):
            rows_wp = _dot_nt(regroup, page_refs[p][0, w].astype(BF16))
            for t in range(tiles):
                for l in range(CMP_STRIDE):
                    cmp_rows_ref[w * tiles + t, l, p * spp:(p + 1) * spp, :] = (
                        rows_wp[l * spp:(l + 1) * spp, t * LANES:(t + 1) * LANES])
        ks_ref[:, r] = page_refs[p][0, 2].astype(BF16)
        vs_ref[:, r] = page_refs[p][0, 3].astype(BF16)
    new8 = jnp.broadcast_to(kvn_ref[0], (8, 6 * KVW))
    first = lax.broadcasted_iota(jnp.int32, (KVW, LANES), 1) == 0

    def new_col(w):
        return new8[:, w * KVW:(w + 1) * KVW].T[:, 0:1]

    put = lambda w: jnp.where(first, new_col(w), 0.0).astype(BF16)
    ks_ref[:, past:past + LANES] = put(2)
    vs_ref[:, past:past + LANES] = put(3)
    kw_ref[:, 0:wbuf] = win_ref[0, 0].astype(BF16)
    vw_ref[:, 0:wbuf] = win_ref[0, 1].astype(BF16)
    kw_ref[:, wbuf:wbuf + LANES] = put(4)
    vw_ref[:, wbuf:wbuf + LANES] = put(5)
    last = lax.broadcasted_iota(jnp.int32, (KVW, wbuf), 1) == wbuf - 1
    for w in range(2):
        wino_ref[0, w] = jnp.where(last, new_col(4 + w), pltpu.roll(win_ref[0, w], wbuf - 1, 1))

    def compress(w, dst):
        for t in range(tiles):
            def load_strip(l, t=t):
                return cmp_rows_ref[w * tiles + t, l]

            def store(g, val, t=t):
                c0 = (t * GPT + g) * dh
                dst[:, c0:c0 + dh] = val.astype(BF16)

            sub = lambda ref: ref.at[pl.ds(w, 1)]
            _compress_body(load_strip, ncb, sub(w1ab_ref), sub(w1flat_ref), sub(pe_ref), sub(b1_ref), sub(w2_ref),
                           sub(b2_ref), bsh_ref.at[w], store)

    compress(0, kc_ref)

    head_r = lax.broadcasted_iota(jnp.int32, (H, KVW), 0) // NSA_HPG
    lane_g = lax.broadcasted_iota(jnp.int32, (H, KVW), 1) // dh
    diag = head_r == lane_g
    q16 = q_ref[0] * (dh ** -0.5)
    q_bd = jnp.where(diag, jnp.concatenate([q16] * G, axis=1), 0.0).astype(BF16)
    slopes = slope_ref[...]
    gates = _sigmoid(gate_ref[0])

    def fold(o):
        o = jnp.where(diag, o, 0.0)
        return sum(o[:, g * dh:(g + 1) * dh] for g in range(G))

    d_c = past - (lax.broadcasted_iota(jnp.int32, (H, ncb), 1) * CMP_STRIDE + (CMP_BLOCK - 1))
    p_c = _masked_softmax_rows(_dot_nt(q_bd, kc_ref[...]) - slopes * d_c.astype(F32), d_c >= 0)
    compress(1, vc_ref)
    o_c = fold(_dot(p_c.astype(BF16), vc_ref[...]))

    p_grp = jnp.concatenate(
        [jnp.sum(p_c[g * NSA_HPG:(g + 1) * NSA_HPG], axis=0, keepdims=True) for g in range(G)]
        + [jnp.zeros((8 - G, ncb), F32)], axis=0)
    imp = sum(_dot(part, ovl_ref[...]) for part in _split3(p_grp))
    m_l = lax.broadcasted_iota(jnp.int32, (8, nsb), 1)
    valid = m_l * SLC_BLOCK <= past
    cur = past // SLC_BLOCK
    forced = (m_l == 0) | (m_l == cur) | (m_l == cur - 1)
    v = jnp.where(valid, jnp.where(forced, jnp.inf, imp), -jnp.inf)
    v_t = v.T
    mp_i = lax.broadcasted_iota(jnp.int32, (nsb, nsb), 0)
    m_i = lax.broadcasted_iota(jnp.int32, (nsb, nsb), 1)
    sel_rows = []
    for g in range(G):
        v_col = jnp.broadcast_to(v_t[:, g:g + 1], (nsb, nsb))
        v_row = jnp.broadcast_to(v[g:g + 1, :], (nsb, nsb))
        ahead = (v_col > v_row) | ((v_col == v_row) & (mp_i < m_i))
        rank = jnp.sum(jnp.where(ahead, 1.0, 0.0), axis=0, keepdims=True)
        sel = jnp.where((rank < SLC_TOPK) & valid[0:1, :], 1.0, 0.0)
        sel_rows.append(jnp.broadcast_to(sel, (NSA_HPG, nsb)))
    sel_h = jnp.concatenate(sel_rows, axis=0).astype(BF16)

    nk = past + LANES
    expand = jnp.where(lax.broadcasted_iota(jnp.int32, (nsb, nk), 0)
                       == lax.broadcasted_iota(jnp.int32, (nsb, nk), 1) // SLC_BLOCK, 1.0, 0.0).astype(BF16)
    d_s = past - lax.broadcasted_iota(jnp.int32, (H, nk), 1)
    ok = (_dot(sel_h, expand) > 0.5) & (d_s >= 0)
    p_s = _masked_softmax_rows(_dot(q_bd, ks_ref[...]) - slopes * d_s.astype(F32), ok)
    o_s = fold(_dot_nt(p_s.astype(BF16), vs_ref[...]))

    nw = wbuf + LANES
    d_w = wbuf - lax.broadcasted_iota(jnp.int32, (H, nw), 1)
    p_w = _masked_softmax_rows(_dot(q_bd, kw_ref[...]) - slopes * d_w.astype(F32), (d_w >= 0) & (d_w < WINDOW))
    o_w = fold(_dot_nt(p_w.astype(BF16), vw_ref[...]))

    o_ref[0] = (gates[:, 0:1] * o_c + gates[:, 1:2] * o_s + gates[:, 2:3] * o_w).astype(o_ref.dtype)


def _overlap_sample(ncb, nsb):
    i = np.arange(ncb)[:, None] * CMP_STRIDE
    m = np.arange(nsb)[None, :] * SLC_BLOCK
    ok = (i < m + SLC_BLOCK) & (i + CMP_BLOCK > m) & (np.arange(ncb)[:, None] < ncb - 1)
    return ok.astype(np.float32)


def _nsa_sample(q, kv_new, small, cache_kv_paged, page_table, cache_kv_win, cw):
    db = q.shape[0]
    H, dh, G = NSA_HEADS, NSA_HEAD_DIM, NSA_KV_HEADS
    n_pages = page_table.shape[1]
    past = n_pages * PAGE_SIZE
    wbuf = cache_kv_win.shape[1]
    ncb = past // CMP_STRIDE
    nsb = LANES * pl.cdiv(pl.cdiv(past + 1, SLC_BLOCK), LANES)
    pages = cache_kv_paged.transpose(0, 2, 3, 4, 1).reshape(cache_kv_paged.shape[0], 4, KVW, PAGE_SIZE)
    win = cache_kv_win.transpose(0, 2, 3, 4, 1).reshape(db, 2, KVW, wbuf)
    gates_raw = small[:, SSD_HEADS:SSD_HEADS + 3 * H].reshape(db, H, 3)
    slopes = jnp.asarray(_alibi_slopes())[:, None]
    ovl = jnp.asarray(_overlap_sample(ncb, nsb), BF16)
    consts = (slopes, ovl) + tuple(cw)
    per_seq = lambda shape: pl.BlockSpec((1,) + shape, lambda i, pt: (i,) + (0,) * len(shape))
    full = lambda a: pl.BlockSpec(a.shape, lambda i, pt: (0,) * a.ndim)
    page_spec = lambda p: pl.BlockSpec((1, 4, KVW, PAGE_SIZE), lambda i, pt: (pt[i, p], 0, 0, 0))
    grid_spec = pltpu.PrefetchScalarGridSpec(
        num_scalar_prefetch=1,
        grid=(db,),
        in_specs=[page_spec(p) for p in range(n_pages)]
        + [per_seq((2, KVW, wbuf)), per_seq((H, dh)), per_seq((1, 6 * KVW)), per_seq((H, 3))]
        + [full(a) for a in consts],
        out_specs=[per_seq((H, dh)), per_seq((2, KVW, wbuf))],
        scratch_shapes=[pltpu.VMEM((2 * KVW // LANES, CMP_STRIDE, ncb, LANES), F32),
                        pltpu.VMEM((KVW, past + LANES), BF16), pltpu.VMEM((KVW, past + LANES), BF16),
                        pltpu.VMEM((KVW, wbuf + LANES), BF16), pltpu.VMEM((KVW, wbuf + LANES), BF16),
                        pltpu.VMEM((2, ncb + 8, CMP_HIDDEN), F32),
                        pltpu.VMEM((ncb, KVW), BF16), pltpu.VMEM((ncb, KVW), BF16)],
    )
    out, win_new = pl.pallas_call(
        functools.partial(_nsa_sample_kernel, n_pages),
        grid_spec=grid_spec,
        out_shape=[jax.ShapeDtypeStruct((db, H, dh), BF16), jax.ShapeDtypeStruct((db, 2, KVW, wbuf), cache_kv_win.dtype)],
        compiler_params=_params("arbitrary"),
        name="nsa_sample",
    )(page_table.astype(jnp.int32), *([pages] * n_pages), win, q.reshape(db, H, dh), kv_new.reshape(db, 1, 6 * KVW),
      gates_raw, *consts)
    win_new = win_new.reshape(db, 2, G, dh, wbuf).transpose(0, 4, 1, 2, 3)
    return out.reshape(db, H * dh), win_new


def kernel(x_prompt, x_sample, cache_kv_paged, cache_kv_win, state_ssm, state_conv, page_table, w_in, b_in, conv_w, conv_b, dt_bias, a_log, d_skip, ssd_norm_w, cmp_w1, cmp_b1, cmp_w2, cmp_b2, cmp_pe, w_ssd_down, w_nsa_down, w_out, b_out, ln1_g, ln1_b, w_router, b_router, w_e1, w_e3, w_e2, w_s1, w_s3, w_s2, ln2_g, ln2_b):
    b, L, _ = x_prompt.shape
    db = x_sample.shape[0]
    G, dh = NSA_KV_HEADS, NSA_HEAD_DIM
    wbuf = cache_kv_win.shape[1]
    wp, bp = _pack_w_in(w_in, b_in)
    cw = _compress_weights(cmp_w1, cmp_b1, cmp_w2, cmp_b2, cmp_pe)
    fw = _finish_weights(w_ssd_down, w_nsa_down, w_out, b_out, ln1_g, ln1_b, w_router, b_router)
    mw = _moe_weights(w_e1, w_e3, w_e2, w_s1, w_s3, w_s2, ln2_g, ln2_b)
    ssd_w = (conv_w, conv_b, dt_bias, a_log, d_skip, ssd_norm_w)
    tm = min(256, b * L)
    per_seq = lambda a: a.reshape(b, L, a.shape[-1])

    xp = x_prompt.reshape(b * L, D_MODEL)
    z, xbc, q, kv, gm, small = _in_proj(xp, wp, bp, tm)
    ssd_y, st_t = _ssd_prompt(per_seq(xbc), per_seq(z), per_seq(small), *ssd_w)
    kv_t = _kv_channel_major(x_prompt, w_in, b_in, min(512, L))
    nsa_y = _nsa_prompt_branch(per_seq(q), per_seq(kv), kv_t, per_seq(small), cw)
    yp = _finish_and_moe(xp, ssd_y.reshape(b * L, -1), nsa_y.reshape(b * L, -1), gm, fw, mw, tm).reshape(b, L, D_MODEL)
    kv6_t = kv_t.reshape(b, 6, G, dh, L)
    kv_rows_p = kv6_t[:, :4].transpose(0, 4, 1, 2, 3)
    win_p = kv6_t[:, 4:, :, :, L - wbuf:].transpose(0, 4, 1, 2, 3)
    ssm_p = st_t.reshape(b, SSD_STATE, SSD_HEADS, SSD_HEAD_DIM).transpose(0, 2, 3, 1).astype(state_ssm.dtype)
    conv_p = per_seq(xbc)[:, L - (SSD_CONV - 1):]

    xs = x_sample.reshape(db, D_MODEL)
    z, xbc, q, kv, gm, small = _in_proj(xs, wp, bp, db)
    ssd_y, ssm_s = _ssd_sample(xbc, state_conv, z, small, state_ssm, *ssd_w)
    nsa_y, win_s = _nsa_sample(q, kv, small, cache_kv_paged, page_table, cache_kv_win, cw)
    ys = _finish_and_moe(xs, ssd_y, nsa_y, gm, fw, mw, db).reshape(db, 1, D_MODEL)
    kv6 = kv.reshape(db, 1, 6, G, dh)
    kv_rows_s = kv6[:, :, :4]
    conv_s = jnp.concatenate([state_conv[:, 1:].astype(xbc.dtype), xbc[:, None, :]], axis=1)
    return (yp, ys, kv_rows_p, kv_rows_s, win_p, win_s, ssm_p, ssm_s, conv_p, conv_s)
```

```python
import functools

import numpy as np
import jax
import jax.numpy as jnp
from jax import lax
from jax.experimental import pallas as pl
from jax.experimental.pallas import tpu as pltpu

F32 = jnp.float32
BF16 = jnp.bfloat16

D_MODEL = 1024
PAGE_SIZE = 128
SSD_D_INNER = 2048
SSD_HEAD_DIM = 64
SSD_HEADS = 32
SSD_GROUPS = 4
SSD_HPG = 8
SSD_STATE = 128
SSD_CONV = 4
SSD_CHUNK = 128
SSD_CONV_CH = SSD_D_INNER + 2 * SSD_GROUPS * SSD_STATE
NSA_HEAD_DIM = 64
NSA_HEADS = 16
NSA_KV_HEADS = 4
NSA_HPG = 4
CMP_BLOCK = 32
CMP_STRIDE = 16
CMP_HIDDEN = 128
SLC_BLOCK = 64
SLC_TOPK = 16
WINDOW = 512
NSA_NQ = 128
N_EXPERTS = 64
TOP_K = 8
N_ROUTE_GROUPS = 8
TOPK_ROUTE_GROUPS = 4
EXPERT_FF = 256
ROUTE_SCALE = 2.5
ALPHA = 2.0 ** 0.25
NORM_EPS = 1e-5
IN_SPLITS = (SSD_D_INNER, SSD_CONV_CH, SSD_HEADS, NSA_HEADS * NSA_HEAD_DIM,
             6 * NSA_KV_HEADS * NSA_HEAD_DIM, 3 * NSA_HEADS, 2 * D_MODEL)
IN_OFFSETS = tuple(int(v) for v in np.cumsum(IN_SPLITS)[:-1])

LANES = 128
VMEM_LIMIT = 56 * 1024 * 1024

SEG_Z = (0, 2048)
SEG_XBC = (2048, 3072)
SEG_Q = (5120, 1024)
SEG_KV = (6144, 1536)
SEG_GM = (7680, 2048)
SEG_SMALL = (9728, 128)
N_PACKED = 9856


def _params(*sem):
    return pltpu.CompilerParams(dimension_semantics=sem, vmem_limit_bytes=VMEM_LIMIT)


def _silu(x):
    return x * (1.0 / (1.0 + jnp.exp(-x)))


def _sigmoid(x):
    return 1.0 / (1.0 + jnp.exp(-x))


def _softplus(x):
    return jnp.maximum(x, 0.0) + jnp.log(1.0 + jnp.exp(-jnp.abs(x)))


def _split2(x):
    hi = x.astype(BF16)
    lo = (x - hi.astype(F32)).astype(BF16)
    return hi, lo


def _split3(x):
    hi = x.astype(BF16)
    r = x - hi.astype(F32)
    mid = r.astype(BF16)
    lo = (r - mid.astype(F32)).astype(BF16)
    return hi, mid, lo


def _dot(a, b):
    return jnp.dot(a, b, preferred_element_type=F32)


def _dot_nt(a, b):
    return lax.dot_general(a, b, (((1,), (1,)), ((), ())), preferred_element_type=F32)


def _dot_tn(a, b):
    return lax.dot_general(a, b, (((0,), (0,)), ((), ())), preferred_element_type=F32)


def _dot_exact_rhs(x, sel):
    a, b, c = _split3(x)
    return _dot(a, sel) + _dot(b, sel) + _dot(c, sel)


def _pack_w_in(w_in, b_in):
    def pack(m):
        z, xbc, dt, q, kv, gn, gm = jnp.split(m, list(IN_OFFSETS), axis=-1)
        pad = jnp.zeros(m.shape[:-1] + (LANES - SSD_HEADS - 3 * NSA_HEADS,), m.dtype)
        return jnp.concatenate([z, xbc, q, kv, gm, dt, gn, pad], axis=-1)
    return pack(w_in).astype(BF16), pack(b_in[None, :])


def _in_proj_kernel(x_ref, w_ref, b_ref, z_ref, xbc_ref, q_ref, kv_ref, gm_ref, sm_ref):
    x = x_ref[...].astype(BF16)
    for ref, (off, width) in ((z_ref, SEG_Z), (xbc_ref, SEG_XBC), (q_ref, SEG_Q), (kv_ref, SEG_KV),
                              (gm_ref, SEG_GM), (sm_ref, SEG_SMALL)):
        ref[...] = _dot(x, w_ref[:, off:off + width]) + b_ref[:, off:off + width]


def _in_proj(x2d, w_packed, b_packed, tm):
    t = x2d.shape[0]
    segs = (SEG_Z, SEG_XBC, SEG_Q, SEG_KV, SEG_GM, SEG_SMALL)
    return pl.pallas_call(
        _in_proj_kernel,
        grid=(t // tm,),
        in_specs=[pl.BlockSpec((tm, D_MODEL), lambda i: (i, 0)),
                  pl.BlockSpec((D_MODEL, N_PACKED), lambda i: (0, 0), pipeline_mode=pl.Buffered(1)),
                  pl.BlockSpec((1, N_PACKED), lambda i: (0, 0))],
        out_specs=[pl.BlockSpec((tm, w), lambda i: (i, 0)) for _, w in segs],
        out_shape=[jax.ShapeDtypeStruct((t, w), F32) for _, w in segs],
        compiler_params=_params("arbitrary"),
        name="in_proj",
    )(x2d, w_packed, b_packed)


def _kv_channel_major_kernel(x_ref, wt_ref, bt_ref, o_ref):
    o_ref[0] = _dot_nt(wt_ref[...], x_ref[0].astype(BF16)) + bt_ref[...]


def _kv_channel_major(x, w_in, b_in, tm):
    b, L, _ = x.shape
    lo, n = IN_OFFSETS[3], IN_SPLITS[4]
    wt = w_in[:, lo:lo + n].T.astype(BF16)
    bt = b_in[lo:lo + n].astype(F32)[:, None]
    return pl.pallas_call(
        _kv_channel_major_kernel,
        grid=(b, L // tm),
        in_specs=[pl.BlockSpec((1, tm, D_MODEL), lambda i, j: (i, j, 0)),
                  pl.BlockSpec((n, D_MODEL), lambda i, j: (0, 0)),
                  pl.BlockSpec((n, 1), lambda i, j: (0, 0))],
        out_specs=pl.BlockSpec((1, n, tm), lambda i, j: (i, 0, j)),
        out_shape=jax.ShapeDtypeStruct((b, n, L), F32),
        compiler_params=_params("arbitrary", "arbitrary"),
        name="kv_channel_major",
    )(x, wt, bt)


def _head_expand_matrix():
    h = np.arange(LANES)[:, None]
    c = np.arange(SSD_D_INNER)[None, :] // SSD_HEAD_DIM
    return jnp.asarray((h == c).astype(np.float32), BF16)


def _ssd_prompt_kernel(xbc_ref, z_ref, sm_ref, cw_ref, cb_ref, dtb_ref, alog_ref, dskip_ref, nw_ref, e_ref,
                       y_ref, st_ref, xh_ref, state_ref, ybuf_ref):
    c = pl.program_id(1)
    L = SSD_CHUNK
    G, N, P = SSD_GROUPS, SSD_STATE, SSD_HEAD_DIM
    GW = SSD_HPG * P

    @pl.when(c == 0)
    def _():
        xh_ref[0:8, :] = jnp.zeros((8, SSD_CONV_CH), F32)
        state_ref[...] = jnp.zeros_like(state_ref)

    xh_ref[8:8 + L, :] = xbc_ref[0]
    conv = cb_ref[...] + xh_ref[5:5 + L, :] * cw_ref[0:1, :]
    for k in range(1, SSD_CONV):
        conv = conv + xh_ref[5 + k:5 + k + L, :] * cw_ref[k:k + 1, :]
    xh_ref[0:8, :] = xh_ref[L:L + 8, :]
    act = _silu(conv)
    xs = act[:, :SSD_D_INNER]
    bm = act[:, SSD_D_INNER:SSD_D_INNER + G * N].astype(BF16)
    cm = act[:, SSD_D_INNER + G * N:]

    dt = _softplus(sm_ref[0] + dtb_ref[...])
    da = dt * (-jnp.exp(alog_ref[...]))
    row_i = lax.broadcasted_iota(jnp.int32, (L, L), 0)
    col_i = lax.broadcasted_iota(jnp.int32, (L, L), 1)
    tri = row_i >= col_i
    a_cs = _dot_exact_rhs_left(tri, da)
    a_cs_t = a_cs.T
    dt_t = dt.T
    a_last = a_cs[L - 1:L, :]
    w_end = dt * jnp.exp(a_last - a_cs)
    chunk_decay = jnp.broadcast_to(jnp.exp(a_last), (8, LANES))
    e_mat = e_ref[...]
    w_exp = _dot_exact_rhs(w_end, e_mat)
    dec_exp = _dot_exact_rhs(chunk_decay, e_mat)[0:1, :]
    xd = (xs * w_exp).astype(BF16)

    for g in range(G):
        cg = cm[:, g * N:(g + 1) * N]
        bg = bm[:, g * N:(g + 1) * N]
        cb = _dot_nt(cg.astype(BF16), bg)
        for j in range(SSD_HPG):
            h = g * SSD_HPG + j
            col = jnp.broadcast_to(a_cs[:, h:h + 1], (L, L))
            row = a_cs_t[h:h + 1, :]
            lm = jnp.where(tri, jnp.exp(col - row), 0.0)
            m = (cb * lm * dt_t[h:h + 1, :]).astype(BF16)
            eac = (jnp.exp(col) * cg).astype(BF16)
            lhs = jnp.concatenate([m, eac], axis=1)
            rhs = jnp.concatenate([xs[:, h * P:(h + 1) * P].astype(BF16),
                                   state_ref[:, h * P:(h + 1) * P].astype(BF16)], axis=0)
            ybuf_ref[:, h * P:(h + 1) * P] = _dot(lhs, rhs)
        sl = slice(g * GW, (g + 1) * GW)
        state_ref[:, sl] = state_ref[:, sl] * dec_exp[:, sl] + _dot_tn(bg, xd[:, sl])

    y = ybuf_ref[...] + dskip_ref[...] * xs
    y = y * _silu(z_ref[0])
    for g in range(G):
        sl = slice(g * GW, (g + 1) * GW)
        yg = y[:, sl]
        rs = lax.rsqrt(jnp.mean(yg * yg, axis=-1, keepdims=True) + NORM_EPS)
        y_ref[0, :, sl] = (yg * rs * nw_ref[:, sl]).astype(y_ref.dtype)

    @pl.when(c == pl.num_programs(1) - 1)
    def _():
        st_ref[0] = state_ref[...]


def _dot_exact_rhs_left(mask, x):
    sel = jnp.where(mask, 1.0, 0.0).astype(BF16)
    a, b, c = _split3(x)
    return _dot(sel, a) + _dot(sel, b) + _dot(sel, c)


def _ssd_prompt(xbc, z, small, conv_w, conv_b, dt_bias, a_log, d_skip, norm_w):
    b, L, _ = xbc.shape
    nc = L // SSD_CHUNK
    pad = LANES - SSD_HEADS
    dtb = jnp.pad(dt_bias.astype(F32), (0, pad))[None, :]
    alog = jnp.pad(a_log.astype(F32), (0, pad))[None, :]
    dskip = jnp.repeat(d_skip.astype(F32), SSD_HEAD_DIM)[None, :]
    full = lambda shape: pl.BlockSpec(shape, lambda i, j: (0,) * len(shape))
    y, st = pl.pallas_call(
        _ssd_prompt_kernel,
        grid=(b, nc),
        in_specs=[pl.BlockSpec((1, SSD_CHUNK, SSD_CONV_CH), lambda i, j: (i, j, 0)),
                  pl.BlockSpec((1, SSD_CHUNK, SSD_D_INNER), lambda i, j: (i, j, 0)),
                  pl.BlockSpec((1, SSD_CHUNK, LANES), lambda i, j: (i, j, 0)),
                  full((SSD_CONV, SSD_CONV_CH)), full((1, SSD_CONV_CH)), full((1, LANES)), full((1, LANES)),
                  full((1, SSD_D_INNER)), full((1, SSD_D_INNER)), full((LANES, SSD_D_INNER))],
        out_specs=[pl.BlockSpec((1, SSD_CHUNK, SSD_D_INNER), lambda i, j: (i, j, 0)),
                   pl.BlockSpec((1, SSD_STATE, SSD_D_INNER), lambda i, j: (i, 0, 0))],
        out_shape=[jax.ShapeDtypeStruct((b, L, SSD_D_INNER), BF16),
                   jax.ShapeDtypeStruct((b, SSD_STATE, SSD_D_INNER), F32)],
        scratch_shapes=[pltpu.VMEM((SSD_CHUNK + 8, SSD_CONV_CH), F32),
                        pltpu.VMEM((SSD_STATE, SSD_D_INNER), F32),
                        pltpu.VMEM((SSD_CHUNK, SSD_D_INNER), F32)],
        compiler_params=_params("arbitrary", "arbitrary"),
        name="ssd_prompt",
    )(xbc, z, small, conv_w.astype(F32), conv_b.astype(F32)[None, :], dtb, alog, dskip,
      norm_w.astype(F32)[None, :], _head_expand_matrix())
    return y, st


HALF = CMP_BLOCK // 2
KVW = NSA_KV_HEADS * NSA_HEAD_DIM
ROWS_PER_DOT = 2


def _compress_weights(cmp_w1, cmp_b1, cmp_w2, cmp_b2, cmp_pe):
    w1 = cmp_w1.astype(BF16)
    w1ab = jnp.concatenate([w1[:, :HALF], w1[:, HALF:]], axis=-1)
    zero = jnp.zeros_like(w1ab)
    per_head = [jnp.stack([w1ab if h == g else zero for h in range(GPT)], axis=2) for g in range(GPT)]
    w1ab = jnp.stack(per_head, axis=1).reshape(2, GPT, HALF // ROWS_PER_DOT, ROWS_PER_DOT * LANES, 2 * CMP_HIDDEN)
    w1flat = w1.reshape(2, CMP_BLOCK * NSA_HEAD_DIM, CMP_HIDDEN)
    pe8 = jnp.broadcast_to(cmp_pe.reshape(2, 1, CMP_BLOCK * NSA_HEAD_DIM), (2, 8, CMP_BLOCK * NSA_HEAD_DIM))
    return (w1ab, w1flat, pe8.astype(F32), cmp_b1.astype(F32)[:, None, :], cmp_w2.astype(BF16),
            cmp_b2.astype(F32)[:, None, :])


GPT = LANES // NSA_HEAD_DIM


def _compress_body(load_strip, ns, w1ab_ref, w1flat_ref, pe_ref, b1_ref, w2_ref, b2_ref, bsh_ref, store):
    accs = [jnp.zeros((ns, 2 * CMP_HIDDEN), F32) for _ in range(GPT)]
    for c in range(HALF // ROWS_PER_DOT):
        lhs = jnp.concatenate([load_strip(c * ROWS_PER_DOT + i).astype(BF16) for i in range(ROWS_PER_DOT)], axis=1)
        for g in range(GPT):
            accs[g] = accs[g] + _dot(lhs, w1ab_ref[0, g, c])
    cpe = _dot(pe_ref[0].astype(BF16), w1flat_ref[0])[0:1, :] + b1_ref[0]
    bsh_ref[ns:ns + 8, :] = jnp.zeros((8, CMP_HIDDEN), F32)
    for g in range(GPT):
        bsh_ref[0:ns, :] = accs[g][:, CMP_HIDDEN:]
        hid = _silu(accs[g][:, :CMP_HIDDEN] + bsh_ref[1:ns + 1, :] + cpe)
        store(g, _dot(hid.astype(BF16), w2_ref[0]) + b2_ref[0])


def _compress_prompt_kernel(kv_ref, w1ab_ref, w1flat_ref, pe_ref, b1_ref, w2_ref, b2_ref, out_ref, bsh_ref):
    ns = out_ref.shape[3]

    def load_strip(l):
        return kv_ref[0, pl.ds(l, ns, stride=CMP_STRIDE), :]

    def store(g, val):
        out_ref[0, 0, g] = val

    _compress_body(load_strip, ns, w1ab_ref, w1flat_ref, pe_ref, b1_ref, w2_ref, b2_ref, bsh_ref, store)


def _compress_prompt(kv, cw):
    b, L, _ = kv.shape
    ns = L // CMP_STRIDE
    w1ab, w1flat, pe8, b1, w2, b2 = cw
    tiles = KVW // LANES
    per_which = lambda shape: pl.BlockSpec((1,) + shape, lambda i, w, t: (w,) + (0,) * len(shape))
    return pl.pallas_call(
        _compress_prompt_kernel,
        grid=(b, 2, tiles),
        in_specs=[pl.BlockSpec((1, L, LANES), lambda i, w, t: (i, 0, w * tiles + t)),
                  per_which((GPT, HALF // ROWS_PER_DOT, ROWS_PER_DOT * LANES, 2 * CMP_HIDDEN)),
                  per_which((CMP_BLOCK * NSA_HEAD_DIM, CMP_HIDDEN)),
                  per_which((8, CMP_BLOCK * NSA_HEAD_DIM)),
                  per_which((1, CMP_HIDDEN)),
                  per_which((CMP_HIDDEN, NSA_HEAD_DIM)),
                  per_which((1, NSA_HEAD_DIM))],
        out_specs=pl.BlockSpec((1, 1, GPT, ns, NSA_HEAD_DIM), lambda i, w, t: (i, w, t, 0, 0)),
        out_shape=jax.ShapeDtypeStruct((b, 2, NSA_KV_HEADS, ns, NSA_HEAD_DIM), F32),
        scratch_shapes=[pltpu.VMEM((ns + 8, CMP_HIDDEN), F32)],
        compiler_params=_params("arbitrary", "arbitrary", "arbitrary"),
        name="compress_prompt",
    )(kv, w1ab, w1flat, pe8, b1, w2, b2)


NEG = -1e30
SEL_TILE = 512
WIN_KEYS = WINDOW + NSA_NQ
WIN_CHUNK = 128


def _alibi_slopes():
    h = np.arange(1, NSA_HEADS + 1, dtype=np.float32)
    return (2.0 ** (-8.0 * h / NSA_HEADS)).astype(np.float32)


def _overlap_t(ncb, ns):
    i = np.arange(ncb)[None, :] * CMP_STRIDE
    m = np.arange(ns)[:, None] * SLC_BLOCK
    return ((i < m + SLC_BLOCK) & (i + CMP_BLOCK > m)).astype(np.float32)


def _masked_softmax_rows(s, mask):
    sm = jnp.where(mask, s, NEG)
    mx = jnp.max(sm, axis=-1, keepdims=True)
    p = jnp.where(mask, jnp.exp(sm - mx), 0.0)
    return p / jnp.maximum(jnp.sum(p, axis=-1, keepdims=True), 1e-30)


def _select_blocks_t(imp_t, qpos_row, v_ref, n_live):
    ns, nq = imp_t.shape
    m_i = lax.broadcasted_iota(jnp.int32, (ns, nq), 0)
    valid = m_i * SLC_BLOCK <= qpos_row
    cur = qpos_row // SLC_BLOCK
    forced = (m_i == 0) | (m_i == cur) | (m_i == cur - 1)
    del n_live
    bits = pltpu.bitcast(jnp.where(forced, jnp.inf, imp_t), jnp.int32)
    int_min = jnp.int32(-2 ** 31)
    key = jnp.where(valid, jnp.where(bits < 0, bits ^ jnp.int32(2 ** 31 - 1), bits), int_min)
    v_ref[...] = pltpu.bitcast(key, F32)
    k = float(SLC_TOPK)

    def count_ge(t):
        terms = [jnp.where(pltpu.bitcast(v_ref[8 * s:8 * s + 8, :], jnp.int32) >= t, 1.0, 0.0)
                 for s in range(ns // 8)]
        while len(terms) > 1:
            terms = [a + b for a, b in zip(terms[0::2], terms[1::2])] + terms[len(terms) & ~1:]
        part = terms[0]
        for shift in (4, 2, 1):
            part = part + pltpu.roll(part, shift, 0)
        return part

    def step(i, t):
        cand = t + (jnp.int32(1) << (30 - i))
        return jnp.where(count_ge(cand) >= k, cand, t)

    zero = jnp.zeros((8, nq), jnp.int32)
    t0 = jnp.where(count_ge(zero) >= k, zero, int_min)
    thr = lax.fori_loop(0, 31, step, t0)[0:1, :]
    above = jnp.where(key > thr, 1.0, 0.0)
    tied = jnp.where(key == thr, 1.0, 0.0)
    room = k - jnp.sum(above, axis=0, keepdims=True)
    lower = jnp.where(lax.broadcasted_iota(jnp.int32, (ns, ns), 1) < lax.broadcasted_iota(jnp.int32, (ns, ns), 0),
                      1.0, 0.0).astype(BF16)
    ties_before = _dot(lower, tied.astype(BF16))
    chosen = jnp.where(above > 0.0, 1.0, jnp.where(ties_before < room, tied, 0.0))
    return jnp.where(valid, chosen, 0.0)


BIG = 2.0 ** 20
POS_PERIOD = SEL_TILE


def _nsa_prompt_kernel(slopes_ref, q_ref, gate_ref, qpc_ref, kct_ref, vc_ref, kst_ref, va_ref, kwt_ref, vwp_ref,
                       pos_ref, blk_ref, ovl_ref, o_ref, v_ref, kaug_ref, kwaug_ref, m_ref, acc_ref,
                       s0_ref, s1_ref, flag_ref):
    g = pl.program_id(1)
    qi = pl.program_id(2)
    nq, dh, J = NSA_NQ, NSA_HEAD_DIM, NSA_HPG
    ns, ncb = ovl_ref.shape
    L = kst_ref.shape[2]
    start = qi * nq
    slopes = [slopes_ref[g * J + j] for j in range(J)]
    rows = [slice(j * nq, (j + 1) * nq) for j in range(J)]

    @pl.when(qi == 0)
    def _():
        kaug_ref[0:dh, :] = kst_ref[0].astype(BF16)
        kaug_ref[dh:LANES, :] = pos_ref[...]
        kaug_ref[LANES:, :] = blk_ref[...]
        kwaug_ref[0:dh, 0:WINDOW] = jnp.zeros((dh, WINDOW), BF16)
        kwaug_ref[dh:LANES, 0:WINDOW] = pos_ref[:, 0:WINDOW]
        kwaug_ref[0:dh, WINDOW:] = kwt_ref[0].astype(BF16)
        kwaug_ref[dh:LANES, WINDOW:] = pos_ref[...]

    m_ref[...] = jnp.full(m_ref.shape, NEG, F32)
    acc_ref[...] = jnp.zeros(acc_ref.shape, F32)
    qb = q_ref[0] * (dh ** -0.5)
    q_all = jnp.concatenate([qb[:, j * dh:(j + 1) * dh].astype(BF16) for j in range(J)], axis=0)
    q_pos = jnp.concatenate([q_all, qpc_ref[0].astype(BF16)], axis=1)
    gates = _sigmoid(gate_ref[0, 0])

    s_all = _dot(q_all, kct_ref[0, 0])
    rel_end = (lax.broadcasted_iota(jnp.int32, (1, ncb), 1) * CMP_STRIDE + (CMP_BLOCK - 1)) - start
    mask_c = lax.broadcasted_iota(jnp.int32, (nq, ncb), 0) >= rel_end
    rel_endf = rel_end.astype(F32)
    vc = vc_ref[0, 0]
    o_c, p_sum = [], jnp.zeros((nq, ncb), F32)
    for j in range(J):
        sm = jnp.where(mask_c, s_all[rows[j]] + slopes[j] * rel_endf, NEG)
        mx = jnp.max(sm, axis=-1, keepdims=True)
        e = jnp.exp(sm - jnp.where(mx > 0.5 * NEG, mx, 0.0))
        inv = 1.0 / jnp.maximum(jnp.sum(e, axis=-1, keepdims=True), 1e-30)
        p_sum = p_sum + e * inv
        o_c.append(_dot(e.astype(BF16), vc) * inv)

    ovl = ovl_ref[...]
    imp_t = sum(_dot_nt(ovl, part) for part in _split3(p_sum))
    qpos_row = start + lax.broadcasted_iota(jnp.int32, (ns, nq), 1)
    sel_t = _select_blocks_t(imp_t, qpos_row, v_ref, start // SLC_BLOCK + nq // SLC_BLOCK)
    not_q = ((sel_t.T - 1.0) * BIG).astype(BF16)
    q_aug = jnp.concatenate([q_pos, jnp.concatenate([not_q] * J, axis=0)], axis=1)

    tk = SEL_TILE
    causal = (lax.broadcasted_iota(jnp.int32, (nq, tk), 0) - lax.broadcasted_iota(jnp.int32, (nq, tk), 1))

    last = start // tk
    bpt = tk // SLC_BLOCK
    n_flags = ns // bpt
    v_ref[...] = sel_t
    for t in range(n_flags):
        flag_ref[t] = (jnp.max(v_ref[t * bpt:(t + 1) * bpt, :]) > 0.5).astype(jnp.int32)

    def active(t):
        return jnp.where(t < last, flag_ref[jnp.minimum(t, n_flags - 1)], (t == last).astype(jnp.int32)) > 0

    def scores(t, s_ref):
        s_ref[...] = _dot(q_aug, kaug_ref[:, pl.ds(pl.multiple_of(t * tk, tk), tk)])

    def absorb(t, s_ref):
        k0 = pl.multiple_of(t * tk, tk)
        vt = va_ref[0, 0, pl.ds(k0, tk), :]
        shift = (k0 - start).astype(F32)
        allowed = causal + (start - k0) >= 0
        for j in range(J):
            sj = jnp.where(allowed, s_ref[rows[j], :], -BIG)
            c = slopes[j] * shift
            m_old = m_ref[rows[j], :]
            m_new = jnp.maximum(m_old, jnp.max(sj, axis=-1, keepdims=True) + c)
            p = jnp.exp(sj - (m_new - c))
            acc_ref[rows[j], :] = acc_ref[rows[j], :] * jnp.exp(m_old - m_new) + _dot(p.astype(BF16), vt)
            m_ref[rows[j], :] = m_new

    def step(t, s_cur, s_nxt):
        a, b = active(t), active(t + 1)

        @pl.when(a & b)
        def _():
            scores(t + 1, s_nxt)
            absorb(t, s_cur)

        @pl.when(a & jnp.logical_not(b))
        def _():
            absorb(t, s_cur)

        @pl.when(jnp.logical_not(a) & b)
        def _():
            scores(t + 1, s_nxt)

    scores(0, s0_ref)

    def tile_pair(u, carry):
        step(2 * u, s0_ref, s1_ref)
        step(2 * u + 1, s1_ref, s0_ref)
        return carry

    lax.fori_loop(0, last // 2 + 1, tile_pair, 0)

    s_w = _dot(q_pos, kwaug_ref[:, pl.ds(pl.multiple_of(start, nq), WIN_KEYS)])
    vw = vwp_ref[0, 0, pl.ds(pl.multiple_of(start, nq), WIN_KEYS), :]
    qrow = lax.broadcasted_iota(jnp.int32, (nq, WIN_KEYS), 0)
    wcol = lax.broadcasted_iota(jnp.int32, (nq, WIN_KEYS), 1)
    mask_w = (wcol > qrow) & (wcol - WINDOW <= qrow)
    chunk = lax.broadcasted_iota(jnp.int32, (1, WIN_KEYS), 1) // WIN_CHUNK
    p0 = start - WINDOW + chunk * WIN_CHUNK
    origin = (jnp.maximum(p0, 0) // POS_PERIOD) * POS_PERIOD - start
    for j in range(J):
        bias = jnp.where(p0 >= 0, slopes[j] * origin.astype(F32), -BIG)
        sm = jnp.where(mask_w, s_w[rows[j]] + bias, NEG)
        p = jnp.exp(sm - jnp.max(sm, axis=-1, keepdims=True))
        o_w = _dot(p.astype(BF16), vw) / jnp.sum(p, axis=-1, keepdims=True)
        acc = acc_ref[rows[j], :]
        o_s = acc[:, 0:dh] / jnp.maximum(acc[:, dh:dh + 1], 1e-30)
        out = (gates[:, 3 * j:3 * j + 1] * o_c[j] + gates[:, 3 * j + 1:3 * j + 2] * o_s
               + gates[:, 3 * j + 2:3 * j + 3] * o_w)
        o_ref[0, :, j * dh:(j + 1) * dh] = out.astype(o_ref.dtype)


def _slope_columns():
    s = jnp.asarray(_alibi_slopes())
    pieces = jnp.stack([p.astype(F32) for p in _split3(s)], axis=1)
    cols = jnp.concatenate([pieces * SLC_BLOCK, pieces,
                            jnp.zeros((NSA_HEADS, NSA_HEAD_DIM - 6), F32)], axis=1)
    return jnp.repeat(cols, NSA_NQ, axis=0).reshape(NSA_KV_HEADS, NSA_HPG * NSA_NQ, NSA_HEAD_DIM)


def _position_rows(L):
    t = np.arange(L) % POS_PERIOD
    rows = np.zeros((NSA_HEAD_DIM, L), np.float32)
    rows[0:3] = t // SLC_BLOCK
    rows[3:6] = t % SLC_BLOCK
    return jnp.asarray(rows, BF16)


def _block_rows(ns, L):
    return jnp.asarray((np.arange(ns)[:, None] == np.arange(L)[None, :] // SLC_BLOCK).astype(np.float32), BF16)


def _nsa_prompt(q, gates_raw, kct, vc, kv_t, va, vwp):
    b, L, _ = q.shape
    G, dh, J = NSA_KV_HEADS, NSA_HEAD_DIM, NSA_HPG
    ncb, ns = L // CMP_STRIDE, L // SLC_BLOCK
    nqb = L // NSA_NQ
    per_bg = lambda shape: pl.BlockSpec((1, 1) + shape, lambda i, g, t, s: (i, g, 0, 0))
    const = lambda shape: pl.BlockSpec(shape, lambda i, g, t, s: (0,) * len(shape))
    kv_rows = lambda w: pl.BlockSpec((1, dh, L), lambda i, g, t, s: (i, w * G + g, 0))
    grid_spec = pltpu.PrefetchScalarGridSpec(
        num_scalar_prefetch=1,
        grid=(b, G, nqb),
        in_specs=[pl.BlockSpec((1, NSA_NQ, J * dh), lambda i, g, t, s: (i, t, g)),
                  pl.BlockSpec((1, 1, NSA_NQ, 3 * J), lambda i, g, t, s: (i, g, t, 0)),
                  pl.BlockSpec((1, J * NSA_NQ, dh), lambda i, g, t, s: (g, 0, 0)),
                  per_bg((dh, ncb)), per_bg((ncb, dh)), kv_rows(2), per_bg((L, LANES)),
                  kv_rows(4), per_bg((WINDOW + L, dh)),
                  const((dh, L)), const((ns, L)), const((ns, ncb))],
        out_specs=pl.BlockSpec((1, NSA_NQ, J * dh), lambda i, g, t, s: (i, t, g)),
        scratch_shapes=[pltpu.VMEM((ns, NSA_NQ), F32),
                        pltpu.VMEM((LANES + ns, L), BF16), pltpu.VMEM((LANES, WINDOW + L), BF16),
                        pltpu.VMEM((J * NSA_NQ, 1), F32), pltpu.VMEM((J * NSA_NQ, LANES), F32),
                        pltpu.VMEM((J * NSA_NQ, SEL_TILE), F32), pltpu.VMEM((J * NSA_NQ, SEL_TILE), F32),
                        pltpu.SMEM((ns // (SEL_TILE // SLC_BLOCK),), jnp.int32)],
    )
    return pl.pallas_call(
        _nsa_prompt_kernel,
        grid_spec=grid_spec,
        out_shape=jax.ShapeDtypeStruct((b, L, G * J * dh), BF16),
        compiler_params=_params("arbitrary", "arbitrary", "arbitrary"),
        name="nsa_prompt",
    )(jnp.asarray(_alibi_slopes()), q, gates_raw, _slope_columns(), kct, vc, kv_t, va, kv_t, vwp,
      _position_rows(L), _block_rows(ns, L), jnp.asarray(_overlap_t(ncb, ns), BF16))


def _nsa_prompt_branch(q, kv, kv_t, small, cw):
    b, L, _ = q.shape
    G, dh = NSA_KV_HEADS, NSA_HEAD_DIM
    cmp = _compress_prompt(kv, cw).astype(BF16)
    kct = cmp[:, 0].transpose(0, 1, 3, 2)
    vc = cmp[:, 1]
    rows = lambda w: kv[:, :, w * KVW:(w + 1) * KVW].astype(BF16).reshape(b, L, G, dh).transpose(0, 2, 1, 3)
    va = jnp.concatenate([rows(3), jnp.ones((b, G, L, 1), BF16), jnp.zeros((b, G, L, LANES - dh - 1), BF16)], axis=-1)
    vwp = jnp.pad(rows(5), ((0, 0), (0, 0), (WINDOW, 0), (0, 0)))
    gates_raw = small[:, :, SSD_HEADS:SSD_HEADS + 3 * NSA_HEADS].reshape(b, L, G, 3 * NSA_HPG).transpose(0, 2, 1, 3)
    return _nsa_prompt(q, gates_raw, kct, vc, kv_t, va, vwp)


def _layer_norm(v, g, b):
    mu = jnp.mean(v, axis=-1, keepdims=True)
    c = v - mu
    var = jnp.mean(c * c, axis=-1, keepdims=True)
    return c * lax.rsqrt(var + NORM_EPS) * g + b


def _route_t(logits_t, bias_col, v_ref):
    e_n, n = logits_t.shape
    gsz = e_n // N_ROUTE_GROUPS
    scores = _sigmoid(logits_t)
    biased = scores + bias_col
    grp = []
    for a in range(N_ROUTE_GROUPS):
        blk = biased[a * gsz:(a + 1) * gsz]
        m1 = jnp.max(blk, axis=0, keepdims=True)
        cnt = jnp.sum(jnp.where(blk == m1, 1.0, 0.0), axis=0, keepdims=True)
        m2 = jnp.max(jnp.where(blk < m1, blk, -jnp.inf), axis=0, keepdims=True)
        grp.append(m1 + jnp.where(cnt >= 2.0, m1, m2))
    pieces = []
    for a in range(N_ROUTE_GROUPS):
        rank = jnp.zeros((1, n), jnp.int32)
        for b in range(N_ROUTE_GROUPS):
            if b != a:
                ahead = (grp[b] > grp[a]) | ((grp[b] == grp[a]) & (b < a))
                rank = rank + jnp.where(ahead, 1, 0)
        pieces.append(jnp.where(rank < TOPK_ROUTE_GROUPS, biased[a * gsz:(a + 1) * gsz], -jnp.inf))
    masked = jnp.concatenate(pieces, axis=0)
    v_ref[...] = masked
    e_i = lax.broadcasted_iota(jnp.int32, (e_n, n), 0)

    def body(ep, rank):
        r = v_ref[pl.ds(ep, 1), :]
        ahead = (r > masked) | ((r == masked) & (ep < e_i))
        return rank + jnp.where(ahead, 1, 0)

    rank = lax.fori_loop(0, e_n, body, jnp.zeros((e_n, n), jnp.int32))
    picked = jnp.where(rank < TOP_K, scores, 0.0)
    return picked / jnp.sum(picked, axis=0, keepdims=True) * ROUTE_SCALE


def _finish_kernel(x_ref, sy_ref, ny_ref, gm_ref, wsd_ref, wnd_ref, wo_ref, bo_ref, g1_ref, b1_ref,
                   wrh_ref, wrl_ref, br_ref, h_ref, hb_ref, gt_ref, slot_ref, cnt_ref, v_ref):
    gate = _sigmoid(gm_ref[...])
    mixed = (gate[:, :D_MODEL] * _dot(sy_ref[...], wsd_ref[...])
             + gate[:, D_MODEL:] * _dot(ny_ref[...], wnd_ref[...]))
    o = _dot(mixed.astype(BF16), wo_ref[...]) + bo_ref[...]
    h = _layer_norm(ALPHA * x_ref[...] + o, g1_ref[...], b1_ref[...])
    h_ref[...] = h
    hb_ref[...] = h.astype(BF16)
    h_hi, h_lo = _split2(h)
    logits_t = _dot_nt(wrh_ref[...], h_hi) + _dot_nt(wrh_ref[...], h_lo) + _dot_nt(wrl_ref[...], h_hi)
    gates_t = _route_t(logits_t, br_ref[...], v_ref)
    gt_ref[...] = gates_t
    tm = gates_t.shape[1]
    routed = gates_t > 0.0
    earlier = jnp.where(lax.broadcasted_iota(jnp.int32, (tm, tm), 0) < lax.broadcasted_iota(jnp.int32, (tm, tm), 1),
                        1.0, 0.0).astype(BF16)
    ones = jnp.where(routed, 1.0, 0.0)
    slot_ref[...] = jnp.where(routed, _dot(ones.astype(BF16), earlier) + 1.0, 0.0)
    cnt_ref[0] = jnp.broadcast_to(jnp.sum(ones, axis=1, keepdims=True), (N_EXPERTS, LANES))


def _finish(x2d, ssd_y, nsa_y, gm, fw, tm):
    t = x2d.shape[0]
    wsd, wnd, wo, bo, g1, b1, wrh, wrl, br = fw
    row = lambda w: pl.BlockSpec((tm, w), lambda i: (i, 0))
    full = lambda a: pl.BlockSpec(a.shape, lambda i: (0,) * a.ndim)
    col = pl.BlockSpec((N_EXPERTS, tm), lambda i: (0, i))
    return pl.pallas_call(
        _finish_kernel,
        grid=(t // tm,),
        in_specs=[row(D_MODEL), row(SSD_D_INNER), row(D_MODEL), row(2 * D_MODEL)] + [full(a) for a in fw],
        out_specs=[row(D_MODEL), row(D_MODEL), col, col, pl.BlockSpec((1, N_EXPERTS, LANES), lambda i: (i, 0, 0))],
        out_shape=[jax.ShapeDtypeStruct((t, D_MODEL), F32), jax.ShapeDtypeStruct((t, D_MODEL), BF16),
                   jax.ShapeDtypeStruct((N_EXPERTS, t), F32), jax.ShapeDtypeStruct((N_EXPERTS, t), F32),
                   jax.ShapeDtypeStruct((t // tm, N_EXPERTS, LANES), F32)],
        scratch_shapes=[pltpu.VMEM((N_EXPERTS, tm), F32)],
        compiler_params=_params("arbitrary"),
        name="finish",
    )(x2d, ssd_y, nsa_y, gm, *fw)


def _finish_weights(w_ssd_down, w_nsa_down, w_out, b_out, ln1_g, ln1_b, w_router, b_router):
    wr_t = w_router.astype(F32).T
    wrh = wr_t.astype(BF16)
    wrl = (wr_t - wrh.astype(F32)).astype(BF16)
    return (w_ssd_down.astype(BF16), w_nsa_down.astype(BF16), w_out.astype(BF16), b_out.astype(F32)[None, :],
            ln1_g.astype(F32)[None, :], ln1_b.astype(F32)[None, :], wrh, wrl, b_router.astype(F32)[:, None])


MOE_R = 64
MOE_ALIGN = 16
MOE_BM = 512
MOE_TAIL = MOE_BM
MOE_GROUP = 8


def _swiglu(x, w1, w3):
    return _silu(_dot(x, w1)) * _dot(x, w3)


def _moe_blocks(t, tt):
    rows = (t * TOP_K + (t // tt) * N_EXPERTS * (MOE_ALIGN - 1) + N_EXPERTS * (MOE_R + MOE_BM - 1))
    return pl.cdiv(rows, MOE_BM) + MOE_TAIL // MOE_BM


def _moe_plan(cnt, nb):
    aligned = (cnt + MOE_ALIGN - 1) // MOE_ALIGN * MOE_ALIGN
    before = jnp.cumsum(aligned, axis=0) - aligned
    total = jnp.sum(aligned, axis=0)
    region = (total + MOE_R + MOE_BM - 1) // MOE_BM * MOE_BM
    region_end = jnp.cumsum(region)
    start = (region_end - region)[None, :] + before
    n_pass = jnp.maximum(jnp.max((cnt + MOE_R - 1) // MOE_R, axis=1), 1)
    blk_row0 = jnp.arange(nb, dtype=region_end.dtype) * MOE_BM
    blk_exp = jnp.minimum(jnp.sum(region_end[None, :] <= blk_row0[:, None], axis=1), N_EXPERTS - 1)
    i32 = lambda a: a.astype(jnp.int32)
    return (i32(start.reshape(-1)), i32(cnt.reshape(-1)), i32(n_pass), i32(blk_exp),
            i32(region_end[-1:] // MOE_BM))


def _chunk_hits(slot_ref, g, k):
    want = (lax.broadcasted_iota(jnp.int32, (MOE_R, 1), 0) + (k * MOE_R + 1)).astype(F32)
    return [slot_ref[e:e + 1, :] == want for e in range(g * MOE_GROUP, (g + 1) * MOE_GROUP)]


def _for_runs(cnt_ref, tile, k, fn):
    def body(e, c):
        @pl.when(cnt_ref[tile * N_EXPERTS + e] > k * MOE_R)
        def _():
            fn(e)
        return c
    lax.fori_loop(0, N_EXPERTS, body, 0)


def _moe_dispatch_kernel(start_ref, cnt_ref, npass_ref, hb_ref, slot_ref, xs_in_ref, xs_ref, buf_ref, sem_ref):
    del xs_in_ref
    i = pl.program_id(0)
    E, R, GR = N_EXPERTS, MOE_R, MOE_GROUP * MOE_R
    half = i % 2
    h = hb_ref[...]

    def fill(k):
        for g in range(E // MOE_GROUP):
            onehot = jnp.concatenate([jnp.where(hit, 1.0, 0.0).astype(BF16) for hit in _chunk_hits(slot_ref, g, k)],
                                     axis=0)
            buf_ref[half, g * GR:(g + 1) * GR, :] = _dot(onehot, h).astype(BF16)

    def copy(tile, e, k, hf):
        dst = pl.multiple_of(start_ref[tile * E + e] + k * R, MOE_ALIGN)
        return pltpu.make_async_copy(buf_ref.at[hf, pl.ds(pl.multiple_of(e * R, R), R), :],
                                     xs_ref.at[pl.ds(dst, R), :], sem_ref.at[hf])

    fill(0)

    @pl.when(i > 0)
    def _():
        k_prev = npass_ref[i - 1] - 1
        _for_runs(cnt_ref, i - 1, k_prev, lambda e: copy(i - 1, e, k_prev, 1 - half).wait())

    _for_runs(cnt_ref, i, 0, lambda e: copy(i, e, 0, half).start())

    def more(k, c):
        _for_runs(cnt_ref, i, k - 1, lambda e: copy(i, e, k - 1, half).wait())
        fill(k)
        _for_runs(cnt_ref, i, k, lambda e: copy(i, e, k, half).start())
        return c

    lax.fori_loop(1, npass_ref[i], more, 0)

    @pl.when(i == pl.num_programs(0) - 1)
    def _():
        k_last = npass_ref[i] - 1
        _for_runs(cnt_ref, i, k_last, lambda e: copy(i, e, k_last, half).wait())


def _moe_dispatch(hb, slots, plan, nb, tt):
    t = hb.shape[0]
    start, cnt, n_pass, _, _ = plan
    rows = nb * MOE_BM
    grid_spec = pltpu.PrefetchScalarGridSpec(
        num_scalar_prefetch=3,
        grid=(t // tt,),
        in_specs=[pl.BlockSpec((tt, D_MODEL), lambda i, *_: (i, 0)),
                  pl.BlockSpec((N_EXPERTS, tt), lambda i, *_: (0, i)),
                  pl.BlockSpec(memory_space=pl.ANY)],
        out_specs=pl.BlockSpec(memory_space=pl.ANY),
        scratch_shapes=[pltpu.VMEM((2, N_EXPERTS * MOE_R, D_MODEL), BF16), pltpu.SemaphoreType.DMA((2,))],
    )
    return pl.pallas_call(
        _moe_dispatch_kernel,
        grid_spec=grid_spec,
        out_shape=jax.ShapeDtypeStruct((rows, D_MODEL), BF16),
        input_output_aliases={5: 0},
        compiler_params=_params("arbitrary"),
        name="moe_dispatch",
    )(start, cnt, n_pass, hb, slots, jnp.zeros((rows, D_MODEL), BF16))


def _moe_ffn_kernel(blk_exp_ref, nused_ref, x_ref, w1_ref, w3_ref, w2_ref, y_ref):
    del blk_exp_ref
    in_use = pl.program_id(0) < nused_ref[0]

    @pl.when(in_use)
    def _():
        a = _swiglu(x_ref[...], w1_ref[0], w3_ref[0])
        y_ref[...] = _dot(a.astype(BF16), w2_ref[0]).astype(y_ref.dtype)

    @pl.when(jnp.logical_not(in_use))
    def _():
        y_ref[...] = jnp.zeros(y_ref.shape, y_ref.dtype)


def _moe_ffn(xs, plan, nb, w1, w3, w2):
    _, _, _, blk_exp, n_used = plan
    grid_spec = pltpu.PrefetchScalarGridSpec(
        num_scalar_prefetch=2,
        grid=(nb,),
        in_specs=[pl.BlockSpec((MOE_BM, D_MODEL), lambda j, be, nu: (jnp.minimum(j, nu[0] - 1), 0)),
                  pl.BlockSpec((1, D_MODEL, EXPERT_FF), lambda j, be, nu: (be[j], 0, 0)),
                  pl.BlockSpec((1, D_MODEL, EXPERT_FF), lambda j, be, nu: (be[j], 0, 0)),
                  pl.BlockSpec((1, EXPERT_FF, D_MODEL), lambda j, be, nu: (be[j], 0, 0))],
        out_specs=pl.BlockSpec((MOE_BM, D_MODEL), lambda j, be, nu: (j, 0)),
    )
    return pl.pallas_call(
        _moe_ffn_kernel,
        grid_spec=grid_spec,
        out_shape=jax.ShapeDtypeStruct(xs.shape, BF16),
        compiler_params=_params("arbitrary"),
        name="moe_ffn",
    )(blk_exp, n_used, xs, w1, w3, w2)


def _moe_combine_kernel(start_ref, npass_ref, hb_ref, h_ref, gate_ref, slot_ref, ys_ref,
                        ws1_ref, ws3_ref, ws2_ref, g2_ref, b2_ref, y_ref, buf_ref, sem_ref):
    i = pl.program_id(0)
    n = pl.num_programs(0)
    E, R, GR = N_EXPERTS, MOE_R, MOE_GROUP * MOE_R
    half = i % 2

    def copy(tile, e, k, hf):
        src = pl.multiple_of(start_ref[tile * E + e] + k * R, MOE_ALIGN)
        return pltpu.make_async_copy(ys_ref.at[pl.ds(src, R), :],
                                     buf_ref.at[hf, pl.ds(pl.multiple_of(e * R, R), R), :], sem_ref.at[hf])

    def all_runs(fn):
        def body(e, c):
            fn(e)
            return c
        lax.fori_loop(0, E, body, 0)

    @pl.when(i == 0)
    def _():
        all_runs(lambda e: copy(0, e, 0, 0).start())

    @pl.when(i + 1 < n)
    def _():
        all_runs(lambda e: copy(i + 1, e, 0, 1 - half).start())

    def gathered(k):
        acc = jnp.zeros(y_ref.shape, F32)
        for g in range(E // MOE_GROUP):
            rows = slice(g * GR, (g + 1) * GR)
            hits = _chunk_hits(slot_ref, g, k)
            gate = jnp.concatenate(
                [jnp.sum(jnp.where(hit, gate_ref[e:e + 1, :], 0.0), axis=1, keepdims=True)
                 for hit, e in zip(hits, range(g * MOE_GROUP, (g + 1) * MOE_GROUP))], axis=0)
            onehot = jnp.concatenate([jnp.where(hit, 1.0, 0.0).astype(BF16) for hit in hits], axis=0)
            weighted = (buf_ref[half, rows, :].astype(F32) * gate).astype(BF16)
            acc = acc + _dot_tn(onehot, weighted)
        return acc

    all_runs(lambda e: copy(i, e, 0, half).wait())
    routed = gathered(0)

    def more(k, acc):
        all_runs(lambda e: copy(i, e, k, half).start())
        all_runs(lambda e: copy(i, e, k, half).wait())
        return acc + gathered(k)

    routed = lax.fori_loop(1, npass_ref[i], more, routed)
    x = hb_ref[...]
    shared = _dot(_swiglu(x, ws1_ref[...], ws3_ref[...]).astype(BF16), ws2_ref[...])
    y_ref[...] = _layer_norm(ALPHA * h_ref[...] + (routed + shared), g2_ref[...], b2_ref[...])


def _moe_combine(hb, h, gates_t, slots, ys, plan, mw, tt):
    t = h.shape[0]
    start, _, n_pass, _, _ = plan
    _, _, _, ws1, ws3, ws2, g2, b2 = mw
    weights = (ws1, ws3, ws2, g2, b2)
    row = lambda w: pl.BlockSpec((tt, w), lambda i, *_: (i, 0))
    col = pl.BlockSpec((N_EXPERTS, tt), lambda i, *_: (0, i))
    full = lambda a: pl.BlockSpec(a.shape, lambda i, *_: (0,) * a.ndim)
    grid_spec = pltpu.PrefetchScalarGridSpec(
        num_scalar_prefetch=2,
        grid=(t // tt,),
        in_specs=[row(D_MODEL), row(D_MODEL), col, col, pl.BlockSpec(memory_space=pl.ANY)]
        + [full(a) for a in weights],
        out_specs=row(D_MODEL),
        scratch_shapes=[pltpu.VMEM((2, N_EXPERTS * MOE_R, D_MODEL), BF16), pltpu.SemaphoreType.DMA((2,))],
    )
    return pl.pallas_call(
        _moe_combine_kernel,
        grid_spec=grid_spec,
        out_shape=jax.ShapeDtypeStruct((t, D_MODEL), F32),
        compiler_params=_params("arbitrary"),
        name="moe_combine",
    )(start, n_pass, hb, h, gates_t, slots, ys, *weights)


def _moe_weights(w_e1, w_e3, w_e2, w_s1, w_s3, w_s2, ln2_g, ln2_b):
    return (w_e1.astype(BF16), w_e3.astype(BF16), w_e2.astype(BF16), w_s1.astype(BF16), w_s3.astype(BF16),
            w_s2.astype(BF16), ln2_g.astype(F32)[None, :], ln2_b.astype(F32)[None, :])


def _finish_and_moe(x2d, ssd_y, nsa_y, gm, fw, mw, tt):
    t = x2d.shape[0]
    h, hb, gates_t, slots, run_len = _finish(x2d, ssd_y, nsa_y, gm, fw, tt)
    nb = _moe_blocks(t, tt)
    plan = _moe_plan(run_len[:, :, 0].astype(jnp.int32), nb)
    xs = _moe_dispatch(hb, slots, plan, nb, tt)
    ys = _moe_ffn(xs, plan, nb, mw[0], mw[1], mw[2])
    return _moe_combine(hb, h, gates_t, slots, ys, plan, mw, tt)


def _ssd_sample_kernel(xbc_ref, sc_ref, z_ref, sm_ref, st_ref, cw_ref, cb_ref, dtb_ref, alog_ref, dskip_ref,
                       nw_ref, e_ref, y_ref, sto_ref, xdt_t_ref, dec_t_ref, b_ref, c_ref, xs_ref, yt_ref):
    i = pl.program_id(0)
    db = xbc_ref.shape[0]
    G, N = SSD_GROUPS, SSD_STATE
    GW = SSD_HPG * SSD_HEAD_DIM

    @pl.when(i == 0)
    def _():
        conv = cb_ref[...] + xbc_ref[...] * cw_ref[SSD_CONV - 1:SSD_CONV, :]
        for k in range(SSD_CONV - 1):
            conv = conv + sc_ref[k] * cw_ref[k:k + 1, :]
        act = _silu(conv)
        xs = act[:, :SSD_D_INNER]
        xs_ref[...] = xs
        b_ref[...] = act[:, SSD_D_INNER:SSD_D_INNER + G * N].astype(BF16)
        c_ref[...] = act[:, SSD_D_INNER + G * N:].astype(BF16)
        dt = _softplus(sm_ref[...] + dtb_ref[...])
        dec = jnp.exp(dt * (-jnp.exp(alog_ref[...])))
        e_mat = e_ref[...]
        xdt_t_ref[...] = (xs * _dot_exact_rhs(dt, e_mat)).T.astype(BF16)
        dec_t_ref[...] = _dot_exact_rhs(dec, e_mat).T
        yt_ref[...] = jnp.zeros_like(yt_ref)

    is_row = lax.broadcasted_iota(jnp.int32, (db, N), 0) == i
    onehot = jnp.where(is_row, 1.0, 0.0).astype(BF16)
    is_lane = lax.broadcasted_iota(jnp.int32, (GW, db), 1) == i
    for g in range(G):
        rows = slice(g * GW, (g + 1) * GW)
        b_g = jnp.where(is_row, b_ref[:, g * N:(g + 1) * N], jnp.zeros((), BF16))
        contrib = _dot(xdt_t_ref[rows, :], b_g)
        decay = _dot_exact_rhs(dec_t_ref[rows, :], onehot)
        new = st_ref[0, rows, :] * decay + contrib
        sto_ref[0, rows, :] = new
        y_all = _dot_nt(new.astype(BF16), c_ref[:, g * N:(g + 1) * N])
        yt_ref[rows, :] += jnp.where(is_lane, y_all, 0.0)

    @pl.when(i == pl.num_programs(0) - 1)
    def _():
        y = yt_ref[...].T + dskip_ref[...] * xs_ref[...]
        y = y * _silu(z_ref[...])
        for g in range(G):
            sl = slice(g * GW, (g + 1) * GW)
            yg = y[:, sl]
            rs = lax.rsqrt(jnp.mean(yg * yg, axis=-1, keepdims=True) + NORM_EPS)
            y_ref[:, sl] = (yg * rs * nw_ref[:, sl]).astype(y_ref.dtype)


def _ssd_sample(xbc, state_conv, z, small, state_ssm, conv_w, conv_b, dt_bias, a_log, d_skip, norm_w):
    db = xbc.shape[0]
    pad = LANES - SSD_HEADS
    dtb = jnp.pad(dt_bias.astype(F32), (0, pad))[None, :]
    alog = jnp.pad(a_log.astype(F32), (0, pad))[None, :]
    dskip = jnp.repeat(d_skip.astype(F32), SSD_HEAD_DIM)[None, :]
    sc = state_conv.astype(F32).transpose(1, 0, 2)
    st = state_ssm.reshape(db, SSD_D_INNER, SSD_STATE)
    args = (xbc, sc, z, small, st, conv_w.astype(F32), conv_b.astype(F32)[None, :], dtb, alog, dskip,
            norm_w.astype(F32)[None, :], _head_expand_matrix())
    full = lambda a: pl.BlockSpec(a.shape, lambda i: (0,) * a.ndim)
    st_spec = pl.BlockSpec((1, SSD_D_INNER, SSD_STATE), lambda i: (i, 0, 0))
    y, st_new = pl.pallas_call(
        _ssd_sample_kernel,
        grid=(db,),
        in_specs=[full(a) for a in args[:4]] + [st_spec] + [full(a) for a in args[5:]],
        out_specs=[pl.BlockSpec((db, SSD_D_INNER), lambda i: (0, 0)), st_spec],
        out_shape=[jax.ShapeDtypeStruct((db, SSD_D_INNER), BF16),
                   jax.ShapeDtypeStruct((db, SSD_D_INNER, SSD_STATE), F32)],
        scratch_shapes=[pltpu.VMEM((SSD_D_INNER, db), BF16), pltpu.VMEM((SSD_D_INNER, db), F32),
                        pltpu.VMEM((db, SSD_GROUPS * SSD_STATE), BF16),
                        pltpu.VMEM((db, SSD_GROUPS * SSD_STATE), BF16),
                        pltpu.VMEM((db, SSD_D_INNER), F32), pltpu.VMEM((SSD_D_INNER, db), F32)],
        compiler_params=_params("arbitrary"),
        name="ssd_sample",
    )(*args)
    return y, st_new.reshape(state_ssm.shape)


def _nsa_sample_kernel(n_pages, *refs):
    pt_ref = refs[0]
    page_refs = refs[1:1 + n_pages]
    (win_ref, q_ref, kvn_ref, gate_ref, slope_ref, ovl_ref, w1ab_ref, w1flat_ref, pe_ref, b1_ref, w2_ref,
     b2_ref, o_ref, wino_ref, cmp_rows_ref, ks_ref, vs_ref, kw_ref, vw_ref, bsh_ref, kc_ref, vc_ref) = refs[1 + n_pages:]
    del pt_ref
    H, dh, G = NSA_HEADS, NSA_HEAD_DIM, NSA_KV_HEADS
    past = n_pages * PAGE_SIZE
    ncb = past // CMP_STRIDE
    nsb = ovl_ref.shape[1]
    wbuf = win_ref.shape[3]
    tiles = KVW // LANES

    spp = PAGE_SIZE // CMP_STRIDE
    src_row = lax.broadcasted_iota(jnp.int32, (PAGE_SIZE, PAGE_SIZE), 1)
    regroup = jnp.where(lax.broadcasted_iota(jnp.int32, (PAGE_SIZE, PAGE_SIZE), 0)
                        == spp * (src_row % CMP_STRIDE) + src_row // CMP_STRIDE, 1.0, 0.0).astype(BF16)
    for p in range(n_pages):
        r = slice(p * PAGE_SIZE, (p + 1) * PAGE_SIZE)
        for w in range(2):
            rows_wp = _dot_nt(regroup, page_refs[p][0, w].astype(BF16))
            for t in range(tiles):
                for l in range(CMP_STRIDE):
                    cmp_rows_ref[w * tiles + t, l, p * spp:(p + 1) * spp, :] = (
                        rows_wp[l * spp:(l + 1) * spp, t * LANES:(t + 1) * LANES])
        ks_ref[:, r] = page_refs[p][0, 2].astype(BF16)
        vs_ref[:, r] = page_refs[p][0, 3].astype(BF16)
    new8 = jnp.broadcast_to(kvn_ref[0], (8, 6 * KVW))
    first = lax.broadcasted_iota(jnp.int32, (KVW, LANES), 1) == 0

    def new_col(w):
        return new8[:, w * KVW:(w + 1) * KVW].T[:, 0:1]

    put = lambda w: jnp.where(first, new_col(w), 0.0).astype(BF16)
    ks_ref[:, past:past + LANES] = put(2)
    vs_ref[:, past:past + LANES] = put(3)
    kw_ref[:, 0:wbuf] = win_ref[0, 0].astype(BF16)
    vw_ref[:, 0:wbuf] = win_ref[0, 1].astype(BF16)
    kw_ref[:, wbuf:wbuf + LANES] = put(4)
    vw_ref[:, wbuf:wbuf + LANES] = put(5)
    last = lax.broadcasted_iota(jnp.int32, (KVW, wbuf), 1) == wbuf - 1
    for w in range(2):
        wino_ref[0, w] = jnp.where(last, new_col(4 + w), pltpu.roll(win_ref[0, w], wbuf - 1, 1))

    def compress(w, dst):
        for t in range(tiles):
            def load_strip(l, t=t):
                return cmp_rows_ref[w * tiles + t, l]

            def store(g, val, t=t):
                c0 = (t * GPT + g) * dh
                dst[:, c0:c0 + dh] = val.astype(BF16)

            sub = lambda ref: ref.at[pl.ds(w, 1)]
            _compress_body(load_strip, ncb, sub(w1ab_ref), sub(w1flat_ref), sub(pe_ref), sub(b1_ref), sub(w2_ref),
                           sub(b2_ref), bsh_ref.at[w], store)

    compress(0, kc_ref)

    head_r = lax.broadcasted_iota(jnp.int32, (H, KVW), 0) // NSA_HPG
    lane_g = lax.broadcasted_iota(jnp.int32, (H, KVW), 1) // dh
    diag = head_r == lane_g
    q16 = q_ref[0] * (dh ** -0.5)
    q_bd = jnp.where(diag, jnp.concatenate([q16] * G, axis=1), 0.0).astype(BF16)
    slopes = slope_ref[...]
    gates = _sigmoid(gate_ref[0])

    def fold(o):
        o = jnp.where(diag, o, 0.0)
        return sum(o[:, g * dh:(g + 1) * dh] for g in range(G))

    d_c = past - (lax.broadcasted_iota(jnp.int32, (H, ncb), 1) * CMP_STRIDE + (CMP_BLOCK - 1))
    p_c = _masked_softmax_rows(_dot_nt(q_bd, kc_ref[...]) - slopes * d_c.astype(F32), d_c >= 0)
    compress(1, vc_ref)
    o_c = fold(_dot(p_c.astype(BF16), vc_ref[...]))

    p_grp = jnp.concatenate(
        [jnp.sum(p_c[g * NSA_HPG:(g + 1) * NSA_HPG], axis=0, keepdims=True) for g in range(G)]
        + [jnp.zeros((8 - G, ncb), F32)], axis=0)
    imp = sum(_dot(part, ovl_ref[...]) for part in _split3(p_grp))
    m_l = lax.broadcasted_iota(jnp.int32, (8, nsb), 1)
    valid = m_l * SLC_BLOCK <= past
    cur = past // SLC_BLOCK
    forced = (m_l == 0) | (m_l == cur) | (m_l == cur - 1)
    v = jnp.where(valid, jnp.where(forced, jnp.inf, imp), -jnp.inf)
    v_t = v.T
    mp_i = lax.broadcasted_iota(jnp.int32, (nsb, nsb), 0)
    m_i = lax.broadcasted_iota(jnp.int32, (nsb, nsb), 1)
    sel_rows = []
    for g in range(G):
        v_col = jnp.broadcast_to(v_t[:, g:g + 1], (nsb, nsb))
        v_row = jnp.broadcast_to(v[g:g + 1, :], (nsb, nsb))
        ahead = (v_col > v_row) | ((v_col == v_row) & (mp_i < m_i))
        rank = jnp.sum(jnp.where(ahead, 1.0, 0.0), axis=0, keepdims=True)
        sel = jnp.where((rank < SLC_TOPK) & valid[0:1, :], 1.0, 0.0)
        sel_rows.append(jnp.broadcast_to(sel, (NSA_HPG, nsb)))
    sel_h = jnp.concatenate(sel_rows, axis=0).astype(BF16)

    nk = past + LANES
    expand = jnp.where(lax.broadcasted_iota(jnp.int32, (nsb, nk), 0)
                       == lax.broadcasted_iota(jnp.int32, (nsb, nk), 1) // SLC_BLOCK, 1.0, 0.0).astype(BF16)
    d_s = past - lax.broadcasted_iota(jnp.int32, (H, nk), 1)
    ok = (_dot(sel_h, expand) > 0.5) & (d_s >= 0)
    p_s = _masked_softmax_rows(_dot(q_bd, ks_ref[...]) - slopes * d_s.astype(F32), ok)
    o_s = fold(_dot_nt(p_s.astype(BF16), vs_ref[...]))

    nw = wbuf + LANES
    d_w = wbuf - lax.broadcasted_iota(jnp.int32, (H, nw), 1)
    p_w = _masked_softmax_rows(_dot(q_bd, kw_ref[...]) - slopes * d_w.astype(F32), (d_w >= 0) & (d_w < WINDOW))
    o_w = fold(_dot_nt(p_w.astype(BF16), vw_ref[...]))

    o_ref[0] = (gates[:, 0:1] * o_c + gates[:, 1:2] * o_s + gates[:, 2:3] * o_w).astype(o_ref.dtype)


def _overlap_sample(ncb, nsb):
    i = np.arange(ncb)[:, None] * CMP_STRIDE
    m = np.arange(nsb)[None, :] * SLC_BLOCK
    ok = (i < m + SLC_BLOCK) & (i + CMP_BLOCK > m) & (np.arange(ncb)[:, None] < ncb - 1)
    return ok.astype(np.float32)


def _nsa_sample(q, kv_new, small, cache_kv_paged, page_table, cache_kv_win, cw):
    db = q.shape[0]
    H, dh, G = NSA_HEADS, NSA_HEAD_DIM, NSA_KV_HEADS
    n_pages = page_table.shape[1]
    past = n_pages * PAGE_SIZE
    wbuf = cache_kv_win.shape[1]
    ncb = past // CMP_STRIDE
    nsb = LANES * pl.cdiv(pl.cdiv(past + 1, SLC_BLOCK), LANES)
    pages = cache_kv_paged.transpose(0, 2, 3, 4, 1).reshape(cache_kv_paged.shape[0], 4, KVW, PAGE_SIZE)
    win = cache_kv_win.transpose(0, 2, 3, 4, 1).reshape(db, 2, KVW, wbuf)
    gates_raw = small[:, SSD_HEADS:SSD_HEADS + 3 * H].reshape(db, H, 3)
    slopes = jnp.asarray(_alibi_slopes())[:, None]
    ovl = jnp.asarray(_overlap_sample(ncb, nsb), BF16)
    consts = (slopes, ovl) + tuple(cw)
    per_seq = lambda shape: pl.BlockSpec((1,) + shape, lambda i, pt: (i,) + (0,) * len(shape))
    full = lambda a: pl.BlockSpec(a.shape, lambda i, pt: (0,) * a.ndim)
    page_spec = lambda p: pl.BlockSpec((1, 4, KVW, PAGE_SIZE), lambda i, pt: (pt[i, p], 0, 0, 0))
    grid_spec = pltpu.PrefetchScalarGridSpec(
        num_scalar_prefetch=1,
        grid=(db,),
        in_specs=[page_spec(p) for p in range(n_pages)]
        + [per_seq((2, KVW, wbuf)), per_seq((H, dh)), per_seq((1, 6 * KVW)), per_seq((H, 3))]
        + [full(a) for a in consts],
        out_specs=[per_seq((H, dh)), per_seq((2, KVW, wbuf))],
        scratch_shapes=[pltpu.VMEM((2 * KVW // LANES, CMP_STRIDE, ncb, LANES), F32),
                        pltpu.VMEM((KVW, past + LANES), BF16), pltpu.VMEM((KVW, past + LANES), BF16),
                        pltpu.VMEM((KVW, wbuf + LANES), BF16), pltpu.VMEM((KVW, wbuf + LANES), BF16),
                        pltpu.VMEM((2, ncb + 8, CMP_HIDDEN), F32),
                        pltpu.VMEM((ncb, KVW), BF16), pltpu.VMEM((ncb, KVW), BF16)],
    )
    out, win_new = pl.pallas_call(
        functools.partial(_nsa_sample_kernel, n_pages),
        grid_spec=grid_spec,
        out_shape=[jax.ShapeDtypeStruct((db, H, dh), BF16), jax.ShapeDtypeStruct((db, 2, KVW, wbuf), cache_kv_win.dtype)],
        compiler_params=_params("arbitrary"),
        name="nsa_sample",
    )(page_table.astype(jnp.int32), *([pages] * n_pages), win, q.reshape(db, H, dh), kv_new.reshape(db, 1, 6 * KVW),
      gates_raw, *consts)
    win_new = win_new.reshape(db, 2, G, dh, wbuf).transpose(0, 4, 1, 2, 3)
    return out.reshape(db, H * dh), win_new


def kernel(x_prompt, x_sample, cache_kv_paged, cache_kv_win, state_ssm, state_conv, page_table, w_in, b_in, conv_w, conv_b, dt_bias, a_log, d_skip, ssd_norm_w, cmp_w1, cmp_b1, cmp_w2, cmp_b2, cmp_pe, w_ssd_down, w_nsa_down, w_out, b_out, ln1_g, ln1_b, w_router, b_router, w_e1, w_e3, w_e2, w_s1, w_s3, w_s2, ln2_g, ln2_b):
    b, L, _ = x_prompt.shape
    db = x_sample.shape[0]
    G, dh = NSA_KV_HEADS, NSA_HEAD_DIM
    wbuf = cache_kv_win.shape[1]
    wp, bp = _pack_w_in(w_in, b_in)
    cw = _compress_weights(cmp_w1, cmp_b1, cmp_w2, cmp_b2, cmp_pe)
    fw = _finish_weights(w_ssd_down, w_nsa_down, w_out, b_out, ln1_g, ln1_b, w_router, b_router)
    mw = _moe_weights(w_e1, w_e3, w_e2, w_s1, w_s3, w_s2, ln2_g, ln2_b)
    ssd_w = (conv_w, conv_b, dt_bias, a_log, d_skip, ssd_norm_w)
    tm = min(256, b * L)
    per_seq = lambda a: a.reshape(b, L, a.shape[-1])

    xp = x_prompt.reshape(b * L, D_MODEL)
    z, xbc, q, kv, gm, small = _in_proj(xp, wp, bp, tm)
    ssd_y, st_t = _ssd_prompt(per_seq(xbc), per_seq(z), per_seq(small), *ssd_w)
    kv_t = _kv_channel_major(x_prompt, w_in, b_in, min(512, L))
    nsa_y = _nsa_prompt_branch(per_seq(q), per_seq(kv), kv_t, per_seq(small), cw)
    yp = _finish_and_moe(xp, ssd_y.reshape(b * L, -1), nsa_y.reshape(b * L, -1), gm, fw, mw, tm).reshape(b, L, D_MODEL)
    kv6_t = kv_t.reshape(b, 6, G, dh, L)
    kv_rows_p = kv6_t[:, :4].transpose(0, 4, 1, 2, 3)
    win_p = kv6_t[:, 4:, :, :, L - wbuf:].transpose(0, 4, 1, 2, 3)
    ssm_p = st_t.reshape(b, SSD_STATE, SSD_HEADS, SSD_HEAD_DIM).transpose(0, 2, 3, 1).astype(state_ssm.dtype)
    conv_p = per_seq(xbc)[:, L - (SSD_CONV - 1):]

    xs = x_sample.reshape(db, D_MODEL)
    z, xbc, q, kv, gm, small = _in_proj(xs, wp, bp, db)
    ssd_y, ssm_s = _ssd_sample(xbc, state_conv, z, small, state_ssm, *ssd_w)
    nsa_y, win_s = _nsa_sample(q, kv, small, cache_kv_paged, page_table, cache_kv_win, cw)
    ys = _finish_and_moe(xs, ssd_y, nsa_y, gm, fw, mw, db).reshape(db, 1, D_MODEL)
    kv6 = kv.reshape(db, 1, 6, G, dh)
    kv_rows_s = kv6[:, :, :4]
    conv_s = jnp.concatenate([state_conv[:, 1:].astype(xbc.dtype), xbc[:, None, :]], axis=1)
    return (yp, ys, kv_rows_p, kv_rows_s, win_p, win_s, ssm_p, ssm_s, conv_p, conv_s)
```

```python
import functools

import numpy as np
import jax
import jax.numpy as jnp
from jax import lax
from jax.experimental import pallas as pl
from jax.experimental.pallas import tpu as pltpu

F32 = jnp.float32
BF16 = jnp.bfloat16

D_MODEL = 1024
PAGE_SIZE = 128
SSD_D_INNER = 2048
SSD_HEAD_DIM = 64
SSD_HEADS = 32
SSD_GROUPS = 4
SSD_HPG = 8
SSD_STATE = 128
SSD_CONV = 4
SSD_CHUNK = 128
SSD_CONV_CH = SSD_D_INNER + 2 * SSD_GROUPS * SSD_STATE
NSA_HEAD_DIM = 64
NSA_HEADS = 16
NSA_KV_HEADS = 4
NSA_HPG = 4
CMP_BLOCK = 32
CMP_STRIDE = 16
CMP_HIDDEN = 128
SLC_BLOCK = 64
SLC_TOPK = 16
WINDOW = 512
NSA_NQ = 128
N_EXPERTS = 64
TOP_K = 8
N_ROUTE_GROUPS = 8
TOPK_ROUTE_GROUPS = 4
EXPERT_FF = 256
ROUTE_SCALE = 2.5
ALPHA = 2.0 ** 0.25
NORM_EPS = 1e-5
IN_SPLITS = (SSD_D_INNER, SSD_CONV_CH, SSD_HEADS, NSA_HEADS * NSA_HEAD_DIM,
             6 * NSA_KV_HEADS * NSA_HEAD_DIM, 3 * NSA_HEADS, 2 * D_MODEL)
IN_OFFSETS = tuple(int(v) for v in np.cumsum(IN_SPLITS)[:-1])

LANES = 128
VMEM_LIMIT = 56 * 1024 * 1024

SEG_Z = (0, 2048)
SEG_XBC = (2048, 3072)
SEG_Q = (5120, 1024)
SEG_KV = (6144, 1536)
SEG_GM = (7680, 2048)
SEG_SMALL = (9728, 128)
N_PACKED = 9856


def _params(*sem):
    return pltpu.CompilerParams(dimension_semantics=sem, vmem_limit_bytes=VMEM_LIMIT)


def _silu(x):
    return x * (1.0 / (1.0 + jnp.exp(-x)))


def _sigmoid(x):
    return 1.0 / (1.0 + jnp.exp(-x))


def _softplus(x):
    return jnp.maximum(x, 0.0) + jnp.log(1.0 + jnp.exp(-jnp.abs(x)))


def _split2(x):
    hi = x.astype(BF16)
    lo = (x - hi.astype(F32)).astype(BF16)
    return hi, lo


def _split3(x):
    hi = x.astype(BF16)
    r = x - hi.astype(F32)
    mid = r.astype(BF16)
    lo = (r - mid.astype(F32)).astype(BF16)
    return hi, mid, lo


def _dot(a, b):
    return jnp.dot(a, b, preferred_element_type=F32)


def _dot_nt(a, b):
    return lax.dot_general(a, b, (((1,), (1,)), ((), ())), preferred_element_type=F32)


def _dot_tn(a, b):
    return lax.dot_general(a, b, (((0,), (0,)), ((), ())), preferred_element_type=F32)


def _dot_exact_rhs(x, sel):
    a, b, c = _split3(x)
    return _dot(a, sel) + _dot(b, sel) + _dot(c, sel)


def _pack_w_in(w_in, b_in):
    def pack(m):
        z, xbc, dt, q, kv, gn, gm = jnp.split(m, list(IN_OFFSETS), axis=-1)
        pad = jnp.zeros(m.shape[:-1] + (LANES - SSD_HEADS - 3 * NSA_HEADS,), m.dtype)
        return jnp.concatenate([z, xbc, q, kv, gm, dt, gn, pad], axis=-1)
    return pack(w_in).astype(BF16), pack(b_in[None, :])


def _in_proj_kernel(x_ref, w_ref, b_ref, z_ref, xbc_ref, q_ref, kv_ref, gm_ref, sm_ref):
    x = x_ref[...].astype(BF16)
    for ref, (off, width) in ((z_ref, SEG_Z), (xbc_ref, SEG_XBC), (q_ref, SEG_Q), (kv_ref, SEG_KV),
                              (gm_ref, SEG_GM), (sm_ref, SEG_SMALL)):
        ref[...] = _dot(x, w_ref[:, off:off + width]) + b_ref[:, off:off + width]


def _in_proj(x2d, w_packed, b_packed, tm):
    t = x2d.shape[0]
    segs = (SEG_Z, SEG_XBC, SEG_Q, SEG_KV, SEG_GM, SEG_SMALL)
    return pl.pallas_call(
        _in_proj_kernel,
        grid=(t // tm,),
        in_specs=[pl.BlockSpec((tm, D_MODEL), lambda i: (i, 0)),
                  pl.BlockSpec((D_MODEL, N_PACKED), lambda i: (0, 0), pipeline_mode=pl.Buffered(1)),
                  pl.BlockSpec((1, N_PACKED), lambda i: (0, 0))],
        out_specs=[pl.BlockSpec((tm, w), lambda i: (i, 0)) for _, w in segs],
        out_shape=[jax.ShapeDtypeStruct((t, w), F32) for _, w in segs],
        compiler_params=_params("arbitrary"),
        name="in_proj",
    )(x2d, w_packed, b_packed)


def _kv_channel_major_kernel(x_ref, wt_ref, bt_ref, o_ref):
    o_ref[0] = _dot_nt(wt_ref[...], x_ref[0].astype(BF16)) + bt_ref[...]


def _kv_channel_major(x, w_in, b_in, tm):
    b, L, _ = x.shape
    lo, n = IN_OFFSETS[3], IN_SPLITS[4]
    wt = w_in[:, lo:lo + n].T.astype(BF16)
    bt = b_in[lo:lo + n].astype(F32)[:, None]
    return pl.pallas_call(
        _kv_channel_major_kernel,
        grid=(b, L // tm),
        in_specs=[pl.BlockSpec((1, tm, D_MODEL), lambda i, j: (i, j, 0)),
                  pl.BlockSpec((n, D_MODEL), lambda i, j: (0, 0)),
                  pl.BlockSpec((n, 1), lambda i, j: (0, 0))],
        out_specs=pl.BlockSpec((1, n, tm), lambda i, j: (i, 0, j)),
        out_shape=jax.ShapeDtypeStruct((b, n, L), F32),
        compiler_params=_params("arbitrary", "arbitrary"),
        name="kv_channel_major",
    )(x, wt, bt)


def _head_expand_matrix():
    h = np.arange(LANES)[:, None]
    c = np.arange(SSD_D_INNER)[None, :] // SSD_HEAD_DIM
    return jnp.asarray((h == c).astype(np.float32), BF16)


def _ssd_prompt_kernel(xbc_ref, z_ref, sm_ref, cw_ref, cb_ref, dtb_ref, alog_ref, dskip_ref, nw_ref, e_ref,
                       y_ref, st_ref, xh_ref, state_ref, ybuf_ref):
    c = pl.program_id(1)
    L = SSD_CHUNK
    G, N, P = SSD_GROUPS, SSD_STATE, SSD_HEAD_DIM
    GW = SSD_HPG * P

    @pl.when(c == 0)
    def _():
        xh_ref[0:8, :] = jnp.zeros((8, SSD_CONV_CH), F32)
        state_ref[...] = jnp.zeros_like(state_ref)

    xh_ref[8:8 + L, :] = xbc_ref[0]
    conv = cb_ref[...] + xh_ref[5:5 + L, :] * cw_ref[0:1, :]
    for k in range(1, SSD_CONV):
        conv = conv + xh_ref[5 + k:5 + k + L, :] * cw_ref[k:k + 1, :]
    xh_ref[0:8, :] = xh_ref[L:L + 8, :]
    act = _silu(conv)
    xs = act[:, :SSD_D_INNER]
    bm = act[:, SSD_D_INNER:SSD_D_INNER + G * N].astype(BF16)
    cm = act[:, SSD_D_INNER + G * N:]

    dt = _softplus(sm_ref[0] + dtb_ref[...])
    da = dt * (-jnp.exp(alog_ref[...]))
    row_i = lax.broadcasted_iota(jnp.int32, (L, L), 0)
    col_i = lax.broadcasted_iota(jnp.int32, (L, L), 1)
    tri = row_i >= col_i
    a_cs = _dot_exact_rhs_left(tri, da)
    a_cs_t = a_cs.T
    dt_t = dt.T
    a_last = a_cs[L - 1:L, :]
    w_end = dt * jnp.exp(a_last - a_cs)
    chunk_decay = jnp.broadcast_to(jnp.exp(a_last), (8, LANES))
    e_mat = e_ref[...]
    w_exp = _dot_exact_rhs(w_end, e_mat)
    dec_exp = _dot_exact_rhs(chunk_decay, e_mat)[0:1, :]
    xd = (xs * w_exp).astype(BF16)

    for g in range(G):
        cg = cm[:, g * N:(g + 1) * N]
        bg = bm[:, g * N:(g + 1) * N]
        cb = _dot_nt(cg.astype(BF16), bg)
        for j in range(SSD_HPG):
            h = g * SSD_HPG + j
            col = jnp.broadcast_to(a_cs[:, h:h + 1], (L, L))
            row = a_cs_t[h:h + 1, :]
            lm = jnp.where(tri, jnp.exp(col - row), 0.0)
            m = (cb * lm * dt_t[h:h + 1, :]).astype(BF16)
            eac = (jnp.exp(col) * cg).astype(BF16)
            lhs = jnp.concatenate([m, eac], axis=1)
            rhs = jnp.concatenate([xs[:, h * P:(h + 1) * P].astype(BF16),
                                   state_ref[:, h * P:(h + 1) * P].astype(BF16)], axis=0)
            ybuf_ref[:, h * P:(h + 1) * P] = _dot(lhs, rhs)
        sl = slice(g * GW, (g + 1) * GW)
        state_ref[:, sl] = state_ref[:, sl] * dec_exp[:, sl] + _dot_tn(bg, xd[:, sl])

    y = ybuf_ref[...] + dskip_ref[...] * xs
    y = y * _silu(z_ref[0])
    for g in range(G):
        sl = slice(g * GW, (g + 1) * GW)
        yg = y[:, sl]
        rs = lax.rsqrt(jnp.mean(yg * yg, axis=-1, keepdims=True) + NORM_EPS)
        y_ref[0, :, sl] = (yg * rs * nw_ref[:, sl]).astype(y_ref.dtype)

    @pl.when(c == pl.num_programs(1) - 1)
    def _():
        st_ref[0] = state_ref[...]


def _dot_exact_rhs_left(mask, x):
    sel = jnp.where(mask, 1.0, 0.0).astype(BF16)
    a, b, c = _split3(x)
    return _dot(sel, a) + _dot(sel, b) + _dot(sel, c)


def _ssd_prompt(xbc, z, small, conv_w, conv_b, dt_bias, a_log, d_skip, norm_w):
    b, L, _ = xbc.shape
    nc = L // SSD_CHUNK
    pad = LANES - SSD_HEADS
    dtb = jnp.pad(dt_bias.astype(F32), (0, pad))[None, :]
    alog = jnp.pad(a_log.astype(F32), (0, pad))[None, :]
    dskip = jnp.repeat(d_skip.astype(F32), SSD_HEAD_DIM)[None, :]
    full = lambda shape: pl.BlockSpec(shape, lambda i, j: (0,) * len(shape))
    y, st = pl.pallas_call(
        _ssd_prompt_kernel,
        grid=(b, nc),
        in_specs=[pl.BlockSpec((1, SSD_CHUNK, SSD_CONV_CH), lambda i, j: (i, j, 0)),
                  pl.BlockSpec((1, SSD_CHUNK, SSD_D_INNER), lambda i, j: (i, j, 0)),
                  pl.BlockSpec((1, SSD_CHUNK, LANES), lambda i, j: (i, j, 0)),
                  full((SSD_CONV, SSD_CONV_CH)), full((1, SSD_CONV_CH)), full((1, LANES)), full((1, LANES)),
                  full((1, SSD_D_INNER)), full((1, SSD_D_INNER)), full((LANES, SSD_D_INNER))],
        out_specs=[pl.BlockSpec((1, SSD_CHUNK, SSD_D_INNER), lambda i, j: (i, j, 0)),
                   pl.BlockSpec((1, SSD_STATE, SSD_D_INNER), lambda i, j: (i, 0, 0))],
        out_shape=[jax.ShapeDtypeStruct((b, L, SSD_D_INNER), BF16),
                   jax.ShapeDtypeStruct((b, SSD_STATE, SSD_D_INNER), F32)],
        scratch_shapes=[pltpu.VMEM((SSD_CHUNK + 8, SSD_CONV_CH), F32),
                        pltpu.VMEM((SSD_STATE, SSD_D_INNER), F32),
                        pltpu.VMEM((SSD_CHUNK, SSD_D_INNER), F32)],
        compiler_params=_params("arbitrary", "arbitrary"),
        name="ssd_prompt",
    )(xbc, z, small, conv_w.astype(F32), conv_b.astype(F32)[None, :], dtb, alog, dskip,
      norm_w.astype(F32)[None, :], _head_expand_matrix())
    return y, st


HALF = CMP_BLOCK // 2
KVW = NSA_KV_HEADS * NSA_HEAD_DIM
ROWS_PER_DOT = 2


def _compress_weights(cmp_w1, cmp_b1, cmp_w2, cmp_b2, cmp_pe):
    w1 = cmp_w1.astype(BF16)
    w1ab = jnp.concatenate([w1[:, :HALF], w1[:, HALF:]], axis=-1)
    zero = jnp.zeros_like(w1ab)
    per_head = [jnp.stack([w1ab if h == g else zero for h in range(GPT)], axis=2) for g in range(GPT)]
    w1ab = jnp.stack(per_head, axis=1).reshape(2, GPT, HALF // ROWS_PER_DOT, ROWS_PER_DOT * LANES, 2 * CMP_HIDDEN)
    w1flat = w1.reshape(2, CMP_BLOCK * NSA_HEAD_DIM, CMP_HIDDEN)
    pe8 = jnp.broadcast_to(cmp_pe.reshape(2, 1, CMP_BLOCK * NSA_HEAD_DIM), (2, 8, CMP_BLOCK * NSA_HEAD_DIM))
    return (w1ab, w1flat, pe8.astype(F32), cmp_b1.astype(F32)[:, None, :], cmp_w2.astype(BF16),
            cmp_b2.astype(F32)[:, None, :])


GPT = LANES // NSA_HEAD_DIM


def _compress_body(load_strip, ns, w1ab_ref, w1flat_ref, pe_ref, b1_ref, w2_ref, b2_ref, bsh_ref, store):
    accs = [jnp.zeros((ns, 2 * CMP_HIDDEN), F32) for _ in range(GPT)]
    for c in range(HALF // ROWS_PER_DOT):
        lhs = jnp.concatenate([load_strip(c * ROWS_PER_DOT + i).astype(BF16) for i in range(ROWS_PER_DOT)], axis=1)
        for g in range(GPT):
            accs[g] = accs[g] + _dot(lhs, w1ab_ref[0, g, c])
    cpe = _dot(pe_ref[0].astype(BF16), w1flat_ref[0])[0:1, :] + b1_ref[0]
    bsh_ref[ns:ns + 8, :] = jnp.zeros((8, CMP_HIDDEN), F32)
    for g in range(GPT):
        bsh_ref[0:ns, :] = accs[g][:, CMP_HIDDEN:]
        hid = _silu(accs[g][:, :CMP_HIDDEN] + bsh_ref[1:ns + 1, :] + cpe)
        store(g, _dot(hid.astype(BF16), w2_ref[0]) + b2_ref[0])


def _compress_prompt_kernel(kv_ref, w1ab_ref, w1flat_ref, pe_ref, b1_ref, w2_ref, b2_ref, out_ref, bsh_ref):
    ns = out_ref.shape[3]

    def load_strip(l):
        return kv_ref[0, pl.ds(l, ns, stride=CMP_STRIDE), :]

    def store(g, val):
        out_ref[0, 0, g] = val

    _compress_body(load_strip, ns, w1ab_ref, w1flat_ref, pe_ref, b1_ref, w2_ref, b2_ref, bsh_ref, store)


def _compress_prompt(kv, cw):
    b, L, _ = kv.shape
    ns = L // CMP_STRIDE
    w1ab, w1flat, pe8, b1, w2, b2 = cw
    tiles = KVW // LANES
    per_which = lambda shape: pl.BlockSpec((1,) + shape, lambda i, w, t: (w,) + (0,) * len(shape))
    return pl.pallas_call(
        _compress_prompt_kernel,
        grid=(b, 2, tiles),
        in_specs=[pl.BlockSpec((1, L, LANES), lambda i, w, t: (i, 0, w * tiles + t)),
                  per_which((GPT, HALF // ROWS_PER_DOT, ROWS_PER_DOT * LANES, 2 * CMP_HIDDEN)),
                  per_which((CMP_BLOCK * NSA_HEAD_DIM, CMP_HIDDEN)),
                  per_which((8, CMP_BLOCK * NSA_HEAD_DIM)),
                  per_which((1, CMP_HIDDEN)),
                  per_which((CMP_HIDDEN, NSA_HEAD_DIM)),
                  per_which((1, NSA_HEAD_DIM))],
        out_specs=pl.BlockSpec((1, 1, GPT, ns, NSA_HEAD_DIM), lambda i, w, t: (i, w, t, 0, 0)),
        out_shape=jax.ShapeDtypeStruct((b, 2, NSA_KV_HEADS, ns, NSA_HEAD_DIM), F32),
        scratch_shapes=[pltpu.VMEM((ns + 8, CMP_HIDDEN), F32)],
        compiler_params=_params("arbitrary", "arbitrary", "arbitrary"),
        name="compress_prompt",
    )(kv, w1ab, w1flat, pe8, b1, w2, b2)


NEG = -1e30
SEL_TILE = 512
WIN_KEYS = WINDOW + NSA_NQ
WIN_CHUNK = 128


def _alibi_slopes():
    h = np.arange(1, NSA_HEADS + 1, dtype=np.float32)
    return (2.0 ** (-8.0 * h / NSA_HEADS)).astype(np.float32)


def _overlap_t(ncb, ns):
    i = np.arange(ncb)[None, :] * CMP_STRIDE
    m = np.arange(ns)[:, None] * SLC_BLOCK
    return ((i < m + SLC_BLOCK) & (i + CMP_BLOCK > m)).astype(np.float32)


def _masked_softmax_rows(s, mask):
    sm = jnp.where(mask, s, NEG)
    mx = jnp.max(sm, axis=-1, keepdims=True)
    p = jnp.where(mask, jnp.exp(sm - mx), 0.0)
    return p / jnp.maximum(jnp.sum(p, axis=-1, keepdims=True), 1e-30)


def _select_blocks_t(imp_t, qpos_row, v_ref, n_live):
    ns, nq = imp_t.shape
    m_i = lax.broadcasted_iota(jnp.int32, (ns, nq), 0)
    valid = m_i * SLC_BLOCK <= qpos_row
    cur = qpos_row // SLC_BLOCK
    forced = (m_i == 0) | (m_i == cur) | (m_i == cur - 1)
    del n_live
    bits = pltpu.bitcast(jnp.where(forced, jnp.inf, imp_t), jnp.int32)
    int_min = jnp.int32(-2 ** 31)
    key = jnp.where(valid, jnp.where(bits < 0, bits ^ jnp.int32(2 ** 31 - 1), bits), int_min)
    v_ref[...] = pltpu.bitcast(key, F32)
    k = float(SLC_TOPK)

    def count_ge(t):
        terms = [jnp.where(pltpu.bitcast(v_ref[8 * s:8 * s + 8, :], jnp.int32) >= t, 1.0, 0.0)
                 for s in range(ns // 8)]
        while len(terms) > 1:
            terms = [a + b for a, b in zip(terms[0::2], terms[1::2])] + terms[len(terms) & ~1:]
        part = terms[0]
        for shift in (4, 2, 1):
            part = part + pltpu.roll(part, shift, 0)
        return part

    def step(i, t):
        cand = t + (jnp.int32(1) << (30 - i))
        return jnp.where(count_ge(cand) >= k, cand, t)

    zero = jnp.zeros((8, nq), jnp.int32)
    t0 = jnp.where(count_ge(zero) >= k, zero, int_min)
    thr = lax.fori_loop(0, 31, step, t0)[0:1, :]
    above = jnp.where(key > thr, 1.0, 0.0)
    tied = jnp.where(key == thr, 1.0, 0.0)
    room = k - jnp.sum(above, axis=0, keepdims=True)
    lower = jnp.where(lax.broadcasted_iota(jnp.int32, (ns, ns), 1) < lax.broadcasted_iota(jnp.int32, (ns, ns), 0),
                      1.0, 0.0).astype(BF16)
    ties_before = _dot(lower, tied.astype(BF16))
    chosen = jnp.where(above > 0.0, 1.0, jnp.where(ties_before < room, tied, 0.0))
    return jnp.where(valid, chosen, 0.0)


BIG = 2.0 ** 20
POS_PERIOD = SEL_TILE


def _nsa_prompt_kernel(slopes_ref, q_ref, gate_ref, qpc_ref, kct_ref, vc_ref, kst_ref, va_ref, kwt_ref, vwp_ref,
                       pos_ref, blk_ref, ovl_ref, o_ref, v_ref, kaug_ref, kwaug_ref, m_ref, acc_ref,
                       s0_ref, s1_ref, sc_ref, sw_ref, flag_ref):
    g = pl.program_id(1)
    qi = pl.program_id(2)
    nq, dh, J = NSA_NQ, NSA_HEAD_DIM, NSA_HPG
    ns, ncb = ovl_ref.shape
    L = kst_ref.shape[2]
    start = qi * nq
    slopes = [slopes_ref[g * J + j] for j in range(J)]
    rows = [slice(j * nq, (j + 1) * nq) for j in range(J)]

    @pl.when(qi == 0)
    def _():
        kaug_ref[0:dh, :] = kst_ref[0].astype(BF16)
        kaug_ref[dh:LANES, :] = pos_ref[...]
        kaug_ref[LANES:, :] = blk_ref[...]
        kwaug_ref[0:dh, 0:WINDOW] = jnp.zeros((dh, WINDOW), BF16)
        kwaug_ref[dh:LANES, 0:WINDOW] = pos_ref[:, 0:WINDOW]
        kwaug_ref[0:dh, WINDOW:] = kwt_ref[0].astype(BF16)
        kwaug_ref[dh:LANES, WINDOW:] = pos_ref[...]

    m_ref[...] = jnp.full(m_ref.shape, NEG, F32)
    acc_ref[...] = jnp.zeros(acc_ref.shape, F32)
    qb = q_ref[0] * (dh ** -0.5)
    q_all = jnp.concatenate([qb[:, j * dh:(j + 1) * dh].astype(BF16) for j in range(J)], axis=0)
    q_pos = jnp.concatenate([q_all, qpc_ref[0].astype(BF16)], axis=1)
    gates = _sigmoid(gate_ref[0, 0])

    sc_ref[...] = _dot(q_all, kct_ref[0, 0])
    rel_end = (lax.broadcasted_iota(jnp.int32, (1, ncb), 1) * CMP_STRIDE + (CMP_BLOCK - 1)) - start
    mask_c = lax.broadcasted_iota(jnp.int32, (nq, ncb), 0) >= rel_end
    rel_endf = rel_end.astype(F32)
    vc = vc_ref[0, 0]
    o_c, p_sum = [], jnp.zeros((nq, ncb), F32)
    for j in range(J):
        sm = jnp.where(mask_c, sc_ref[rows[j], :] + slopes[j] * rel_endf, NEG)
        mx = jnp.max(sm, axis=-1, keepdims=True)
        e = jnp.exp(sm - jnp.where(mx > 0.5 * NEG, mx, 0.0))
        inv = 1.0 / jnp.maximum(jnp.sum(e, axis=-1, keepdims=True), 1e-30)
        p_sum = p_sum + e * inv
        o_c.append(_dot(e.astype(BF16), vc) * inv)

    ovl = ovl_ref[...]
    imp_t = sum(_dot_nt(ovl, part) for part in _split3(p_sum))
    qpos_row = start + lax.broadcasted_iota(jnp.int32, (ns, nq), 1)
    sel_t = _select_blocks_t(imp_t, qpos_row, v_ref, start // SLC_BLOCK + nq // SLC_BLOCK)
    not_q = ((sel_t.T - 1.0) * BIG).astype(BF16)
    q_aug = jnp.concatenate([q_pos, jnp.concatenate([not_q] * J, axis=0)], axis=1)

    tk = SEL_TILE
    causal = (lax.broadcasted_iota(jnp.int32, (nq, tk), 0) - lax.broadcasted_iota(jnp.int32, (nq, tk), 1))

    last = start // tk
    bpt = tk // SLC_BLOCK
    n_flags = ns // bpt
    v_ref[...] = sel_t
    for t in range(n_flags):
        flag_ref[t] = (jnp.max(v_ref[t * bpt:(t + 1) * bpt, :]) > 0.5).astype(jnp.int32)

    def active(t):
        return jnp.where(t < last, flag_ref[jnp.minimum(t, n_flags - 1)], (t == last).astype(jnp.int32)) > 0

    def scores(t, s_ref):
        s_ref[...] = _dot(q_aug, kaug_ref[:, pl.ds(pl.multiple_of(t * tk, tk), tk)])

    def absorb(t, s_ref):
        k0 = pl.multiple_of(t * tk, tk)
        vt = va_ref[0, 0, pl.ds(k0, tk), :]
        shift = (k0 - start).astype(F32)
        allowed = causal + (start - k0) >= 0
        for j in range(J):
            sj = jnp.where(allowed, s_ref[rows[j], :], -BIG)
            c = slopes[j] * shift
            m_old = m_ref[rows[j], :]
            m_new = jnp.maximum(m_old, jnp.max(sj, axis=-1, keepdims=True) + c)
            p = jnp.exp(sj - (m_new - c))
            acc_ref[rows[j], :] = acc_ref[rows[j], :] * jnp.exp(m_old - m_new) + _dot(p.astype(BF16), vt)
            m_ref[rows[j], :] = m_new

    def step(t, s_cur, s_nxt):
        a, b = active(t), active(t + 1)

        @pl.when(a & b)
        def _():
            scores(t + 1, s_nxt)
            absorb(t, s_cur)

        @pl.when(a & jnp.logical_not(b))
        def _():
            absorb(t, s_cur)

        @pl.when(jnp.logical_not(a) & b)
        def _():
            scores(t + 1, s_nxt)

    scores(0, s0_ref)

    def tile_pair(u, carry):
        step(2 * u, s0_ref, s1_ref)
        step(2 * u + 1, s1_ref, s0_ref)
        return carry

    lax.fori_loop(0, last // 2 + 1, tile_pair, 0)

    sw_ref[...] = _dot(q_pos, kwaug_ref[:, pl.ds(pl.multiple_of(start, nq), WIN_KEYS)])
    vw = vwp_ref[0, 0, pl.ds(pl.multiple_of(start, nq), WIN_KEYS), :]
    qrow = lax.broadcasted_iota(jnp.int32, (nq, WIN_KEYS), 0)
    wcol = lax.broadcasted_iota(jnp.int32, (nq, WIN_KEYS), 1)
    mask_w = (wcol > qrow) & (wcol - WINDOW <= qrow)
    chunk = lax.broadcasted_iota(jnp.int32, (1, WIN_KEYS), 1) // WIN_CHUNK
    p0 = start - WINDOW + chunk * WIN_CHUNK
    origin = (jnp.maximum(p0, 0) // POS_PERIOD) * POS_PERIOD - start
    for j in range(J):
        bias = jnp.where(p0 >= 0, slopes[j] * origin.astype(F32), -BIG)
        sm = jnp.where(mask_w, sw_ref[rows[j], :] + bias, NEG)
        p = jnp.exp(sm - jnp.max(sm, axis=-1, keepdims=True))
        o_w = _dot(p.astype(BF16), vw) / jnp.sum(p, axis=-1, keepdims=True)
        acc = acc_ref[rows[j], :]
        o_s = acc[:, 0:dh] / jnp.maximum(acc[:, dh:dh + 1], 1e-30)
        out = (gates[:, 3 * j:3 * j + 1] * o_c[j] + gates[:, 3 * j + 1:3 * j + 2] * o_s
               + gates[:, 3 * j + 2:3 * j + 3] * o_w)
        o_ref[0, :, j * dh:(j + 1) * dh] = out.astype(o_ref.dtype)


def _slope_columns():
    s = jnp.asarray(_alibi_slopes())
    pieces = jnp.stack([p.astype(F32) for p in _split3(s)], axis=1)
    cols = jnp.concatenate([pieces * SLC_BLOCK, pieces,
                            jnp.zeros((NSA_HEADS, NSA_HEAD_DIM - 6), F32)], axis=1)
    return jnp.repeat(cols, NSA_NQ, axis=0).reshape(NSA_KV_HEADS, NSA_HPG * NSA_NQ, NSA_HEAD_DIM)


def _position_rows(L):
    t = np.arange(L) % POS_PERIOD
    rows = np.zeros((NSA_HEAD_DIM, L), np.float32)
    rows[0:3] = t // SLC_BLOCK
    rows[3:6] = t % SLC_BLOCK
    return jnp.asarray(rows, BF16)


def _block_rows(ns, L):
    return jnp.asarray((np.arange(ns)[:, None] == np.arange(L)[None, :] // SLC_BLOCK).astype(np.float32), BF16)


def _nsa_prompt(q, gates_raw, kct, vc, kv_t, va, vwp):
    b, L, _ = q.shape
    G, dh, J = NSA_KV_HEADS, NSA_HEAD_DIM, NSA_HPG
    ncb, ns = L // CMP_STRIDE, L // SLC_BLOCK
    nqb = L // NSA_NQ
    per_bg = lambda shape: pl.BlockSpec((1, 1) + shape, lambda i, g, t, s: (i, g, 0, 0))
    const = lambda shape: pl.BlockSpec(shape, lambda i, g, t, s: (0,) * len(shape))
    kv_rows = lambda w: pl.BlockSpec((1, dh, L), lambda i, g, t, s: (i, w * G + g, 0))
    grid_spec = pltpu.PrefetchScalarGridSpec(
        num_scalar_prefetch=1,
        grid=(b, G, nqb),
        in_specs=[pl.BlockSpec((1, NSA_NQ, J * dh), lambda i, g, t, s: (i, t, g)),
                  pl.BlockSpec((1, 1, NSA_NQ, 3 * J), lambda i, g, t, s: (i, g, t, 0)),
                  pl.BlockSpec((1, J * NSA_NQ, dh), lambda i, g, t, s: (g, 0, 0)),
                  per_bg((dh, ncb)), per_bg((ncb, dh)), kv_rows(2), per_bg((L, LANES)),
                  kv_rows(4), per_bg((WINDOW + L, dh)),
                  const((dh, L)), const((ns, L)), const((ns, ncb))],
        out_specs=pl.BlockSpec((1, NSA_NQ, J * dh), lambda i, g, t, s: (i, t, g)),
        scratch_shapes=[pltpu.VMEM((ns, NSA_NQ), F32),
                        pltpu.VMEM((LANES + ns, L), BF16), pltpu.VMEM((LANES, WINDOW + L), BF16),
                        pltpu.VMEM((J * NSA_NQ, 1), F32), pltpu.VMEM((J * NSA_NQ, LANES), F32),
                        pltpu.VMEM((J * NSA_NQ, SEL_TILE), F32), pltpu.VMEM((J * NSA_NQ, SEL_TILE), F32),
                        pltpu.VMEM((J * NSA_NQ, ncb), F32), pltpu.VMEM((J * NSA_NQ, WIN_KEYS), F32),
                        pltpu.SMEM((ns // (SEL_TILE // SLC_BLOCK),), jnp.int32)],
    )
    return pl.pallas_call(
        _nsa_prompt_kernel,
        grid_spec=grid_spec,
        out_shape=jax.ShapeDtypeStruct((b, L, G * J * dh), BF16),
        compiler_params=_params("arbitrary", "arbitrary", "arbitrary"),
        name="nsa_prompt",
    )(jnp.asarray(_alibi_slopes()), q, gates_raw, _slope_columns(), kct, vc, kv_t, va, kv_t, vwp,
      _position_rows(L), _block_rows(ns, L), jnp.asarray(_overlap_t(ncb, ns), BF16))


def _nsa_prompt_branch(q, kv, kv_t, small, cw):
    b, L, _ = q.shape
    G, dh = NSA_KV_HEADS, NSA_HEAD_DIM
    cmp = _compress_prompt(kv, cw).astype(BF16)
    kct = cmp[:, 0].transpose(0, 1, 3, 2)
    vc = cmp[:, 1]
    rows = lambda w: kv[:, :, w * KVW:(w + 1) * KVW].astype(BF16).reshape(b, L, G, dh).transpose(0, 2, 1, 3)
    va = jnp.concatenate([rows(3), jnp.ones((b, G, L, 1), BF16), jnp.zeros((b, G, L, LANES - dh - 1), BF16)], axis=-1)
    vwp = jnp.pad(rows(5), ((0, 0), (0, 0), (WINDOW, 0), (0, 0)))
    gates_raw = small[:, :, SSD_HEADS:SSD_HEADS + 3 * NSA_HEADS].reshape(b, L, G, 3 * NSA_HPG).transpose(0, 2, 1, 3)
    return _nsa_prompt(q, gates_raw, kct, vc, kv_t, va, vwp)


def _layer_norm(v, g, b):
    mu = jnp.mean(v, axis=-1, keepdims=True)
    c = v - mu
    var = jnp.mean(c * c, axis=-1, keepdims=True)
    return c * lax.rsqrt(var + NORM_EPS) * g + b


def _route_t(logits_t, bias_col, v_ref):
    e_n, n = logits_t.shape
    gsz = e_n // N_ROUTE_GROUPS
    scores = _sigmoid(logits_t)
    biased = scores + bias_col
    grp = []
    for a in range(N_ROUTE_GROUPS):
        blk = biased[a * gsz:(a + 1) * gsz]
        m1 = jnp.max(blk, axis=0, keepdims=True)
        cnt = jnp.sum(jnp.where(blk == m1, 1.0, 0.0), axis=0, keepdims=True)
        m2 = jnp.max(jnp.where(blk < m1, blk, -jnp.inf), axis=0, keepdims=True)
        grp.append(m1 + jnp.where(cnt >= 2.0, m1, m2))
    pieces = []
    for a in range(N_ROUTE_GROUPS):
        rank = jnp.zeros((1, n), jnp.int32)
        for b in range(N_ROUTE_GROUPS):
            if b != a:
                ahead = (grp[b] > grp[a]) | ((grp[b] == grp[a]) & (b < a))
                rank = rank + jnp.where(ahead, 1, 0)
        pieces.append(jnp.where(rank < TOPK_ROUTE_GROUPS, biased[a * gsz:(a + 1) * gsz], -jnp.inf))
    masked = jnp.concatenate(pieces, axis=0)
    v_ref[...] = masked
    e_i = lax.broadcasted_iota(jnp.int32, (e_n, n), 0)

    def body(ep, rank):
        r = v_ref[pl.ds(ep, 1), :]
        tie = jnp.where(ep < e_i, 1.0, 0.0)
        return rank + jnp.where(r > masked, 1.0, jnp.where(r == masked, tie, 0.0))

    rank = lax.fori_loop(0, e_n, body, jnp.zeros((e_n, n), F32))
    picked = jnp.where(rank < TOP_K, scores, 0.0)
    return picked / jnp.sum(picked, axis=0, keepdims=True) * ROUTE_SCALE


def _finish_kernel(x_ref, sy_ref, ny_ref, gm_ref, wsd_ref, wnd_ref, wo_ref, bo_ref, g1_ref, b1_ref,
                   wrh_ref, wrl_ref, br_ref, h_ref, hb_ref, gt_ref, slot_ref, cnt_ref, v_ref):
    gate = _sigmoid(gm_ref[...])
    mixed = (gate[:, :D_MODEL] * _dot(sy_ref[...], wsd_ref[...])
             + gate[:, D_MODEL:] * _dot(ny_ref[...], wnd_ref[...]))
    o = _dot(mixed.astype(BF16), wo_ref[...]) + bo_ref[...]
    h = _layer_norm(ALPHA * x_ref[...] + o, g1_ref[...], b1_ref[...])
    h_ref[...] = h
    hb_ref[...] = h.astype(BF16)
    h_hi, h_lo = _split2(h)
    logits_t = _dot_nt(wrh_ref[...], h_hi) + _dot_nt(wrh_ref[...], h_lo) + _dot_nt(wrl_ref[...], h_hi)
    gates_t = _route_t(logits_t, br_ref[...], v_ref)
    gt_ref[...] = gates_t
    tm = gates_t.shape[1]
    routed = gates_t > 0.0
    earlier = jnp.where(lax.broadcasted_iota(jnp.int32, (tm, tm), 0) < lax.broadcasted_iota(jnp.int32, (tm, tm), 1),
                        1.0, 0.0).astype(BF16)
    ones = jnp.where(routed, 1.0, 0.0)
    slot_ref[...] = jnp.where(routed, _dot(ones.astype(BF16), earlier) + 1.0, 0.0)
    cnt_ref[0] = jnp.broadcast_to(jnp.sum(ones, axis=1, keepdims=True), (N_EXPERTS, LANES))


def _finish(x2d, ssd_y, nsa_y, gm, fw, tm):
    t = x2d.shape[0]
    wsd, wnd, wo, bo, g1, b1, wrh, wrl, br = fw
    row = lambda w: pl.BlockSpec((tm, w), lambda i: (i, 0))
    full = lambda a: pl.BlockSpec(a.shape, lambda i: (0,) * a.ndim)
    col = pl.BlockSpec((N_EXPERTS, tm), lambda i: (0, i))
    return pl.pallas_call(
        _finish_kernel,
        grid=(t // tm,),
        in_specs=[row(D_MODEL), row(SSD_D_INNER), row(D_MODEL), row(2 * D_MODEL)] + [full(a) for a in fw],
        out_specs=[row(D_MODEL), row(D_MODEL), col, col, pl.BlockSpec((1, N_EXPERTS, LANES), lambda i: (i, 0, 0))],
        out_shape=[jax.ShapeDtypeStruct((t, D_MODEL), F32), jax.ShapeDtypeStruct((t, D_MODEL), BF16),
                   jax.ShapeDtypeStruct((N_EXPERTS, t), F32), jax.ShapeDtypeStruct((N_EXPERTS, t), F32),
                   jax.ShapeDtypeStruct((t // tm, N_EXPERTS, LANES), F32)],
        scratch_shapes=[pltpu.VMEM((N_EXPERTS, tm), F32)],
        compiler_params=_params("arbitrary"),
        name="finish",
    )(x2d, ssd_y, nsa_y, gm, *fw)


def _finish_weights(w_ssd_down, w_nsa_down, w_out, b_out, ln1_g, ln1_b, w_router, b_router):
    wr_t = w_router.astype(F32).T
    wrh = wr_t.astype(BF16)
    wrl = (wr_t - wrh.astype(F32)).astype(BF16)
    return (w_ssd_down.astype(BF16), w_nsa_down.astype(BF16), w_out.astype(BF16), b_out.astype(F32)[None, :],
            ln1_g.astype(F32)[None, :], ln1_b.astype(F32)[None, :], wrh, wrl, b_router.astype(F32)[:, None])


MOE_R = 64
MOE_ALIGN = 16
MOE_BM = 512
MOE_TAIL = MOE_BM
MOE_ZERO = MOE_BM + MOE_R
MOE_GROUP = 8


def _swiglu(x, w1, w3):
    return _silu(_dot(x, w1)) * _dot(x, w3)


def _moe_blocks(t, tt):
    rows = (t * TOP_K + (t // tt) * N_EXPERTS * (MOE_ALIGN - 1) + N_EXPERTS * (MOE_R + MOE_BM - 1))
    return pl.cdiv(rows, MOE_BM) + MOE_TAIL // MOE_BM


def _moe_plan(cnt, nb):
    aligned = (cnt + MOE_ALIGN - 1) // MOE_ALIGN * MOE_ALIGN
    before = jnp.cumsum(aligned, axis=0) - aligned
    total = jnp.sum(aligned, axis=0)
    region = (total + MOE_R + MOE_BM - 1) // MOE_BM * MOE_BM
    region_end = jnp.cumsum(region)
    start = (region_end - region)[None, :] + before
    n_pass = jnp.maximum(jnp.max((cnt + MOE_R - 1) // MOE_R, axis=1), 1)
    blk_row0 = jnp.arange(nb, dtype=region_end.dtype) * MOE_BM
    blk_exp = jnp.minimum(jnp.sum(region_end[None, :] <= blk_row0[:, None], axis=1), N_EXPERTS - 1)
    i32 = lambda a: a.astype(jnp.int32)
    return (i32(start.reshape(-1)), i32(cnt.reshape(-1)), i32(n_pass), i32(blk_exp),
            i32(region_end[-1:] // MOE_BM), i32(jnp.maximum(region_end - MOE_ZERO, 0)))


def _chunk_hits(slot_ref, g, k):
    want = (lax.broadcasted_iota(jnp.int32, (MOE_R, 1), 0) + (k * MOE_R + 1)).astype(F32)
    return [slot_ref[e:e + 1, :] == want for e in range(g * MOE_GROUP, (g + 1) * MOE_GROUP)]


def _for_runs(cnt_ref, tile, k, fn):
    def body(e, c):
        @pl.when(cnt_ref[tile * N_EXPERTS + e] > k * MOE_R)
        def _():
            fn(e)
        return c
    lax.fori_loop(0, N_EXPERTS, body, 0)


def _moe_dispatch_kernel(start_ref, cnt_ref, npass_ref, tail_ref, nused_ref, hb_ref, slot_ref, xs_ref, buf_ref,
                         zero_ref, sem_ref):
    i = pl.program_id(0)
    E, R, GR = N_EXPERTS, MOE_R, MOE_GROUP * MOE_R
    half = i % 2
    h = hb_ref[...]

    @pl.when(i == 0)
    def _():
        zero_ref[...] = jnp.zeros(zero_ref.shape, zero_ref.dtype)

        def zero_copy(e):
            dst = pl.multiple_of(tail_ref[e], MOE_ALIGN)
            return pltpu.make_async_copy(zero_ref, xs_ref.at[pl.ds(dst, MOE_ZERO), :], sem_ref.at[2])

        for parity in range(2):
            def each(fn):
                def body(u, c):
                    fn(2 * u + parity)
                    return c
                lax.fori_loop(0, E // 2, body, 0)
            each(lambda e: zero_copy(e).start())
            each(lambda e: zero_copy(e).wait())

        def zero_block(j):
            return pltpu.make_async_copy(zero_ref.at[pl.ds(0, MOE_BM), :],
                                         xs_ref.at[pl.ds(pl.multiple_of(j * MOE_BM, MOE_BM), MOE_BM), :], sem_ref.at[2])

        def unused(fn):
            def body(j, c):
                fn(j)
                return c
            lax.fori_loop(nused_ref[0], xs_ref.shape[0] // MOE_BM, body, 0)

        unused(lambda j: zero_block(j).start())
        unused(lambda j: zero_block(j).wait())

    def fill(k):
        for g in range(E // MOE_GROUP):
            onehot = jnp.concatenate([jnp.where(hit, 1.0, 0.0).astype(BF16) for hit in _chunk_hits(slot_ref, g, k)],
                                     axis=0)
            buf_ref[half, g * GR:(g + 1) * GR, :] = _dot(onehot, h).astype(BF16)

    def copy(tile, e, k, hf):
        dst = pl.multiple_of(start_ref[tile * E + e] + k * R, MOE_ALIGN)
        return pltpu.make_async_copy(buf_ref.at[hf, pl.ds(pl.multiple_of(e * R, R), R), :],
                                     xs_ref.at[pl.ds(dst, R), :], sem_ref.at[hf])

    fill(0)

    @pl.when(i > 0)
    def _():
        k_prev = npass_ref[i - 1] - 1
        _for_runs(cnt_ref, i - 1, k_prev, lambda e: copy(i - 1, e, k_prev, 1 - half).wait())

    _for_runs(cnt_ref, i, 0, lambda e: copy(i, e, 0, half).start())

    def more(k, c):
        _for_runs(cnt_ref, i, k - 1, lambda e: copy(i, e, k - 1, half).wait())
        fill(k)
        _for_runs(cnt_ref, i, k, lambda e: copy(i, e, k, half).start())
        return c

    lax.fori_loop(1, npass_ref[i], more, 0)

    @pl.when(i == pl.num_programs(0) - 1)
    def _():
        k_last = npass_ref[i] - 1
        _for_runs(cnt_ref, i, k_last, lambda e: copy(i, e, k_last, half).wait())


def _moe_dispatch(hb, slots, plan, nb, tt):
    t = hb.shape[0]
    start, cnt, n_pass, _, n_used, tail = plan
    rows = nb * MOE_BM
    grid_spec = pltpu.PrefetchScalarGridSpec(
        num_scalar_prefetch=5,
        grid=(t // tt,),
        in_specs=[pl.BlockSpec((tt, D_MODEL), lambda i, *_: (i, 0)),
                  pl.BlockSpec((N_EXPERTS, tt), lambda i, *_: (0, i))],
        out_specs=pl.BlockSpec(memory_space=pl.ANY),
        scratch_shapes=[pltpu.VMEM((2, N_EXPERTS * MOE_R, D_MODEL), BF16), pltpu.VMEM((MOE_ZERO, D_MODEL), BF16),
                        pltpu.SemaphoreType.DMA((3,))],
    )
    return pl.pallas_call(
        _moe_dispatch_kernel,
        grid_spec=grid_spec,
        out_shape=jax.ShapeDtypeStruct((rows, D_MODEL), BF16),
        compiler_params=_params("arbitrary"),
        name="moe_dispatch",
    )(start, cnt, n_pass, tail, n_used, hb, slots)


def _moe_ffn_kernel(blk_exp_ref, nused_ref, x_ref, w1_ref, w3_ref, w2_ref, y_ref):
    del blk_exp_ref
    in_use = pl.program_id(0) < nused_ref[0]

    @pl.when(in_use)
    def _():
        a = _swiglu(x_ref[...], w1_ref[0], w3_ref[0])
        y_ref[...] = _dot(a.astype(BF16), w2_ref[0]).astype(y_ref.dtype)

    @pl.when(jnp.logical_not(in_use))
    def _():
        y_ref[...] = jnp.zeros(y_ref.shape, y_ref.dtype)


def _moe_ffn(xs, plan, nb, w1, w3, w2):
    _, _, _, blk_exp, n_used, _ = plan
    grid_spec = pltpu.PrefetchScalarGridSpec(
        num_scalar_prefetch=2,
        grid=(nb,),
        in_specs=[pl.BlockSpec((MOE_BM, D_MODEL), lambda j, be, nu: (jnp.minimum(j, nu[0] - 1), 0)),
                  pl.BlockSpec((1, D_MODEL, EXPERT_FF), lambda j, be, nu: (be[j], 0, 0)),
                  pl.BlockSpec((1, D_MODEL, EXPERT_FF), lambda j, be, nu: (be[j], 0, 0)),
                  pl.BlockSpec((1, EXPERT_FF, D_MODEL), lambda j, be, nu: (be[j], 0, 0))],
        out_specs=pl.BlockSpec((MOE_BM, D_MODEL), lambda j, be, nu: (j, 0)),
    )
    return pl.pallas_call(
        _moe_ffn_kernel,
        grid_spec=grid_spec,
        out_shape=jax.ShapeDtypeStruct(xs.shape, BF16),
        compiler_params=_params("arbitrary"),
        name="moe_ffn",
    )(blk_exp, n_used, xs, w1, w3, w2)


def _moe_combine_kernel(start_ref, npass_ref, hb_ref, h_ref, gate_ref, slot_ref, ys_ref,
                        ws1_ref, ws3_ref, ws2_ref, g2_ref, b2_ref, y_ref, buf_ref, sem_ref):
    i = pl.program_id(0)
    n = pl.num_programs(0)
    E, R, GR = N_EXPERTS, MOE_R, MOE_GROUP * MOE_R
    half = i % 2

    def copy(tile, e, k, hf):
        src = pl.multiple_of(start_ref[tile * E + e] + k * R, MOE_ALIGN)
        return pltpu.make_async_copy(ys_ref.at[pl.ds(src, R), :],
                                     buf_ref.at[hf, pl.ds(pl.multiple_of(e * R, R), R), :], sem_ref.at[hf])

    def all_runs(fn):
        def body(e, c):
            fn(e)
            return c
        lax.fori_loop(0, E, body, 0)

    @pl.when(i == 0)
    def _():
        all_runs(lambda e: copy(0, e, 0, 0).start())

    @pl.when(i + 1 < n)
    def _():
        all_runs(lambda e: copy(i + 1, e, 0, 1 - half).start())

    def gathered(k):
        acc = jnp.zeros(y_ref.shape, F32)
        for g in range(E // MOE_GROUP):
            rows = slice(g * GR, (g + 1) * GR)
            hits = _chunk_hits(slot_ref, g, k)
            gate = jnp.concatenate(
                [jnp.sum(jnp.where(hit, gate_ref[e:e + 1, :], 0.0), axis=1, keepdims=True)
                 for hit, e in zip(hits, range(g * MOE_GROUP, (g + 1) * MOE_GROUP))], axis=0)
            onehot = jnp.concatenate([jnp.where(hit, 1.0, 0.0).astype(BF16) for hit in hits], axis=0)
            weighted = (buf_ref[half, rows, :].astype(F32) * gate).astype(BF16)
            acc = acc + _dot_tn(onehot, weighted)
        return acc

    all_runs(lambda e: copy(i, e, 0, half).wait())
    routed = gathered(0)

    def more(k, acc):
        all_runs(lambda e: copy(i, e, k, half).start())
        all_runs(lambda e: copy(i, e, k, half).wait())
        return acc + gathered(k)

    routed = lax.fori_loop(1, npass_ref[i], more, routed)
    x = hb_ref[...]
    shared = _dot(_swiglu(x, ws1_ref[...], ws3_ref[...]).astype(BF16), ws2_ref[...])
    y_ref[...] = _layer_norm(ALPHA * h_ref[...] + (routed + shared), g2_ref[...], b2_ref[...])


def _moe_combine(hb, h, gates_t, slots, ys, plan, mw, tt):
    t = h.shape[0]
    start, _, n_pass, _, _, _ = plan
    _, _, _, ws1, ws3, ws2, g2, b2 = mw
    weights = (ws1, ws3, ws2, g2, b2)
    row = lambda w: pl.BlockSpec((tt, w), lambda i, *_: (i, 0))
    col = pl.BlockSpec((N_EXPERTS, tt), lambda i, *_: (0, i))
    full = lambda a: pl.BlockSpec(a.shape, lambda i, *_: (0,) * a.ndim)
    grid_spec = pltpu.PrefetchScalarGridSpec(
        num_scalar_prefetch=2,
        grid=(t // tt,),
        in_specs=[row(D_MODEL), row(D_MODEL), col, col, pl.BlockSpec(memory_space=pl.ANY)]
        + [full(a) for a in weights],
        out_specs=row(D_MODEL),
        scratch_shapes=[pltpu.VMEM((2, N_EXPERTS * MOE_R, D_MODEL), BF16), pltpu.SemaphoreType.DMA((2,))],
    )
    return pl.pallas_call(
        _moe_combine_kernel,
        grid_spec=grid_spec,
        out_shape=jax.ShapeDtypeStruct((t, D_MODEL), F32),
        compiler_params=_params("arbitrary"),
        name="moe_combine",
    )(start, n_pass, hb, h, gates_t, slots, ys, *weights)


def _moe_weights(w_e1, w_e3, w_e2, w_s1, w_s3, w_s2, ln2_g, ln2_b):
    return (w_e1.astype(BF16), w_e3.astype(BF16), w_e2.astype(BF16), w_s1.astype(BF16), w_s3.astype(BF16),
            w_s2.astype(BF16), ln2_g.astype(F32)[None, :], ln2_b.astype(F32)[None, :])


def _finish_and_moe(x2d, ssd_y, nsa_y, gm, fw, mw, tt):
    t = x2d.shape[0]
    h, hb, gates_t, slots, run_len = _finish(x2d, ssd_y, nsa_y, gm, fw, tt)
    nb = _moe_blocks(t, tt)
    plan = _moe_plan(run_len[:, :, 0].astype(jnp.int32), nb)
    xs = _moe_dispatch(hb, slots, plan, nb, tt)
    ys = _moe_ffn(xs, plan, nb, mw[0], mw[1], mw[2])
    return _moe_combine(hb, h, gates_t, slots, ys, plan, mw, tt)


def _ssd_sample_kernel(xbc_ref, sc_ref, z_ref, sm_ref, st_ref, cw_ref, cb_ref, dtb_ref, alog_ref, dskip_ref,
                       nw_ref, e_ref, y_ref, sto_ref, xdt_t_ref, dec_t_ref, b_ref, c_ref, xs_ref, yt_ref):
    i = pl.program_id(0)
    db = xbc_ref.shape[0]
    G, N = SSD_GROUPS, SSD_STATE
    GW = SSD_HPG * SSD_HEAD_DIM

    @pl.when(i == 0)
    def _():
        conv = cb_ref[...] + xbc_ref[...] * cw_ref[SSD_CONV - 1:SSD_CONV, :]
        for k in range(SSD_CONV - 1):
            conv = conv + sc_ref[k] * cw_ref[k:k + 1, :]
        act = _silu(conv)
        xs = act[:, :SSD_D_INNER]
        xs_ref[...] = xs
        b_ref[...] = act[:, SSD_D_INNER:SSD_D_INNER + G * N].astype(BF16)
        c_ref[...] = act[:, SSD_D_INNER + G * N:].astype(BF16)
        dt = _softplus(sm_ref[...] + dtb_ref[...])
        dec = jnp.exp(dt * (-jnp.exp(alog_ref[...])))
        e_mat = e_ref[...]
        xdt_t_ref[...] = (xs * _dot_exact_rhs(dt, e_mat)).T.astype(BF16)
        dec_t_ref[...] = _dot_exact_rhs(dec, e_mat).T
        yt_ref[...] = jnp.zeros_like(yt_ref)

    is_row = lax.broadcasted_iota(jnp.int32, (db, N), 0) == i
    onehot = jnp.where(is_row, 1.0, 0.0).astype(BF16)
    is_lane = lax.broadcasted_iota(jnp.int32, (GW, db), 1) == i
    for g in range(G):
        rows = slice(g * GW, (g + 1) * GW)
        b_g = jnp.where(is_row, b_ref[:, g * N:(g + 1) * N], jnp.zeros((), BF16))
        contrib = _dot(xdt_t_ref[rows, :], b_g)
        decay = _dot_exact_rhs(dec_t_ref[rows, :], onehot)
        new = st_ref[0, rows, :] * decay + contrib
        sto_ref[0, rows, :] = new
        y_all = _dot_nt(new.astype(BF16), c_ref[:, g * N:(g + 1) * N])
        yt_ref[rows, :] += jnp.where(is_lane, y_all, 0.0)

    @pl.when(i == pl.num_programs(0) - 1)
    def _():
        y = yt_ref[...].T + dskip_ref[...] * xs_ref[...]
        y = y * _silu(z_ref[...])
        for g in range(G):
            sl = slice(g * GW, (g + 1) * GW)
            yg = y[:, sl]
            rs = lax.rsqrt(jnp.mean(yg * yg, axis=-1, keepdims=True) + NORM_EPS)
            y_ref[:, sl] = (yg * rs * nw_ref[:, sl]).astype(y_ref.dtype)


def _ssd_sample(xbc, state_conv, z, small, state_ssm, conv_w, conv_b, dt_bias, a_log, d_skip, norm_w):
    db = xbc.shape[0]
    pad = LANES - SSD_HEADS
    dtb = jnp.pad(dt_bias.astype(F32), (0, pad))[None, :]
    alog = jnp.pad(a_log.astype(F32), (0, pad))[None, :]
    dskip = jnp.repeat(d_skip.astype(F32), SSD_HEAD_DIM)[None, :]
    sc = state_conv.astype(F32).transpose(1, 0, 2)
    st = state_ssm.reshape(db, SSD_D_INNER, SSD_STATE)
    args = (xbc, sc, z, small, st, conv_w.astype(F32), conv_b.astype(F32)[None, :], dtb, alog, dskip,
            norm_w.astype(F32)[None, :], _head_expand_matrix())
    full = lambda a: pl.BlockSpec(a.shape, lambda i: (0,) * a.ndim)
    st_spec = pl.BlockSpec((1, SSD_D_INNER, SSD_STATE), lambda i: (i, 0, 0))
    y, st_new = pl.pallas_call(
        _ssd_sample_kernel,
        grid=(db,),
        in_specs=[full(a) for a in args[:4]] + [st_spec] + [full(a) for a in args[5:]],
        out_specs=[pl.BlockSpec((db, SSD_D_INNER), lambda i: (0, 0)), st_spec],
        out_shape=[jax.ShapeDtypeStruct((db, SSD_D_INNER), BF16),
                   jax.ShapeDtypeStruct((db, SSD_D_INNER, SSD_STATE), F32)],
        scratch_shapes=[pltpu.VMEM((SSD_D_INNER, db), BF16), pltpu.VMEM((SSD_D_INNER, db), F32),
                        pltpu.VMEM((db, SSD_GROUPS * SSD_STATE), BF16),
                        pltpu.VMEM((db, SSD_GROUPS * SSD_STATE), BF16),
                        pltpu.VMEM((db, SSD_D_INNER), F32), pltpu.VMEM((SSD_D_INNER, db), F32)],
        compiler_params=_params("arbitrary"),
        name="ssd_sample",
    )(*args)
    return y, st_new.reshape(state_ssm.shape)


def _nsa_sample_kernel(n_pages, *refs):
    pt_ref = refs[0]
    page_refs = refs[1:1 + n_pages]
    (win_ref, q_ref, kvn_ref, gate_ref, slope_ref, ovl_ref, w1ab_ref, w1flat_ref, pe_ref, b1_ref, w2_ref,
     b2_ref, o_ref, wino_ref, cmp_rows_ref, ks_ref, vs_ref, kw_ref, vw_ref, bsh_ref, kc_ref, vc_ref) = refs[1 + n_pages:]
    del pt_ref
    H, dh, G = NSA_HEADS, NSA_HEAD_DIM, NSA_KV_HEADS
    past = n_pages * PAGE_SIZE
    ncb = past // CMP_STRIDE
    nsb = ovl_ref.shape[1]
    wbuf = win_ref.shape[3]
    tiles = KVW // LANES

    spp = PAGE_SIZE // CMP_STRIDE
    src_row = lax.broadcasted_iota(jnp.int32, (PAGE_SIZE, PAGE_SIZE), 1)
    regroup = jnp.where(lax.broadcasted_iota(jnp.int32, (PAGE_SIZE, PAGE_SIZE), 0)
                        == spp * (src_row % CMP_STRIDE) + src_row // CMP_STRIDE, 1.0, 0.0).astype(BF16)
    for p in range(n_pages):
        r = slice(p * PAGE_SIZE, (p + 1) * PAGE_SIZE)
        for w in range(2):
            rows_wp = _dot_nt(regroup, page_refs[p][0, w].astype(BF16))
            for t in range(tiles):
                for l in range(CMP_STRIDE):
                    cmp_rows_ref[w * tiles + t, l, p * spp:(p + 1) * spp, :] = (
                        rows_wp[l * spp:(l + 1) * spp, t * LANES:(t + 1) * LANES])
        ks_ref[:, r] = page_refs[p][0, 2].astype(BF16)
        vs_ref[:, r] = page_refs[p][0, 3].astype(BF16)
    new8 = jnp.broadcast_to(kvn_ref[0], (8, 6 * KVW))
    first = lax.broadcasted_iota(jnp.int32, (KVW, LANES), 1) == 0

    def new_col(w):
        return new8[:, w * KVW:(w + 1) * KVW].T[:, 0:1]

    put = lambda w: jnp.where(first, new_col(w), 0.0).astype(BF16)
    ks_ref[:, past:past + LANES] = put(2)
    vs_ref[:, past:past + LANES] = put(3)
    kw_ref[:, 0:wbuf] = win_ref[0, 0].astype(BF16)
    vw_ref[:, 0:wbuf] = win_ref[0, 1].astype(BF16)
    kw_ref[:, wbuf:wbuf + LANES] = put(4)
    vw_ref[:, wbuf:wbuf + LANES] = put(5)
    last = lax.broadcasted_iota(jnp.int32, (KVW, wbuf), 1) == wbuf - 1
    for w in range(2):
        wino_ref[0, w] = jnp.where(last, new_col(4 + w), pltpu.roll(win_ref[0, w], wbuf - 1, 1))

    def compress(w, dst):
        for t in range(tiles):
            def load_strip(l, t=t):
                return cmp_rows_ref[w * tiles + t, l]

            def store(g, val, t=t):
                c0 = (t * GPT + g) * dh
                dst[:, c0:c0 + dh] = val.astype(BF16)

            sub = lambda ref: ref.at[pl.ds(w, 1)]
            _compress_body(load_strip, ncb, sub(w1ab_ref), sub(w1flat_ref), sub(pe_ref), sub(b1_ref), sub(w2_ref),
                           sub(b2_ref), bsh_ref.at[w], store)

    compress(0, kc_ref)

    head_r = lax.broadcasted_iota(jnp.int32, (H, KVW), 0) // NSA_HPG
    lane_g = lax.broadcasted_iota(jnp.int32, (H, KVW), 1) // dh
    diag = head_r == lane_g
    q16 = q_ref[0] * (dh ** -0.5)
    q_bd = jnp.where(diag, jnp.concatenate([q16] * G, axis=1), 0.0).astype(BF16)
    slopes = slope_ref[...]
    gates = _sigmoid(gate_ref[0])

    def fold(o):
        o = jnp.where(diag, o, 0.0)
        return sum(o[:, g * dh:(g + 1) * dh] for g in range(G))

    d_c = past - (lax.broadcasted_iota(jnp.int32, (H, ncb), 1) * CMP_STRIDE + (CMP_BLOCK - 1))
    p_c = _masked_softmax_rows(_dot_nt(q_bd, kc_ref[...]) - slopes * d_c.astype(F32), d_c >= 0)
    compress(1, vc_ref)
    o_c = fold(_dot(p_c.astype(BF16), vc_ref[...]))

    p_grp = jnp.concatenate(
        [jnp.sum(p_c[g * NSA_HPG:(g + 1) * NSA_HPG], axis=0, keepdims=True) for g in range(G)]
        + [jnp.zeros((8 - G, ncb), F32)], axis=0)
    imp = sum(_dot(part, ovl_ref[...]) for part in _split3(p_grp))
    m_l = lax.broadcasted_iota(jnp.int32, (8, nsb), 1)
    valid = m_l * SLC_BLOCK <= past
    cur = past // SLC_BLOCK
    forced = (m_l == 0) | (m_l == cur) | (m_l == cur - 1)
    v = jnp.where(valid, jnp.where(forced, jnp.inf, imp), -jnp.inf)
    v_t = v.T
    mp_i = lax.broadcasted_iota(jnp.int32, (nsb, nsb), 0)
    m_i = lax.broadcasted_iota(jnp.int32, (nsb, nsb), 1)
    sel_rows = []
    for g in range(G):
        v_col = jnp.broadcast_to(v_t[:, g:g + 1], (nsb, nsb))
        v_row = jnp.broadcast_to(v[g:g + 1, :], (nsb, nsb))
        ahead = (v_col > v_row) | ((v_col == v_row) & (mp_i < m_i))
        rank = jnp.sum(jnp.where(ahead, 1.0, 0.0), axis=0, keepdims=True)
        sel = jnp.where((rank < SLC_TOPK) & valid[0:1, :], 1.0, 0.0)
        sel_rows.append(jnp.broadcast_to(sel, (NSA_HPG, nsb)))
    sel_h = jnp.concatenate(sel_rows, axis=0).astype(BF16)

    nk = past + LANES
    expand = jnp.where(lax.broadcasted_iota(jnp.int32, (nsb, nk), 0)
                       == lax.broadcasted_iota(jnp.int32, (nsb, nk), 1) // SLC_BLOCK, 1.0, 0.0).astype(BF16)
    d_s = past - lax.broadcasted_iota(jnp.int32, (H, nk), 1)
    ok = (_dot(sel_h, expand) > 0.5) & (d_s >= 0)
    p_s = _masked_softmax_rows(_dot(q_bd, ks_ref[...]) - slopes * d_s.astype(F32), ok)
    o_s = fold(_dot_nt(p_s.astype(BF16), vs_ref[...]))

    nw = wbuf + LANES
    d_w = wbuf - lax.broadcasted_iota(jnp.int32, (H, nw), 1)
    p_w = _masked_softmax_rows(_dot(q_bd, kw_ref[...]) - slopes * d_w.astype(F32), (d_w >= 0) & (d_w < WINDOW))
    o_w = fold(_dot_nt(p_w.astype(BF16), vw_ref[...]))

    o_ref[0] = (gates[:, 0:1] * o_c + gates[:, 1:2] * o_s + gates[:, 2:3] * o_w).astype(o_ref.dtype)


def _overlap_sample(ncb, nsb):
    i = np.arange(ncb)[:, None] * CMP_STRIDE
    m = np.arange(nsb)[None, :] * SLC_BLOCK
    ok = (i < m + SLC_BLOCK) & (i + CMP_BLOCK > m) & (np.arange(ncb)[:, None] < ncb - 1)
    return ok.astype(np.float32)


def _nsa_sample(q, kv_new, small, cache_kv_paged, page_table, cache_kv_win, cw):
    db = q.shape[0]
    H, dh, G = NSA_HEADS, NSA_HEAD_DIM, NSA_KV_HEADS
    n_pages = page_table.shape[1]
    past = n_pages * PAGE_SIZE
    wbuf = cache_kv_win.shape[1]
    ncb = past // CMP_STRIDE
    nsb = LANES * pl.cdiv(pl.cdiv(past + 1, SLC_BLOCK), LANES)
    pages = cache_kv_paged.transpose(0, 2, 3, 4, 1).reshape(cache_kv_paged.shape[0], 4, KVW, PAGE_SIZE)
    win = cache_kv_win.transpose(0, 2, 3, 4, 1).reshape(db, 2, KVW, wbuf)
    gates_raw = small[:, SSD_HEADS:SSD_HEADS + 3 * H].reshape(db, H, 3)
    slopes = jnp.asarray(_alibi_slopes())[:, None]
    ovl = jnp.asarray(_overlap_sample(ncb, nsb), BF16)
    consts = (slopes, ovl) + tuple(cw)
    per_seq = lambda shape: pl.BlockSpec((1,) + shape, lambda i, pt: (i,) + (0,) * len(shape))
    full = lambda a: pl.BlockSpec(a.shape, lambda i, pt: (0,) * a.ndim)
    page_spec = lambda p: pl.BlockSpec((1, 4, KVW, PAGE_SIZE), lambda i, pt: (pt[i, p], 0, 0, 0))
    grid_spec = pltpu.PrefetchScalarGridSpec(
        num_scalar_prefetch=1,
        grid=(db,),
        in_specs=[page_spec(p) for p in range(n_pages)]
        + [per_seq((2, KVW, wbuf)), per_seq((H, dh)), per_seq((1, 6 * KVW)), per_seq((H, 3))]
        + [full(a) for a in consts],
        out_specs=[per_seq((H, dh)), per_seq((2, KVW, wbuf))],
        scratch_shapes=[pltpu.VMEM((2 * KVW // LANES, CMP_STRIDE, ncb, LANES), F32),
                        pltpu.VMEM((KVW, past + LANES), BF16), pltpu.VMEM((KVW, past + LANES), BF16),
                        pltpu.VMEM((KVW, wbuf + LANES), BF16), pltpu.VMEM((KVW, wbuf + LANES), BF16),
                        pltpu.VMEM((2, ncb + 8, CMP_HIDDEN), F32),
                        pltpu.VMEM((ncb, KVW), BF16), pltpu.VMEM((ncb, KVW), BF16)],
    )
    out, win_new = pl.pallas_call(
        functools.partial(_nsa_sample_kernel, n_pages),
        grid_spec=grid_spec,
        out_shape=[jax.ShapeDtypeStruct((db, H, dh), BF16), jax.ShapeDtypeStruct((db, 2, KVW, wbuf), cache_kv_win.dtype)],
        compiler_params=_params("arbitrary"),
        name="nsa_sample",
    )(page_table.astype(jnp.int32), *([pages] * n_pages), win, q.reshape(db, H, dh), kv_new.reshape(db, 1, 6 * KVW),
      gates_raw, *consts)
    win_new = win_new.reshape(db, 2, G, dh, wbuf).transpose(0, 4, 1, 2, 3)
    return out.reshape(db, H * dh), win_new


def kernel(x_prompt, x_sample, cache_kv_paged, cache_kv_win, state_ssm, state_conv, page_table, w_in, b_in, conv_w, conv_b, dt_bias, a_log, d_skip, ssd_norm_w, cmp_w1, cmp_b1, cmp_w2, cmp_b2, cmp_pe, w_ssd_down, w_nsa_down, w_out, b_out, ln1_g, ln1_b, w_router, b_router, w_e1, w_e3, w_e2, w_s1, w_s3, w_s2, ln2_g, ln2_b):
    b, L, _ = x_prompt.shape
    db = x_sample.shape[0]
    G, dh = NSA_KV_HEADS, NSA_HEAD_DIM
    wbuf = cache_kv_win.shape[1]
    wp, bp = _pack_w_in(w_in, b_in)
    cw = _compress_weights(cmp_w1, cmp_b1, cmp_w2, cmp_b2, cmp_pe)
    fw = _finish_weights(w_ssd_down, w_nsa_down, w_out, b_out, ln1_g, ln1_b, w_router, b_router)
    mw = _moe_weights(w_e1, w_e3, w_e2, w_s1, w_s3, w_s2, ln2_g, ln2_b)
    ssd_w = (conv_w, conv_b, dt_bias, a_log, d_skip, ssd_norm_w)
    tm = min(256, b * L)
    per_seq = lambda a: a.reshape(b, L, a.shape[-1])

    xp = x_prompt.reshape(b * L, D_MODEL)
    z, xbc, q, kv, gm, small = _in_proj(xp, wp, bp, tm)
    ssd_y, st_t = _ssd_prompt(per_seq(xbc), per_seq(z), per_seq(small), *ssd_w)
    kv_t = _kv_channel_major(x_prompt, w_in, b_in, min(512, L))
    nsa_y = _nsa_prompt_branch(per_seq(q), per_seq(kv), kv_t, per_seq(small), cw)
    yp = _finish_and_moe(xp, ssd_y.reshape(b * L, -1), nsa_y.reshape(b * L, -1), gm, fw, mw, tm).reshape(b, L, D_MODEL)
    kv6_t = kv_t.reshape(b, 6, G, dh, L)
    kv_rows_p = kv6_t[:, :4].transpose(0, 4, 1, 2, 3)
    win_p = kv6_t[:, 4:, :, :, L - wbuf:].transpose(0, 4, 1, 2, 3)
    ssm_p = st_t.reshape(b, SSD_STATE, SSD_HEADS, SSD_HEAD_DIM).transpose(0, 2, 3, 1).astype(state_ssm.dtype)
    conv_p = per_seq(xbc)[:, L - (SSD_CONV - 1):]

    xs = x_sample.reshape(db, D_MODEL)
    z, xbc, q, kv, gm, small = _in_proj(xs, wp, bp, db)
    ssd_y, ssm_s = _ssd_sample(xbc, state_conv, z, small, state_ssm, *ssd_w)
    nsa_y, win_s = _nsa_sample(q, kv, small, cache_kv_paged, page_table, cache_kv_win, cw)
    ys = _finish_and_moe(xs, ssd_y, nsa_y, gm, fw, mw, db).reshape(db, 1, D_MODEL)
    kv6 = kv.reshape(db, 1, 6, G, dh)
    kv_rows_s = kv6[:, :, :4]
    conv_s = jnp.concatenate([state_conv[:, 1:].astype(xbc.dtype), xbc[:, None, :]], axis=1)
    return (yp, ys, kv_rows_p, kv_rows_s, win_p, win_s, ssm_p, ssm_s, conv_p, conv_s)
```

```python
import functools

import numpy as np
import jax
import jax.numpy as jnp
from jax import lax
from jax.experimental import pallas as pl
from jax.experimental.pallas import tpu as pltpu

F32 = jnp.float32
BF16 = jnp.bfloat16

D_MODEL = 1024
PAGE_SIZE = 128
SSD_D_INNER = 2048
SSD_HEAD_DIM = 64
SSD_HEADS = 32
SSD_GROUPS = 4
SSD_HPG = 8
SSD_STATE = 128
SSD_CONV = 4
SSD_CHUNK = 128
SSD_CONV_CH = SSD_D_INNER + 2 * SSD_GROUPS * SSD_STATE
NSA_HEAD_DIM = 64
NSA_HEADS = 16
NSA_KV_HEADS = 4
NSA_HPG = 4
CMP_BLOCK = 32
CMP_STRIDE = 16
CMP_HIDDEN = 128
SLC_BLOCK = 64
SLC_TOPK = 16
WINDOW = 512
NSA_NQ = 128
N_EXPERTS = 64
TOP_K = 8
N_ROUTE_GROUPS = 8
TOPK_ROUTE_GROUPS = 4
EXPERT_FF = 256
ROUTE_SCALE = 2.5
ALPHA = 2.0 ** 0.25
NORM_EPS = 1e-5
IN_SPLITS = (SSD_D_INNER, SSD_CONV_CH, SSD_HEADS, NSA_HEADS * NSA_HEAD_DIM,
             6 * NSA_KV_HEADS * NSA_HEAD_DIM, 3 * NSA_HEADS, 2 * D_MODEL)
IN_OFFSETS = tuple(int(v) for v in np.cumsum(IN_SPLITS)[:-1])

LANES = 128
VMEM_LIMIT = 56 * 1024 * 1024

SEG_Z = (0, 2048)
SEG_XBC = (2048, 3072)
SEG_Q = (5120, 1024)
SEG_KV = (6144, 1536)
SEG_GM = (7680, 2048)
SEG_SMALL = (9728, 128)
N_PACKED = 9856


def _params(*sem):
    return pltpu.CompilerParams(dimension_semantics=sem, vmem_limit_bytes=VMEM_LIMIT)


def _silu(x):
    return x * (1.0 / (1.0 + jnp.exp(-x)))


def _sigmoid(x):
    return 1.0 / (1.0 + jnp.exp(-x))


def _softplus(x):
    return jnp.maximum(x, 0.0) + jnp.log(1.0 + jnp.exp(-jnp.abs(x)))


def _split2(x):
    hi = x.astype(BF16)
    lo = (x - hi.astype(F32)).astype(BF16)
    return hi, lo


def _split3(x):
    hi = x.astype(BF16)
    r = x - hi.astype(F32)
    mid = r.astype(BF16)
    lo = (r - mid.astype(F32)).astype(BF16)
    return hi, mid, lo


def _dot(a, b):
    return jnp.dot(a, b, preferred_element_type=F32)


def _dot_nt(a, b):
    return lax.dot_general(a, b, (((1,), (1,)), ((), ())), preferred_element_type=F32)


def _dot_tn(a, b):
    return lax.dot_general(a, b, (((0,), (0,)), ((), ())), preferred_element_type=F32)


def _dot_exact_rhs(x, sel):
    a, b, c = _split3(x)
    return _dot(a, sel) + _dot(b, sel) + _dot(c, sel)


def _pack_w_in(w_in, b_in):
    def pack(m):
        z, xbc, dt, q, kv, gn, gm = jnp.split(m, list(IN_OFFSETS), axis=-1)
        pad = jnp.zeros(m.shape[:-1] + (LANES - SSD_HEADS - 3 * NSA_HEADS,), m.dtype)
        return jnp.concatenate([z, xbc, q, kv, gm, dt, gn, pad], axis=-1)
    return pack(w_in).astype(BF16), pack(b_in[None, :])


def _in_proj_kernel(x_ref, w_ref, b_ref, z_ref, xbc_ref, q_ref, kv_ref, gm_ref, sm_ref):
    x = x_ref[...].astype(BF16)
    for ref, (off, width) in ((z_ref, SEG_Z), (xbc_ref, SEG_XBC), (q_ref, SEG_Q), (kv_ref, SEG_KV),
                              (gm_ref, SEG_GM), (sm_ref, SEG_SMALL)):
        ref[...] = _dot(x, w_ref[:, off:off + width]) + b_ref[:, off:off + width]


def _in_proj(x2d, w_packed, b_packed, tm):
    t = x2d.shape[0]
    segs = (SEG_Z, SEG_XBC, SEG_Q, SEG_KV, SEG_GM, SEG_SMALL)
    return pl.pallas_call(
        _in_proj_kernel,
        grid=(t // tm,),
        in_specs=[pl.BlockSpec((tm, D_MODEL), lambda i: (i, 0)),
                  pl.BlockSpec((D_MODEL, N_PACKED), lambda i: (0, 0), pipeline_mode=pl.Buffered(1)),
                  pl.BlockSpec((1, N_PACKED), lambda i: (0, 0))],
        out_specs=[pl.BlockSpec((tm, w), lambda i: (i, 0)) for _, w in segs],
        out_shape=[jax.ShapeDtypeStruct((t, w), F32) for _, w in segs],
        compiler_params=_params("arbitrary"),
        name="in_proj",
    )(x2d, w_packed, b_packed)


def _kv_channel_major_kernel(x_ref, wt_ref, bt_ref, o_ref):
    o_ref[0] = _dot_nt(wt_ref[...], x_ref[0].astype(BF16)) + bt_ref[...]


def _kv_channel_major(x, w_in, b_in, tm):
    b, L, _ = x.shape
    lo, n = IN_OFFSETS[3], IN_SPLITS[4]
    wt = w_in[:, lo:lo + n].T.astype(BF16)
    bt = b_in[lo:lo + n].astype(F32)[:, None]
    return pl.pallas_call(
        _kv_channel_major_kernel,
        grid=(b, L // tm),
        in_specs=[pl.BlockSpec((1, tm, D_MODEL), lambda i, j: (i, j, 0)),
                  pl.BlockSpec((n, D_MODEL), lambda i, j: (0, 0)),
                  pl.BlockSpec((n, 1), lambda i, j: (0, 0))],
        out_specs=pl.BlockSpec((1, n, tm), lambda i, j: (i, 0, j)),
        out_shape=jax.ShapeDtypeStruct((b, n, L), F32),
        compiler_params=_params("arbitrary", "arbitrary"),
        name="kv_channel_major",
    )(x, wt, bt)


def _head_expand_matrix():
    h = np.arange(LANES)[:, None]
    c = np.arange(SSD_D_INNER)[None, :] // SSD_HEAD_DIM
    return jnp.asarray((h == c).astype(np.float32), BF16)


def _ssd_prompt_kernel(xbc_ref, z_ref, sm_ref, cw_ref, cb_ref, dtb_ref, alog_ref, dskip_ref, nw_ref, e_ref,
                       y_ref, st_ref, xh_ref, state_ref, ybuf_ref):
    c = pl.program_id(1)
    L = SSD_CHUNK
    G, N, P = SSD_GROUPS, SSD_STATE, SSD_HEAD_DIM
    GW = SSD_HPG * P

    @pl.when(c == 0)
    def _():
        xh_ref[0:8, :] = jnp.zeros((8, SSD_CONV_CH), F32)
        state_ref[...] = jnp.zeros_like(state_ref)

    xh_ref[8:8 + L, :] = xbc_ref[0]
    conv = cb_ref[...] + xh_ref[5:5 + L, :] * cw_ref[0:1, :]
    for k in range(1, SSD_CONV):
        conv = conv + xh_ref[5 + k:5 + k + L, :] * cw_ref[k:k + 1, :]
    xh_ref[0:8, :] = xh_ref[L:L + 8, :]
    act = _silu(conv)
    xs = act[:, :SSD_D_INNER]
    bm = act[:, SSD_D_INNER:SSD_D_INNER + G * N].astype(BF16)
    cm = act[:, SSD_D_INNER + G * N:]

    dt = _softplus(sm_ref[0] + dtb_ref[...])
    da = dt * (-jnp.exp(alog_ref[...]))
    row_i = lax.broadcasted_iota(jnp.int32, (L, L), 0)
    col_i = lax.broadcasted_iota(jnp.int32, (L, L), 1)
    tri = row_i >= col_i
    a_cs = _dot_exact_rhs_left(tri, da)
    a_cs_t = a_cs.T
    dt_t = dt.T
    a_last = a_cs[L - 1:L, :]
    w_end = dt * jnp.exp(a_last - a_cs)
    chunk_decay = jnp.broadcast_to(jnp.exp(a_last), (8, LANES))
    e_mat = e_ref[...]
    w_exp = _dot_exact_rhs(w_end, e_mat)
    dec_exp = _dot_exact_rhs(chunk_decay, e_mat)[0:1, :]
    xd = (xs * w_exp).astype(BF16)

    for g in range(G):
        cg = cm[:, g * N:(g + 1) * N]
        bg = bm[:, g * N:(g + 1) * N]
        cb = _dot_nt(cg.astype(BF16), bg)
        for j in range(SSD_HPG):
            h = g * SSD_HPG + j
            col = jnp.broadcast_to(a_cs[:, h:h + 1], (L, L))
            row = a_cs_t[h:h + 1, :]
            lm = jnp.where(tri, jnp.exp(col - row), 0.0)
            m = (cb * lm * dt_t[h:h + 1, :]).astype(BF16)
            eac = (jnp.exp(col) * cg).astype(BF16)
            lhs = jnp.concatenate([m, eac], axis=1)
            rhs = jnp.concatenate([xs[:, h * P:(h + 1) * P].astype(BF16),
                                   state_ref[:, h * P:(h + 1) * P].astype(BF16)], axis=0)
            ybuf_ref[:, h * P:(h + 1) * P] = _dot(lhs, rhs)
        sl = slice(g * GW, (g + 1) * GW)
        state_ref[:, sl] = state_ref[:, sl] * dec_exp[:, sl] + _dot_tn(bg, xd[:, sl])

    y = ybuf_ref[...] + dskip_ref[...] * xs
    y = y * _silu(z_ref[0])
    for g in range(G):
        sl = slice(g * GW, (g + 1) * GW)
        yg = y[:, sl]
        rs = lax.rsqrt(jnp.mean(yg * yg, axis=-1, keepdims=True) + NORM_EPS)
        y_ref[0, :, sl] = (yg * rs * nw_ref[:, sl]).astype(y_ref.dtype)

    @pl.when(c == pl.num_programs(1) - 1)
    def _():
        st_ref[0] = state_ref[...]


def _dot_exact_rhs_left(mask, x):
    sel = jnp.where(mask, 1.0, 0.0).astype(BF16)
    a, b, c = _split3(x)
    return _dot(sel, a) + _dot(sel, b) + _dot(sel, c)


def _ssd_prompt(xbc, z, small, conv_w, conv_b, dt_bias, a_log, d_skip, norm_w):
    b, L, _ = xbc.shape
    nc = L // SSD_CHUNK
    pad = LANES - SSD_HEADS
    dtb = jnp.pad(dt_bias.astype(F32), (0, pad))[None, :]
    alog = jnp.pad(a_log.astype(F32), (0, pad))[None, :]
    dskip = jnp.repeat(d_skip.astype(F32), SSD_HEAD_DIM)[None, :]
    full = lambda shape: pl.BlockSpec(shape, lambda i, j: (0,) * len(shape))
    y, st = pl.pallas_call(
        _ssd_prompt_kernel,
        grid=(b, nc),
        in_specs=[pl.BlockSpec((1, SSD_CHUNK, SSD_CONV_CH), lambda i, j: (i, j, 0)),
                  pl.BlockSpec((1, SSD_CHUNK, SSD_D_INNER), lambda i, j: (i, j, 0)),
                  pl.BlockSpec((1, SSD_CHUNK, LANES), lambda i, j: (i, j, 0)),
                  full((SSD_CONV, SSD_CONV_CH)), full((1, SSD_CONV_CH)), full((1, LANES)), full((1, LANES)),
                  full((1, SSD_D_INNER)), full((1, SSD_D_INNER)), full((LANES, SSD_D_INNER))],
        out_specs=[pl.BlockSpec((1, SSD_CHUNK, SSD_D_INNER), lambda i, j: (i, j, 0)),
                   pl.BlockSpec((1, SSD_STATE, SSD_D_INNER), lambda i, j: (i, 0, 0))],
        out_shape=[jax.ShapeDtypeStruct((b, L, SSD_D_INNER), BF16),
                   jax.ShapeDtypeStruct((b, SSD_STATE, SSD_D_INNER), F32)],
        scratch_shapes=[pltpu.VMEM((SSD_CHUNK + 8, SSD_CONV_CH), F32),
                        pltpu.VMEM((SSD_STATE, SSD_D_INNER), F32),
                        pltpu.VMEM((SSD_CHUNK, SSD_D_INNER), F32)],
        compiler_params=_params("arbitrary", "arbitrary"),
        name="ssd_prompt",
    )(xbc, z, small, conv_w.astype(F32), conv_b.astype(F32)[None, :], dtb, alog, dskip,
      norm_w.astype(F32)[None, :], _head_expand_matrix())
    return y, st


HALF = CMP_BLOCK // 2
KVW = NSA_KV_HEADS * NSA_HEAD_DIM
ROWS_PER_DOT = 2


def _compress_weights(cmp_w1, cmp_b1, cmp_w2, cmp_b2, cmp_pe):
    w1 = cmp_w1.astype(BF16)
    w1ab = jnp.concatenate([w1[:, :HALF], w1[:, HALF:]], axis=-1)
    zero = jnp.zeros_like(w1ab)
    per_head = [jnp.stack([w1ab if h == g else zero for h in range(GPT)], axis=2) for g in range(GPT)]
    w1ab = jnp.stack(per_head, axis=1).reshape(2, GPT, HALF // ROWS_PER_DOT, ROWS_PER_DOT * LANES, 2 * CMP_HIDDEN)
    w1flat = w1.reshape(2, CMP_BLOCK * NSA_HEAD_DIM, CMP_HIDDEN)
    pe8 = jnp.broadcast_to(cmp_pe.reshape(2, 1, CMP_BLOCK * NSA_HEAD_DIM), (2, 8, CMP_BLOCK * NSA_HEAD_DIM))
    return (w1ab, w1flat, pe8.astype(F32), cmp_b1.astype(F32)[:, None, :], cmp_w2.astype(BF16),
            cmp_b2.astype(F32)[:, None, :])


GPT = LANES // NSA_HEAD_DIM


def _compress_body(load_strip, ns, w1ab_ref, w1flat_ref, pe_ref, b1_ref, w2_ref, b2_ref, bsh_ref, store):
    accs = [jnp.zeros((ns, 2 * CMP_HIDDEN), F32) for _ in range(GPT)]
    for c in range(HALF // ROWS_PER_DOT):
        lhs = jnp.concatenate([load_strip(c * ROWS_PER_DOT + i).astype(BF16) for i in range(ROWS_PER_DOT)], axis=1)
        for g in range(GPT):
            accs[g] = accs[g] + _dot(lhs, w1ab_ref[0, g, c])
    cpe = _dot(pe_ref[0].astype(BF16), w1flat_ref[0])[0:1, :] + b1_ref[0]
    bsh_ref[ns:ns + 8, :] = jnp.zeros((8, CMP_HIDDEN), F32)
    for g in range(GPT):
        bsh_ref[0:ns, :] = accs[g][:, CMP_HIDDEN:]
        hid = _silu(accs[g][:, :CMP_HIDDEN] + bsh_ref[1:ns + 1, :] + cpe)
        store(g, _dot(hid.astype(BF16), w2_ref[0]) + b2_ref[0])


def _compress_prompt_kernel(kv_ref, w1ab_ref, w1flat_ref, pe_ref, b1_ref, w2_ref, b2_ref, out_ref, bsh_ref):
    ns = out_ref.shape[3]

    def load_strip(l):
        return kv_ref[0, pl.ds(l, ns, stride=CMP_STRIDE), :]

    def store(g, val):
        out_ref[0, 0, g] = val

    _compress_body(load_strip, ns, w1ab_ref, w1flat_ref, pe_ref, b1_ref, w2_ref, b2_ref, bsh_ref, store)


def _compress_prompt(kv, cw):
    b, L, _ = kv.shape
    ns = L // CMP_STRIDE
    w1ab, w1flat, pe8, b1, w2, b2 = cw
    tiles = KVW // LANES
    per_which = lambda shape: pl.BlockSpec((1,) + shape, lambda i, w, t: (w,) + (0,) * len(shape))
    return pl.pallas_call(
        _compress_prompt_kernel,
        grid=(b, 2, tiles),
        in_specs=[pl.BlockSpec((1, L, LANES), lambda i, w, t: (i, 0, w * tiles + t)),
                  per_which((GPT, HALF // ROWS_PER_DOT, ROWS_PER_DOT * LANES, 2 * CMP_HIDDEN)),
                  per_which((CMP_BLOCK * NSA_HEAD_DIM, CMP_HIDDEN)),
                  per_which((8, CMP_BLOCK * NSA_HEAD_DIM)),
                  per_which((1, CMP_HIDDEN)),
                  per_which((CMP_HIDDEN, NSA_HEAD_DIM)),
                  per_which((1, NSA_HEAD_DIM))],
        out_specs=pl.BlockSpec((1, 1, GPT, ns, NSA_HEAD_DIM), lambda i, w, t: (i, w, t, 0, 0)),
        out_shape=jax.ShapeDtypeStruct((b, 2, NSA_KV_HEADS, ns, NSA_HEAD_DIM), F32),
        scratch_shapes=[pltpu.VMEM((ns + 8, CMP_HIDDEN), F32)],
        compiler_params=_params("arbitrary", "arbitrary", "arbitrary"),
        name="compress_prompt",
    )(kv, w1ab, w1flat, pe8, b1, w2, b2)


NEG = -1e30
SEL_TILE = 512
WIN_KEYS = WINDOW + NSA_NQ
WIN_CHUNK = 128


def _alibi_slopes():
    h = np.arange(1, NSA_HEADS + 1, dtype=np.float32)
    return (2.0 ** (-8.0 * h / NSA_HEADS)).astype(np.float32)


def _overlap_t(ncb, ns):
    i = np.arange(ncb)[None, :] * CMP_STRIDE
    m = np.arange(ns)[:, None] * SLC_BLOCK
    return ((i < m + SLC_BLOCK) & (i + CMP_BLOCK > m)).astype(np.float32)


def _masked_softmax_rows(s, mask):
    sm = jnp.where(mask, s, NEG)
    mx = jnp.max(sm, axis=-1, keepdims=True)
    p = jnp.where(mask, jnp.exp(sm - mx), 0.0)
    return p / jnp.maximum(jnp.sum(p, axis=-1, keepdims=True), 1e-30)


def _select_blocks_t(imp_t, qpos_row, v_ref, n_live):
    ns, nq = imp_t.shape
    m_i = lax.broadcasted_iota(jnp.int32, (ns, nq), 0)
    valid = m_i * SLC_BLOCK <= qpos_row
    cur = qpos_row // SLC_BLOCK
    forced = (m_i == 0) | (m_i == cur) | (m_i == cur - 1)
    del n_live
    bits = pltpu.bitcast(jnp.where(forced, jnp.inf, imp_t), jnp.int32)
    int_min = jnp.int32(-2 ** 31)
    key = jnp.where(valid, jnp.where(bits < 0, bits ^ jnp.int32(2 ** 31 - 1), bits), int_min)
    v_ref[...] = pltpu.bitcast(key, F32)
    k = float(SLC_TOPK)

    def count_ge(t):
        terms = [jnp.where(pltpu.bitcast(v_ref[8 * s:8 * s + 8, :], jnp.int32) >= t, 1.0, 0.0)
                 for s in range(ns // 8)]
        while len(terms) > 1:
            terms = [a + b for a, b in zip(terms[0::2], terms[1::2])] + terms[len(terms) & ~1:]
        part = terms[0]
        for shift in (4, 2, 1):
            part = part + pltpu.roll(part, shift, 0)
        return part

    def step(i, t):
        cand = t + (jnp.int32(1) << (30 - i))
        return jnp.where(count_ge(cand) >= k, cand, t)

    zero = jnp.zeros((8, nq), jnp.int32)
    t0 = jnp.where(count_ge(zero) >= k, zero, int_min)
    thr = lax.fori_loop(0, 31, step, t0)[0:1, :]
    above = jnp.where(key > thr, 1.0, 0.0)
    tied = jnp.where(key == thr, 1.0, 0.0)
    room = k - jnp.sum(above, axis=0, keepdims=True)
    lower = jnp.where(lax.broadcasted_iota(jnp.int32, (ns, ns), 1) < lax.broadcasted_iota(jnp.int32, (ns, ns), 0),
                      1.0, 0.0).astype(BF16)
    ties_before = _dot(lower, tied.astype(BF16))
    chosen = jnp.where(above > 0.0, 1.0, jnp.where(ties_before < room, tied, 0.0))
    return jnp.where(valid, chosen, 0.0)


BIG = 2.0 ** 60
POS_PERIOD = SEL_TILE


def _nsa_prompt_kernel(slopes_ref, q_ref, gate_ref, qpc_ref, kct_ref, vc_ref, kst_ref, va_ref, kwt_ref, vwp_ref,
                       pos_ref, blk_ref, ovl_ref, o_ref, v_ref, kaug_ref, kwaug_ref, m_ref, acc_ref,
                       s0_ref, s1_ref, sc_ref, sw_ref, flag_ref):
    g = pl.program_id(1)
    qi = pl.program_id(2)
    nq, dh, J = NSA_NQ, NSA_HEAD_DIM, NSA_HPG
    ns, ncb = ovl_ref.shape
    L = kst_ref.shape[2]
    start = qi * nq
    slopes = [slopes_ref[g * J + j] for j in range(J)]
    rows = [slice(j * nq, (j + 1) * nq) for j in range(J)]

    @pl.when(qi == 0)
    def _():
        kaug_ref[0:dh, :] = kst_ref[0].astype(BF16)
        kaug_ref[dh:LANES, :] = pos_ref[...]
        kaug_ref[LANES:, :] = blk_ref[...]
        kwaug_ref[0:dh, 0:WINDOW] = jnp.zeros((dh, WINDOW), BF16)
        kwaug_ref[dh:LANES, 0:WINDOW] = pos_ref[:, 0:WINDOW]
        kwaug_ref[0:dh, WINDOW:] = kwt_ref[0].astype(BF16)
        kwaug_ref[dh:LANES, WINDOW:] = pos_ref[...]

    m_ref[...] = jnp.full(m_ref.shape, NEG, F32)
    acc_ref[...] = jnp.zeros(acc_ref.shape, F32)
    qb = q_ref[0] * (dh ** -0.5)
    q_all = jnp.concatenate([qb[:, j * dh:(j + 1) * dh].astype(BF16) for j in range(J)], axis=0)
    q_pos = jnp.concatenate([q_all, qpc_ref[0].astype(BF16)], axis=1)
    gates = _sigmoid(gate_ref[0, 0])

    sc_ref[...] = _dot(q_all, kct_ref[0, 0])
    rel_end = (lax.broadcasted_iota(jnp.int32, (1, ncb), 1) * CMP_STRIDE + (CMP_BLOCK - 1)) - start
    mask_c = lax.broadcasted_iota(jnp.int32, (nq, ncb), 0) >= rel_end
    rel_endf = rel_end.astype(F32)
    vc = vc_ref[0, 0]
    o_c, p_sum = [], jnp.zeros((nq, ncb), F32)
    for j in range(J):
        sm = jnp.where(mask_c, sc_ref[rows[j], :] + slopes[j] * rel_endf, NEG)
        mx = jnp.max(sm, axis=-1, keepdims=True)
        e = jnp.exp(sm - jnp.where(mx > 0.5 * NEG, mx, 0.0))
        inv = 1.0 / jnp.maximum(jnp.sum(e, axis=-1, keepdims=True), 1e-30)
        p_sum = p_sum + e * inv
        o_c.append(_dot(e.astype(BF16), vc) * inv)

    ovl = ovl_ref[...]
    imp_t = sum(_dot_nt(ovl, part) for part in _split3(p_sum))
    qpos_row = start + lax.broadcasted_iota(jnp.int32, (ns, nq), 1)
    sel_t = _select_blocks_t(imp_t, qpos_row, v_ref, start // SLC_BLOCK + nq // SLC_BLOCK)
    not_q = ((sel_t.T - 1.0) * BIG).astype(BF16)
    q_aug = jnp.concatenate([q_pos, jnp.concatenate([not_q] * J, axis=0)], axis=1)

    tk = SEL_TILE
    causal = (lax.broadcasted_iota(jnp.int32, (nq, tk), 0) - lax.broadcasted_iota(jnp.int32, (nq, tk), 1))

    last = start // tk
    bpt = tk // SLC_BLOCK
    n_flags = ns // bpt
    v_ref[...] = sel_t
    for t in range(n_flags):
        flag_ref[t] = (jnp.max(v_ref[t * bpt:(t + 1) * bpt, :]) > 0.5).astype(jnp.int32)

    def active(t):
        return jnp.where(t < last, flag_ref[jnp.minimum(t, n_flags - 1)], (t == last).astype(jnp.int32)) > 0

    def scores(t, s_ref):
        s_ref[...] = _dot(q_aug, kaug_ref[:, pl.ds(pl.multiple_of(t * tk, tk), tk)])

    def absorb(t, s_ref):
        k0 = pl.multiple_of(t * tk, tk)
        vt = va_ref[0, 0, pl.ds(k0, tk), :]
        shift = (k0 - start).astype(F32)
        allowed = causal + (start - k0) >= 0
        for j in range(J):
            sj = jnp.where(allowed, s_ref[rows[j], :], -BIG)
            c = slopes[j] * shift
            m_old = m_ref[rows[j], :]
            m_new = jnp.maximum(m_old, jnp.max(sj, axis=-1, keepdims=True) + c)
            p = jnp.exp(sj - (m_new - c))
            acc_ref[rows[j], :] = acc_ref[rows[j], :] * jnp.exp(m_old - m_new) + _dot(p.astype(BF16), vt)
            m_ref[rows[j], :] = m_new

    def step(t, s_cur, s_nxt):
        a, b = active(t), active(t + 1)

        @pl.when(a & b)
        def _():
            scores(t + 1, s_nxt)
            absorb(t, s_cur)

        @pl.when(a & jnp.logical_not(b))
        def _():
            absorb(t, s_cur)

        @pl.when(jnp.logical_not(a) & b)
        def _():
            scores(t + 1, s_nxt)

    scores(0, s0_ref)

    def tile_pair(u, carry):
        step(2 * u, s0_ref, s1_ref)
        step(2 * u + 1, s1_ref, s0_ref)
        return carry

    lax.fori_loop(0, last // 2 + 1, tile_pair, 0)

    sw_ref[...] = _dot(q_pos, kwaug_ref[:, pl.ds(pl.multiple_of(start, nq), WIN_KEYS)])
    vw = vwp_ref[0, 0, pl.ds(pl.multiple_of(start, nq), WIN_KEYS), :]
    qrow = lax.broadcasted_iota(jnp.int32, (nq, WIN_KEYS), 0)
    wcol = lax.broadcasted_iota(jnp.int32, (nq, WIN_KEYS), 1)
    mask_w = (wcol > qrow) & (wcol - WINDOW <= qrow)
    chunk = lax.broadcasted_iota(jnp.int32, (1, WIN_KEYS), 1) // WIN_CHUNK
    p0 = start - WINDOW + chunk * WIN_CHUNK
    origin = (jnp.maximum(p0, 0) // POS_PERIOD) * POS_PERIOD - start
    for j in range(J):
        bias = jnp.where(p0 >= 0, slopes[j] * origin.astype(F32), -BIG)
        sm = jnp.where(mask_w, sw_ref[rows[j], :] + bias, NEG)
        p = jnp.exp(sm - jnp.max(sm, axis=-1, keepdims=True))
        o_w = _dot(p.astype(BF16), vw) / jnp.sum(p, axis=-1, keepdims=True)
        acc = acc_ref[rows[j], :]
        o_s = acc[:, 0:dh] / jnp.maximum(acc[:, dh:dh + 1], 1e-30)
        out = (gates[:, 3 * j:3 * j + 1] * o_c[j] + gates[:, 3 * j + 1:3 * j + 2] * o_s
               + gates[:, 3 * j + 2:3 * j + 3] * o_w)
        o_ref[0, :, j * dh:(j + 1) * dh] = out.astype(o_ref.dtype)


def _slope_columns():
    s = jnp.asarray(_alibi_slopes())
    pieces = jnp.stack([p.astype(F32) for p in _split3(s)], axis=1)
    cols = jnp.concatenate([pieces * SLC_BLOCK, pieces,
                            jnp.zeros((NSA_HEADS, NSA_HEAD_DIM - 6), F32)], axis=1)
    return jnp.repeat(cols, NSA_NQ, axis=0).reshape(NSA_KV_HEADS, NSA_HPG * NSA_NQ, NSA_HEAD_DIM)


def _position_rows(L):
    t = np.arange(L) % POS_PERIOD
    rows = np.zeros((NSA_HEAD_DIM, L), np.float32)
    rows[0:3] = t // SLC_BLOCK
    rows[3:6] = t % SLC_BLOCK
    return jnp.asarray(rows, BF16)


def _block_rows(ns, L):
    return jnp.asarray((np.arange(ns)[:, None] == np.arange(L)[None, :] // SLC_BLOCK).astype(np.float32), BF16)


def _nsa_prompt(q, gates_raw, kct, vc, kv_t, va, vwp):
    b, L, _ = q.shape
    G, dh, J = NSA_KV_HEADS, NSA_HEAD_DIM, NSA_HPG
    ncb, ns = L // CMP_STRIDE, L // SLC_BLOCK
    nqb = L // NSA_NQ
    per_bg = lambda shape: pl.BlockSpec((1, 1) + shape, lambda i, g, t, s: (i, g, 0, 0))
    const = lambda shape: pl.BlockSpec(shape, lambda i, g, t, s: (0,) * len(shape))
    kv_rows = lambda w: pl.BlockSpec((1, dh, L), lambda i, g, t, s: (i, w * G + g, 0))
    grid_spec = pltpu.PrefetchScalarGridSpec(
        num_scalar_prefetch=1,
        grid=(b, G, nqb),
        in_specs=[pl.BlockSpec((1, NSA_NQ, J * dh), lambda i, g, t, s: (i, t, g)),
                  pl.BlockSpec((1, 1, NSA_NQ, 3 * J), lambda i, g, t, s: (i, g, t, 0)),
                  pl.BlockSpec((1, J * NSA_NQ, dh), lambda i, g, t, s: (g, 0, 0)),
                  per_bg((dh, ncb)), per_bg((ncb, dh)), kv_rows(2), per_bg((L, LANES)),
                  kv_rows(4), per_bg((WINDOW + L, dh)),
                  const((dh, L)), const((ns, L)), const((ns, ncb))],
        out_specs=pl.BlockSpec((1, NSA_NQ, J * dh), lambda i, g, t, s: (i, t, g)),
        scratch_shapes=[pltpu.VMEM((ns, NSA_NQ), F32),
                        pltpu.VMEM((LANES + ns, L), BF16), pltpu.VMEM((LANES, WINDOW + L), BF16),
                        pltpu.VMEM((J * NSA_NQ, 1), F32), pltpu.VMEM((J * NSA_NQ, LANES), F32),
                        pltpu.VMEM((J * NSA_NQ, SEL_TILE), F32), pltpu.VMEM((J * NSA_NQ, SEL_TILE), F32),
                        pltpu.VMEM((J * NSA_NQ, ncb), F32), pltpu.VMEM((J * NSA_NQ, WIN_KEYS), F32),
                        pltpu.SMEM((ns // (SEL_TILE // SLC_BLOCK),), jnp.int32)],
    )
    return pl.pallas_call(
        _nsa_prompt_kernel,
        grid_spec=grid_spec,
        out_shape=jax.ShapeDtypeStruct((b, L, G * J * dh), BF16),
        compiler_params=_params("arbitrary", "arbitrary", "arbitrary"),
        name="nsa_prompt",
    )(jnp.asarray(_alibi_slopes()), q, gates_raw, _slope_columns(), kct, vc, kv_t, va, kv_t, vwp,
      _position_rows(L), _block_rows(ns, L), jnp.asarray(_overlap_t(ncb, ns), BF16))


def _nsa_prompt_branch(q, kv, kv_t, small, cw):
    b, L, _ = q.shape
    G, dh = NSA_KV_HEADS, NSA_HEAD_DIM
    cmp = _compress_prompt(kv, cw).astype(BF16)
    kct = cmp[:, 0].transpose(0, 1, 3, 2)
    vc = cmp[:, 1]
    rows = lambda w: kv[:, :, w * KVW:(w + 1) * KVW].astype(BF16).reshape(b, L, G, dh).transpose(0, 2, 1, 3)
    va = jnp.concatenate([rows(3), jnp.ones((b, G, L, 1), BF16), jnp.zeros((b, G, L, LANES - dh - 1), BF16)], axis=-1)
    vwp = jnp.pad(rows(5), ((0, 0), (0, 0), (WINDOW, 0), (0, 0)))
    gates_raw = small[:, :, SSD_HEADS:SSD_HEADS + 3 * NSA_HEADS].reshape(b, L, G, 3 * NSA_HPG).transpose(0, 2, 1, 3)
    return _nsa_prompt(q, gates_raw, kct, vc, kv_t, va, vwp)


def _layer_norm(v, g, b):
    mu = jnp.mean(v, axis=-1, keepdims=True)
    c = v - mu
    var = jnp.mean(c * c, axis=-1, keepdims=True)
    return c * lax.rsqrt(var + NORM_EPS) * g + b


def _route_t(logits_t, bias_col, v_ref):
    e_n, n = logits_t.shape
    gsz = e_n // N_ROUTE_GROUPS
    scores = _sigmoid(logits_t)
    biased = scores + bias_col
    grp = []
    for a in range(N_ROUTE_GROUPS):
        blk = biased[a * gsz:(a + 1) * gsz]
        m1 = jnp.max(blk, axis=0, keepdims=True)
        cnt = jnp.sum(jnp.where(blk == m1, 1.0, 0.0), axis=0, keepdims=True)
        m2 = jnp.max(jnp.where(blk < m1, blk, -jnp.inf), axis=0, keepdims=True)
        grp.append(m1 + jnp.where(cnt >= 2.0, m1, m2))
    pieces = []
    for a in range(N_ROUTE_GROUPS):
        rank = jnp.zeros((1, n), jnp.int32)
        for b in range(N_ROUTE_GROUPS):
            if b != a:
                ahead = (grp[b] > grp[a]) | ((grp[b] == grp[a]) & (b < a))
                rank = rank + jnp.where(ahead, 1, 0)
        pieces.append(jnp.where(rank < TOPK_ROUTE_GROUPS, biased[a * gsz:(a + 1) * gsz], -jnp.inf))
    masked = jnp.concatenate(pieces, axis=0)
    v_ref[...] = masked
    e_i = lax.broadcasted_iota(jnp.int32, (e_n, n), 0)

    def body(ep, rank):
        r = v_ref[pl.ds(ep, 1), :]
        tie = jnp.where(ep < e_i, 1.0, 0.0)
        return rank + jnp.where(r > masked, 1.0, jnp.where(r == masked, tie, 0.0))

    rank = lax.fori_loop(0, e_n, body, jnp.zeros((e_n, n), F32))
    picked = jnp.where(rank < TOP_K, scores, 0.0)
    return picked / jnp.sum(picked, axis=0, keepdims=True) * ROUTE_SCALE


def _finish_kernel(x_ref, sy_ref, ny_ref, gm_ref, wsd_ref, wnd_ref, wo_ref, bo_ref, g1_ref, b1_ref,
                   wrh_ref, wrl_ref, br_ref, h_ref, hb_ref, gt_ref, slot_ref, cnt_ref, v_ref):
    gate = _sigmoid(gm_ref[...])
    mixed = (gate[:, :D_MODEL] * _dot(sy_ref[...], wsd_ref[...])
             + gate[:, D_MODEL:] * _dot(ny_ref[...], wnd_ref[...]))
    o = _dot(mixed.astype(BF16), wo_ref[...]) + bo_ref[...]
    h = _layer_norm(ALPHA * x_ref[...] + o, g1_ref[...], b1_ref[...])
    h_ref[...] = h
    hb_ref[...] = h.astype(BF16)
    h_hi, h_lo = _split2(h)
    logits_t = _dot_nt(wrh_ref[...], h_hi) + _dot_nt(wrh_ref[...], h_lo) + _dot_nt(wrl_ref[...], h_hi)
    gates_t = _route_t(logits_t, br_ref[...], v_ref)
    gt_ref[...] = gates_t
    tm = gates_t.shape[1]
    routed = gates_t > 0.0
    earlier = jnp.where(lax.broadcasted_iota(jnp.int32, (tm, tm), 0) < lax.broadcasted_iota(jnp.int32, (tm, tm), 1),
                        1.0, 0.0).astype(BF16)
    ones = jnp.where(routed, 1.0, 0.0)
    slot_ref[...] = jnp.where(routed, _dot(ones.astype(BF16), earlier) + 1.0, 0.0)
    cnt_ref[0] = jnp.broadcast_to(jnp.sum(ones, axis=1, keepdims=True), (N_EXPERTS, LANES))


def _finish(x2d, ssd_y, nsa_y, gm, fw, tm):
    t = x2d.shape[0]
    wsd, wnd, wo, bo, g1, b1, wrh, wrl, br = fw
    row = lambda w: pl.BlockSpec((tm, w), lambda i: (i, 0))
    full = lambda a: pl.BlockSpec(a.shape, lambda i: (0,) * a.ndim)
    col = pl.BlockSpec((N_EXPERTS, tm), lambda i: (0, i))
    return pl.pallas_call(
        _finish_kernel,
        grid=(t // tm,),
        in_specs=[row(D_MODEL), row(SSD_D_INNER), row(D_MODEL), row(2 * D_MODEL)] + [full(a) for a in fw],
        out_specs=[row(D_MODEL), row(D_MODEL), col, col, pl.BlockSpec((1, N_EXPERTS, LANES), lambda i: (i, 0, 0))],
        out_shape=[jax.ShapeDtypeStruct((t, D_MODEL), F32), jax.ShapeDtypeStruct((t, D_MODEL), BF16),
                   jax.ShapeDtypeStruct((N_EXPERTS, t), F32), jax.ShapeDtypeStruct((N_EXPERTS, t), F32),
                   jax.ShapeDtypeStruct((t // tm, N_EXPERTS, LANES), F32)],
        scratch_shapes=[pltpu.VMEM((N_EXPERTS, tm), F32)],
        compiler_params=_params("arbitrary"),
        name="finish",
    )(x2d, ssd_y, nsa_y, gm, *fw)


def _finish_weights(w_ssd_down, w_nsa_down, w_out, b_out, ln1_g, ln1_b, w_router, b_router):
    wr_t = w_router.astype(F32).T
    wrh = wr_t.astype(BF16)
    wrl = (wr_t - wrh.astype(F32)).astype(BF16)
    return (w_ssd_down.astype(BF16), w_nsa_down.astype(BF16), w_out.astype(BF16), b_out.astype(F32)[None, :],
            ln1_g.astype(F32)[None, :], ln1_b.astype(F32)[None, :], wrh, wrl, b_router.astype(F32)[:, None])


MOE_R = 64
MOE_ALIGN = 16
MOE_BM = 512
MOE_TAIL = MOE_BM
MOE_ZERO = MOE_BM + MOE_R
MOE_GROUP = 8


def _swiglu(x, w1, w3):
    return _silu(_dot(x, w1)) * _dot(x, w3)


def _moe_blocks(t, tt):
    rows = (t * TOP_K + (t // tt) * N_EXPERTS * (MOE_ALIGN - 1) + N_EXPERTS * (MOE_R + MOE_BM - 1))
    return pl.cdiv(rows, MOE_BM) + MOE_TAIL // MOE_BM


def _moe_plan(cnt, nb):
    aligned = (cnt + MOE_ALIGN - 1) // MOE_ALIGN * MOE_ALIGN
    before = jnp.cumsum(aligned, axis=0) - aligned
    total = jnp.sum(aligned, axis=0)
    region = (total + MOE_R + MOE_BM - 1) // MOE_BM * MOE_BM
    region_end = jnp.cumsum(region)
    start = (region_end - region)[None, :] + before
    n_pass = jnp.maximum(jnp.max((cnt + MOE_R - 1) // MOE_R, axis=1), 1)
    blk_row0 = jnp.arange(nb, dtype=region_end.dtype) * MOE_BM
    blk_exp = jnp.minimum(jnp.sum(region_end[None, :] <= blk_row0[:, None], axis=1), N_EXPERTS - 1)
    i32 = lambda a: a.astype(jnp.int32)
    return (i32(start.reshape(-1)), i32(cnt.reshape(-1)), i32(n_pass), i32(blk_exp),
            i32(region_end[-1:] // MOE_BM), i32(jnp.maximum(region_end - MOE_ZERO, 0)))


def _chunk_hits(slot_ref, g, k):
    want = (lax.broadcasted_iota(jnp.int32, (MOE_R, 1), 0) + (k * MOE_R + 1)).astype(F32)
    return [slot_ref[e:e + 1, :] == want for e in range(g * MOE_GROUP, (g + 1) * MOE_GROUP)]


def _for_runs(cnt_ref, tile, k, fn):
    def body(e, c):
        @pl.when(cnt_ref[tile * N_EXPERTS + e] > k * MOE_R)
        def _():
            fn(e)
        return c
    lax.fori_loop(0, N_EXPERTS, body, 0)


def _moe_dispatch_kernel(start_ref, cnt_ref, npass_ref, tail_ref, nused_ref, hb_ref, slot_ref, xs_ref, buf_ref,
                         zero_ref, sem_ref):
    i = pl.program_id(0)
    E, R, GR = N_EXPERTS, MOE_R, MOE_GROUP * MOE_R
    half = i % 2
    h = hb_ref[...]

    @pl.when(i == 0)
    def _():
        zero_ref[...] = jnp.zeros(zero_ref.shape, zero_ref.dtype)

        def zero_copy(e):
            dst = pl.multiple_of(tail_ref[e], MOE_ALIGN)
            return pltpu.make_async_copy(zero_ref, xs_ref.at[pl.ds(dst, MOE_ZERO), :], sem_ref.at[2])

        for parity in range(2):
            def each(fn):
                def body(u, c):
                    fn(2 * u + parity)
                    return c
                lax.fori_loop(0, E // 2, body, 0)
            each(lambda e: zero_copy(e).start())
            each(lambda e: zero_copy(e).wait())

        def zero_block(j):
            return pltpu.make_async_copy(zero_ref.at[pl.ds(0, MOE_BM), :],
                                         xs_ref.at[pl.ds(pl.multiple_of(j * MOE_BM, MOE_BM), MOE_BM), :], sem_ref.at[2])

        def unused(fn):
            def body(j, c):
                fn(j)
                return c
            lax.fori_loop(nused_ref[0], xs_ref.shape[0] // MOE_BM, body, 0)

        unused(lambda j: zero_block(j).start())
        unused(lambda j: zero_block(j).wait())

    def fill(k):
        for g in range(E // MOE_GROUP):
            onehot = jnp.concatenate([jnp.where(hit, 1.0, 0.0).astype(BF16) for hit in _chunk_hits(slot_ref, g, k)],
                                     axis=0)
            buf_ref[half, g * GR:(g + 1) * GR, :] = _dot(onehot, h).astype(BF16)

    def copy(tile, e, k, hf):
        dst = pl.multiple_of(start_ref[tile * E + e] + k * R, MOE_ALIGN)
        return pltpu.make_async_copy(buf_ref.at[hf, pl.ds(pl.multiple_of(e * R, R), R), :],
                                     xs_ref.at[pl.ds(dst, R), :], sem_ref.at[hf])

    fill(0)

    @pl.when(i > 0)
    def _():
        k_prev = npass_ref[i - 1] - 1
        _for_runs(cnt_ref, i - 1, k_prev, lambda e: copy(i - 1, e, k_prev, 1 - half).wait())

    _for_runs(cnt_ref, i, 0, lambda e: copy(i, e, 0, half).start())

    def more(k, c):
        _for_runs(cnt_ref, i, k - 1, lambda e: copy(i, e, k - 1, half).wait())
        fill(k)
        _for_runs(cnt_ref, i, k, lambda e: copy(i, e, k, half).start())
        return c

    lax.fori_loop(1, npass_ref[i], more, 0)

    @pl.when(i == pl.num_programs(0) - 1)
    def _():
        k_last = npass_ref[i] - 1
        _for_runs(cnt_ref, i, k_last, lambda e: copy(i, e, k_last, half).wait())


def _moe_dispatch(hb, slots, plan, nb, tt):
    t = hb.shape[0]
    start, cnt, n_pass, _, n_used, tail = plan
    rows = nb * MOE_BM
    grid_spec = pltpu.PrefetchScalarGridSpec(
        num_scalar_prefetch=5,
        grid=(t // tt,),
        in_specs=[pl.BlockSpec((tt, D_MODEL), lambda i, *_: (i, 0)),
                  pl.BlockSpec((N_EXPERTS, tt), lambda i, *_: (0, i))],
        out_specs=pl.BlockSpec(memory_space=pl.ANY),
        scratch_shapes=[pltpu.VMEM((2, N_EXPERTS * MOE_R, D_MODEL), BF16), pltpu.VMEM((MOE_ZERO, D_MODEL), BF16),
                        pltpu.SemaphoreType.DMA((3,))],
    )
    return pl.pallas_call(
        _moe_dispatch_kernel,
        grid_spec=grid_spec,
        out_shape=jax.ShapeDtypeStruct((rows, D_MODEL), BF16),
        compiler_params=_params("arbitrary"),
        name="moe_dispatch",
    )(start, cnt, n_pass, tail, n_used, hb, slots)


def _moe_ffn_kernel(blk_exp_ref, nused_ref, x_ref, w1_ref, w3_ref, w2_ref, y_ref):
    del blk_exp_ref
    in_use = pl.program_id(0) < nused_ref[0]

    @pl.when(in_use)
    def _():
        a = _swiglu(x_ref[...], w1_ref[0], w3_ref[0])
        y_ref[...] = _dot(a.astype(BF16), w2_ref[0]).astype(y_ref.dtype)

    @pl.when(jnp.logical_not(in_use))
    def _():
        y_ref[...] = jnp.zeros(y_ref.shape, y_ref.dtype)


def _moe_ffn(xs, plan, nb, w1, w3, w2):
    _, _, _, blk_exp, n_used, _ = plan
    grid_spec = pltpu.PrefetchScalarGridSpec(
        num_scalar_prefetch=2,
        grid=(nb,),
        in_specs=[pl.BlockSpec((MOE_BM, D_MODEL), lambda j, be, nu: (jnp.minimum(j, nu[0] - 1), 0)),
                  pl.BlockSpec((1, D_MODEL, EXPERT_FF), lambda j, be, nu: (be[j], 0, 0)),
                  pl.BlockSpec((1, D_MODEL, EXPERT_FF), lambda j, be, nu: (be[j], 0, 0)),
                  pl.BlockSpec((1, EXPERT_FF, D_MODEL), lambda j, be, nu: (be[j], 0, 0))],
        out_specs=pl.BlockSpec((MOE_BM, D_MODEL), lambda j, be, nu: (j, 0)),
    )
    return pl.pallas_call(
        _moe_ffn_kernel,
        grid_spec=grid_spec,
        out_shape=jax.ShapeDtypeStruct(xs.shape, BF16),
        compiler_params=_params("arbitrary"),
        name="moe_ffn",
    )(blk_exp, n_used, xs, w1, w3, w2)


def _moe_combine_kernel(start_ref, npass_ref, hb_ref, h_ref, gate_ref, slot_ref, ys_ref,
                        ws1_ref, ws3_ref, ws2_ref, g2_ref, b2_ref, y_ref, buf_ref, sem_ref):
    i = pl.program_id(0)
    n = pl.num_programs(0)
    E, R, GR = N_EXPERTS, MOE_R, MOE_GROUP * MOE_R
    half = i % 2

    def copy(tile, e, k, hf):
        src = pl.multiple_of(start_ref[tile * E + e] + k * R, MOE_ALIGN)
        return pltpu.make_async_copy(ys_ref.at[pl.ds(src, R), :],
                                     buf_ref.at[hf, pl.ds(pl.multiple_of(e * R, R), R), :], sem_ref.at[hf])

    def all_runs(fn):
        def body(e, c):
            fn(e)
            return c
        lax.fori_loop(0, E, body, 0)

    @pl.when(i == 0)
    def _():
        all_runs(lambda e: copy(0, e, 0, 0).start())

    @pl.when(i + 1 < n)
    def _():
        all_runs(lambda e: copy(i + 1, e, 0, 1 - half).start())

    def gathered(k):
        acc = jnp.zeros(y_ref.shape, F32)
        for g in range(E // MOE_GROUP):
            rows = slice(g * GR, (g + 1) * GR)
            hits = _chunk_hits(slot_ref, g, k)
            gate = jnp.concatenate(
                [jnp.sum(jnp.where(hit, gate_ref[e:e + 1, :], 0.0), axis=1, keepdims=True)
                 for hit, e in zip(hits, range(g * MOE_GROUP, (g + 1) * MOE_GROUP))], axis=0)
            onehot = jnp.concatenate([jnp.where(hit, 1.0, 0.0).astype(BF16) for hit in hits], axis=0)
            weighted = (buf_ref[half, rows, :].astype(F32) * gate).astype(BF16)
            acc = acc + _dot_tn(onehot, weighted)
        return acc

    all_runs(lambda e: copy(i, e, 0, half).wait())
    routed = gathered(0)

    def more(k, acc):
        all_runs(lambda e: copy(i, e, k, half).start())
        all_runs(lambda e: copy(i, e, k, half).wait())
        return acc + gathered(k)

    routed = lax.fori_loop(1, npass_ref[i], more, routed)
    x = hb_ref[...]
    shared = _dot(_swiglu(x, ws1_ref[...], ws3_ref[...]).astype(BF16), ws2_ref[...])
    y_ref[...] = _layer_norm(ALPHA * h_ref[...] + (routed + shared), g2_ref[...], b2_ref[...])


def _moe_combine(hb, h, gates_t, slots, ys, plan, mw, tt):
    t = h.shape[0]
    start, _, n_pass, _, _, _ = plan
    _, _, _, ws1, ws3, ws2, g2, b2 = mw
    weights = (ws1, ws3, ws2, g2, b2)
    row = lambda w: pl.BlockSpec((tt, w), lambda i, *_: (i, 0))
    col = pl.BlockSpec((N_EXPERTS, tt), lambda i, *_: (0, i))
    full = lambda a: pl.BlockSpec(a.shape, lambda i, *_: (0,) * a.ndim)
    grid_spec = pltpu.PrefetchScalarGridSpec(
        num_scalar_prefetch=2,
        grid=(t // tt,),
        in_specs=[row(D_MODEL), row(D_MODEL), col, col, pl.BlockSpec(memory_space=pl.ANY)]
        + [full(a) for a in weights],
        out_specs=row(D_MODEL),
        scratch_shapes=[pltpu.VMEM((2, N_EXPERTS * MOE_R, D_MODEL), BF16), pltpu.SemaphoreType.DMA((2,))],
    )
    return pl.pallas_call(
        _moe_combine_kernel,
        grid_spec=grid_spec,
        out_shape=jax.ShapeDtypeStruct((t, D_MODEL), F32),
        compiler_params=_params("arbitrary"),
        name="moe_combine",
    )(start, n_pass, hb, h, gates_t, slots, ys, *weights)


def _moe_weights(w_e1, w_e3, w_e2, w_s1, w_s3, w_s2, ln2_g, ln2_b):
    return (w_e1.astype(BF16), w_e3.astype(BF16), w_e2.astype(BF16), w_s1.astype(BF16), w_s3.astype(BF16),
            w_s2.astype(BF16), ln2_g.astype(F32)[None, :], ln2_b.astype(F32)[None, :])


def _finish_and_moe(x2d, ssd_y, nsa_y, gm, fw, mw, tt):
    t = x2d.shape[0]
    h, hb, gates_t, slots, run_len = _finish(x2d, ssd_y, nsa_y, gm, fw, tt)
    nb = _moe_blocks(t, tt)
    plan = _moe_plan(run_len[:, :, 0].astype(jnp.int32), nb)
    xs = _moe_dispatch(hb, slots, plan, nb, tt)
    ys = _moe_ffn(xs, plan, nb, mw[0], mw[1], mw[2])
    return _moe_combine(hb, h, gates_t, slots, ys, plan, mw, tt)


def _ssd_sample_kernel(xbc_ref, sc_ref, z_ref, sm_ref, st_ref, cw_ref, cb_ref, dtb_ref, alog_ref, dskip_ref,
                       nw_ref, e_ref, y_ref, sto_ref, xdt_t_ref, dec_t_ref, b_ref, c_ref, xs_ref, yt_ref):
    i = pl.program_id(0)
    db = xbc_ref.shape[0]
    G, N = SSD_GROUPS, SSD_STATE
    GW = SSD_HPG * SSD_HEAD_DIM

    @pl.when(i == 0)
    def _():
        conv = cb_ref[...] + xbc_ref[...] * cw_ref[SSD_CONV - 1:SSD_CONV, :]
        for k in range(SSD_CONV - 1):
            conv = conv + sc_ref[k] * cw_ref[k:k + 1, :]
        act = _silu(conv)
        xs = act[:, :SSD_D_INNER]
        xs_ref[...] = xs
        b_ref[...] = act[:, SSD_D_INNER:SSD_D_INNER + G * N].astype(BF16)
        c_ref[...] = act[:, SSD_D_INNER + G * N:].astype(BF16)
        dt = _softplus(sm_ref[...] + dtb_ref[...])
        dec = jnp.exp(dt * (-jnp.exp(alog_ref[...])))
        e_mat = e_ref[...]
        xdt_t_ref[...] = (xs * _dot_exact_rhs(dt, e_mat)).T.astype(BF16)
        dec_t_ref[...] = _dot_exact_rhs(dec, e_mat).T
        yt_ref[...] = jnp.zeros_like(yt_ref)

    is_row = lax.broadcasted_iota(jnp.int32, (db, N), 0) == i
    onehot = jnp.where(is_row, 1.0, 0.0).astype(BF16)
    is_lane = lax.broadcasted_iota(jnp.int32, (GW, db), 1) == i
    for g in range(G):
        rows = slice(g * GW, (g + 1) * GW)
        b_g = jnp.where(is_row, b_ref[:, g * N:(g + 1) * N], jnp.zeros((), BF16))
        contrib = _dot(xdt_t_ref[rows, :], b_g)
        decay = _dot_exact_rhs(dec_t_ref[rows, :], onehot)
        new = st_ref[0, rows, :] * decay + contrib
        sto_ref[0, rows, :] = new
        y_all = _dot_nt(new.astype(BF16), c_ref[:, g * N:(g + 1) * N])
        yt_ref[rows, :] += jnp.where(is_lane, y_all, 0.0)

    @pl.when(i == pl.num_programs(0) - 1)
    def _():
        y = yt_ref[...].T + dskip_ref[...] * xs_ref[...]
        y = y * _silu(z_ref[...])
        for g in range(G):
            sl = slice(g * GW, (g + 1) * GW)
            yg = y[:, sl]
            rs = lax.rsqrt(jnp.mean(yg * yg, axis=-1, keepdims=True) + NORM_EPS)
            y_ref[:, sl] = (yg * rs * nw_ref[:, sl]).astype(y_ref.dtype)


def _ssd_sample(xbc, state_conv, z, small, state_ssm, conv_w, conv_b, dt_bias, a_log, d_skip, norm_w):
    db = xbc.shape[0]
    pad = LANES - SSD_HEADS
    dtb = jnp.pad(dt_bias.astype(F32), (0, pad))[None, :]
    alog = jnp.pad(a_log.astype(F32), (0, pad))[None, :]
    dskip = jnp.repeat(d_skip.astype(F32), SSD_HEAD_DIM)[None, :]
    sc = state_conv.astype(F32).transpose(1, 0, 2)
    st = state_ssm.reshape(db, SSD_D_INNER, SSD_STATE)
    args = (xbc, sc, z, small, st, conv_w.astype(F32), conv_b.astype(F32)[None, :], dtb, alog, dskip,
            norm_w.astype(F32)[None, :], _head_expand_matrix())
    full = lambda a: pl.BlockSpec(a.shape, lambda i: (0,) * a.ndim)
    st_spec = pl.BlockSpec((1, SSD_D_INNER, SSD_STATE), lambda i: (i, 0, 0))
    y, st_new = pl.pallas_call(
        _ssd_sample_kernel,
        grid=(db,),
        in_specs=[full(a) for a in args[:4]] + [st_spec] + [full(a) for a in args[5:]],
        out_specs=[pl.BlockSpec((db, SSD_D_INNER), lambda i: (0, 0)), st_spec],
        out_shape=[jax.ShapeDtypeStruct((db, SSD_D_INNER), BF16),
                   jax.ShapeDtypeStruct((db, SSD_D_INNER, SSD_STATE), F32)],
        scratch_shapes=[pltpu.VMEM((SSD_D_INNER, db), BF16), pltpu.VMEM((SSD_D_INNER, db), F32),
                        pltpu.VMEM((db, SSD_GROUPS * SSD_STATE), BF16),
                        pltpu.VMEM((db, SSD_GROUPS * SSD_STATE), BF16),
                        pltpu.VMEM((db, SSD_D_INNER), F32), pltpu.VMEM((SSD_D_INNER, db), F32)],
        compiler_params=_params("arbitrary"),
        name="ssd_sample",
    )(*args)
    return y, st_new.reshape(state_ssm.shape)


def _nsa_sample_kernel(n_pages, *refs):
    pt_ref = refs[0]
    page_refs = refs[1:1 + n_pages]
    (win_ref, q_ref, kvn_ref, gate_ref, slope_ref, ovl_ref, w1ab_ref, w1flat_ref, pe_ref, b1_ref, w2_ref,
     b2_ref, o_ref, wino_ref, cmp_rows_ref, ks_ref, vs_ref, kw_ref, vw_ref, bsh_ref, kc_ref, vc_ref) = refs[1 + n_pages:]
    del pt_ref
    H, dh, G = NSA_HEADS, NSA_HEAD_DIM, NSA_KV_HEADS
    past = n_pages * PAGE_SIZE
    ncb = past // CMP_STRIDE
    nsb = ovl_ref.shape[1]
    wbuf = win_ref.shape[3]
    tiles = KVW // LANES

    spp = PAGE_SIZE // CMP_STRIDE
    src_row = lax.broadcasted_iota(jnp.int32, (PAGE_SIZE, PAGE_SIZE), 1)
    regroup = jnp.where(lax.broadcasted_iota(jnp.int32, (PAGE_SIZE, PAGE_SIZE), 0)
                        == spp * (src_row % CMP_STRIDE) + src_row // CMP_STRIDE, 1.0, 0.0).astype(BF16)
    for p in range(n_pages):
        r = slice(p * PAGE_SIZE, (p + 1) * PAGE_SIZE)
        for w in range(2):
            rows_wp = _dot_nt(regroup, page_refs[p][0, w].astype(BF16))
            for t in range(tiles):
                for l in range(CMP_STRIDE):
                    cmp_rows_ref[w * tiles + t, l, p * spp:(p + 1) * spp, :] = (
                        rows_wp[l * spp:(l + 1) * spp, t * LANES:(t + 1) * LANES])
        ks_ref[:, r] = page_refs[p][0, 2].astype(BF16)
        vs_ref[:, r] = page_refs[p][0, 3].astype(BF16)
    new8 = jnp.broadcast_to(kvn_ref[0], (8, 6 * KVW))
    first = lax.broadcasted_iota(jnp.int32, (KVW, LANES), 1) == 0

    def new_col(w):
        return new8[:, w * KVW:(w + 1) * KVW].T[:, 0:1]

    put = lambda w: jnp.where(first, new_col(w), 0.0).astype(BF16)
    ks_ref[:, past:past + LANES] = put(2)
    vs_ref[:, past:past + LANES] = put(3)
    kw_ref[:, 0:wbuf] = win_ref[0, 0].astype(BF16)
    vw_ref[:, 0:wbuf] = win_ref[0, 1].astype(BF16)
    kw_ref[:, wbuf:wbuf + LANES] = put(4)
    vw_ref[:, wbuf:wbuf + LANES] = put(5)
    last = lax.broadcasted_iota(jnp.int32, (KVW, wbuf), 1) == wbuf - 1
    for w in range(2):
        wino_ref[0, w] = jnp.where(last, new_col(4 + w), pltpu.roll(win_ref[0, w], wbuf - 1, 1))

    def compress(w, dst):
        for t in range(tiles):
            def load_strip(l, t=t):
                return cmp_rows_ref[w * tiles + t, l]

            def store(g, val, t=t):
                c0 = (t * GPT + g) * dh
                dst[:, c0:c0 + dh] = val.astype(BF16)

            sub = lambda ref: ref.at[pl.ds(w, 1)]
            _compress_body(load_strip, ncb, sub(w1ab_ref), sub(w1flat_ref), sub(pe_ref), sub(b1_ref), sub(w2_ref),
                           sub(b2_ref), bsh_ref.at[w], store)

    compress(0, kc_ref)

    head_r = lax.broadcasted_iota(jnp.int32, (H, KVW), 0) // NSA_HPG
    lane_g = lax.broadcasted_iota(jnp.int32, (H, KVW), 1) // dh
    diag = head_r == lane_g
    q16 = q_ref[0] * (dh ** -0.5)
    q_bd = jnp.where(diag, jnp.concatenate([q16] * G, axis=1), 0.0).astype(BF16)
    slopes = slope_ref[...]
    gates = _sigmoid(gate_ref[0])

    def fold(o):
        o = jnp.where(diag, o, 0.0)
        return sum(o[:, g * dh:(g + 1) * dh] for g in range(G))

    d_c = past - (lax.broadcasted_iota(jnp.int32, (H, ncb), 1) * CMP_STRIDE + (CMP_BLOCK - 1))
    p_c = _masked_softmax_rows(_dot_nt(q_bd, kc_ref[...]) - slopes * d_c.astype(F32), d_c >= 0)
    compress(1, vc_ref)
    o_c = fold(_dot(p_c.astype(BF16), vc_ref[...]))

    p_grp = jnp.concatenate(
        [jnp.sum(p_c[g * NSA_HPG:(g + 1) * NSA_HPG], axis=0, keepdims=True) for g in range(G)]
        + [jnp.zeros((8 - G, ncb), F32)], axis=0)
    imp = sum(_dot(part, ovl_ref[...]) for part in _split3(p_grp))
    m_l = lax.broadcasted_iota(jnp.int32, (8, nsb), 1)
    valid = m_l * SLC_BLOCK <= past
    cur = past // SLC_BLOCK
    forced = (m_l == 0) | (m_l == cur) | (m_l == cur - 1)
    v = jnp.where(valid, jnp.where(forced, jnp.inf, imp), -jnp.inf)
    v_t = v.T
    mp_i = lax.broadcasted_iota(jnp.int32, (nsb, nsb), 0)
    m_i = lax.broadcasted_iota(jnp.int32, (nsb, nsb), 1)
    sel_rows = []
    for g in range(G):
        v_col = jnp.broadcast_to(v_t[:, g:g + 1], (nsb, nsb))
        v_row = jnp.broadcast_to(v[g:g + 1, :], (nsb, nsb))
        ahead = (v_col > v_row) | ((v_col == v_row) & (mp_i < m_i))
        rank = jnp.sum(jnp.where(ahead, 1.0, 0.0), axis=0, keepdims=True)
        sel = jnp.where((rank < SLC_TOPK) & valid[0:1, :], 1.0, 0.0)
        sel_rows.append(jnp.broadcast_to(sel, (NSA_HPG, nsb)))
    sel_h = jnp.concatenate(sel_rows, axis=0).astype(BF16)

    nk = past + LANES
    expand = jnp.where(lax.broadcasted_iota(jnp.int32, (nsb, nk), 0)
                       == lax.broadcasted_iota(jnp.int32, (nsb, nk), 1) // SLC_BLOCK, 1.0, 0.0).astype(BF16)
    d_s = past - lax.broadcasted_iota(jnp.int32, (H, nk), 1)
    ok = (_dot(sel_h, expand) > 0.5) & (d_s >= 0)
    p_s = _masked_softmax_rows(_dot(q_bd, ks_ref[...]) - slopes * d_s.astype(F32), ok)
    o_s = fold(_dot_nt(p_s.astype(BF16), vs_ref[...]))

    nw = wbuf + LANES
    d_w = wbuf - lax.broadcasted_iota(jnp.int32, (H, nw), 1)
    p_w = _masked_softmax_rows(_dot(q_bd, kw_ref[...]) - slopes * d_w.astype(F32), (d_w >= 0) & (d_w < WINDOW))
    o_w = fold(_dot_nt(p_w.astype(BF16), vw_ref[...]))

    o_ref[0] = (gates[:, 0:1] * o_c + gates[:, 1:2] * o_s + gates[:, 2:3] * o_w).astype(o_ref.dtype)


def _overlap_sample(ncb, nsb):
    i = np.arange(ncb)[:, None] * CMP_STRIDE
    m = np.arange(nsb)[None, :] * SLC_BLOCK
    ok = (i < m + SLC_BLOCK) & (i + CMP_BLOCK > m) & (np.arange(ncb)[:, None] < ncb - 1)
    return ok.astype(np.float32)


def _nsa_sample(q, kv_new, small, cache_kv_paged, page_table, cache_kv_win, cw):
    db = q.shape[0]
    H, dh, G = NSA_HEADS, NSA_HEAD_DIM, NSA_KV_HEADS
    n_pages = page_table.shape[1]
    past = n_pages * PAGE_SIZE
    wbuf = cache_kv_win.shape[1]
    ncb = past // CMP_STRIDE
    nsb = LANES * pl.cdiv(pl.cdiv(past + 1, SLC_BLOCK), LANES)
    pages = cache_kv_paged.transpose(0, 2, 3, 4, 1).reshape(cache_kv_paged.shape[0], 4, KVW, PAGE_SIZE)
    win = cache_kv_win.transpose(0, 2, 3, 4, 1).reshape(db, 2, KVW, wbuf)
    gates_raw = small[:, SSD_HEADS:SSD_HEADS + 3 * H].reshape(db, H, 3)
    slopes = jnp.asarray(_alibi_slopes())[:, None]
    ovl = jnp.asarray(_overlap_sample(ncb, nsb), BF16)
    consts = (slopes, ovl) + tuple(cw)
    per_seq = lambda shape: pl.BlockSpec((1,) + shape, lambda i, pt: (i,) + (0,) * len(shape))
    full = lambda a: pl.BlockSpec(a.shape, lambda i, pt: (0,) * a.ndim)
    page_spec = lambda p: pl.BlockSpec((1, 4, KVW, PAGE_SIZE), lambda i, pt: (pt[i, p], 0, 0, 0))
    grid_spec = pltpu.PrefetchScalarGridSpec(
        num_scalar_prefetch=1,
        grid=(db,),
        in_specs=[page_spec(p) for p in range(n_pages)]
        + [per_seq((2, KVW, wbuf)), per_seq((H, dh)), per_seq((1, 6 * KVW)), per_seq((H, 3))]
        + [full(a) for a in consts],
        out_specs=[per_seq((H, dh)), per_seq((2, KVW, wbuf))],
        scratch_shapes=[pltpu.VMEM((2 * KVW // LANES, CMP_STRIDE, ncb, LANES), F32),
                        pltpu.VMEM((KVW, past + LANES), BF16), pltpu.VMEM((KVW, past + LANES), BF16),
                        pltpu.VMEM((KVW, wbuf + LANES), BF16), pltpu.VMEM((KVW, wbuf + LANES), BF16),
                        pltpu.VMEM((2, ncb + 8, CMP_HIDDEN), F32),
                        pltpu.VMEM((ncb, KVW), BF16), pltpu.VMEM((ncb, KVW), BF16)],
    )
    out, win_new = pl.pallas_call(
        functools.partial(_nsa_sample_kernel, n_pages),
        grid_spec=grid_spec,
        out_shape=[jax.ShapeDtypeStruct((db, H, dh), BF16), jax.ShapeDtypeStruct((db, 2, KVW, wbuf), cache_kv_win.dtype)],
        compiler_params=_params("arbitrary"),
        name="nsa_sample",
    )(page_table.astype(jnp.int32), *([pages] * n_pages), win, q.reshape(db, H, dh), kv_new.reshape(db, 1, 6 * KVW),
      gates_raw, *consts)
    win_new = win_new.reshape(db, 2, G, dh, wbuf).transpose(0, 4, 1, 2, 3)
    return out.reshape(db, H * dh), win_new


def kernel(x_prompt, x_sample, cache_kv_paged, cache_kv_win, state_ssm, state_conv, page_table, w_in, b_in, conv_w, conv_b, dt_bias, a_log, d_skip, ssd_norm_w, cmp_w1, cmp_b1, cmp_w2, cmp_b2, cmp_pe, w_ssd_down, w_nsa_down, w_out, b_out, ln1_g, ln1_b, w_router, b_router, w_e1, w_e3, w_e2, w_s1, w_s3, w_s2, ln2_g, ln2_b):
    b, L, _ = x_prompt.shape
    db = x_sample.shape[0]
    G, dh = NSA_KV_HEADS, NSA_HEAD_DIM
    wbuf = cache_kv_win.shape[1]
    wp, bp = _pack_w_in(w_in, b_in)
    cw = _compress_weights(cmp_w1, cmp_b1, cmp_w2, cmp_b2, cmp_pe)
    fw = _finish_weights(w_ssd_down, w_nsa_down, w_out, b_out, ln1_g, ln1_b, w_router, b_router)
    mw = _moe_weights(w_e1, w_e3, w_e2, w_s1, w_s3, w_s2, ln2_g, ln2_b)
    ssd_w = (conv_w, conv_b, dt_bias, a_log, d_skip, ssd_norm_w)
    tm = min(256, b * L)
    per_seq = lambda a: a.reshape(b, L, a.shape[-1])

    xp = x_prompt.reshape(b * L, D_MODEL)
    z, xbc, q, kv, gm, small = _in_proj(xp, wp, bp, tm)
    ssd_y, st_t = _ssd_prompt(per_seq(xbc), per_seq(z), per_seq(small), *ssd_w)
    kv_t = _kv_channel_major(x_prompt, w_in, b_in, min(512, L))
    nsa_y = _nsa_prompt_branch(per_seq(q), per_seq(kv), kv_t, per_seq(small), cw)
    yp = _finish_and_moe(xp, ssd_y.reshape(b * L, -1), nsa_y.reshape(b * L, -1), gm, fw, mw, tm).reshape(b, L, D_MODEL)
    kv6_t = kv_t.reshape(b, 6, G, dh, L)
    kv_rows_p = kv6_t[:, :4].transpose(0, 4, 1, 2, 3)
    win_p = kv6_t[:, 4:, :, :, L - wbuf:].transpose(0, 4, 1, 2, 3)
    ssm_p = st_t.reshape(b, SSD_STATE, SSD_HEADS, SSD_HEAD_DIM).transpose(0, 2, 3, 1).astype(state_ssm.dtype)
    conv_p = per_seq(xbc)[:, L - (SSD_CONV - 1):]

    xs = x_sample.reshape(db, D_MODEL)
    z, xbc, q, kv, gm, small = _in_proj(xs, wp, bp, db)
    ssd_y, ssm_s = _ssd_sample(xbc, state_conv, z, small, state_ssm, *ssd_w)
    nsa_y, win_s = _nsa_sample(q, kv, small, cache_kv_paged, page_table, cache_kv_win, cw)
    ys = _finish_and_moe(xs, ssd_y, nsa_y, gm, fw, mw, db).reshape(db, 1, D_MODEL)
    kv6 = kv.reshape(db, 1, 6, G, dh)
    kv_rows_s = kv6[:, :, :4]
    conv_s = jnp.concatenate([state_conv[:, 1:].astype(xbc.dtype), xbc[:, None, :]], axis=1)
    return (yp, ys, kv_rows_p, kv_rows_s, win_p, win_s, ssm_p, ssm_s, conv_p, conv_s)
```
